```python
import math
import jax, jax.numpy as jnp
from jax import lax
import numpy as np

D_MODEL = 2048
BATCH = 4
SEQ = 8192
DEPTH = 2

N_A_LAYERS = DEPTH // 2
N_B_LAYERS = DEPTH - N_A_LAYERS
N_DENSE_LAYERS = (DEPTH + 1) // 2
N_MOE_LAYERS = DEPTH // 2

MIX_WIDTH = D_MODEL
MEM_WIDTH = D_MODEL // 4
MAIN_WIDTH = MIX_WIDTH - MEM_WIDTH
MEM_HEADS = 4
MEM_HEAD_DIM = MEM_WIDTH // MEM_HEADS
N_MEM = 256

CHUNK = 128
SG_GROUP_DIM = 128
SG_GROUPS = MAIN_WIDTH // SG_GROUP_DIM

DIFF_QK_DIM = 128
DIFF_V_DIM = 2 * DIFF_QK_DIM
DIFF_HEADS = MAIN_WIDTH // DIFF_V_DIM
Q_BLOCK = 128

D_FF_DENSE = 5632
N_EXPERTS = 8
TOP_K = 2
D_FF_EXPERT = 7168

ALPHA = (2.0 * DEPTH) ** 0.25
BETA = (8.0 * DEPTH) ** -0.25
LN_EPS = 1e-5

A_IN_COLS = 2 * MAIN_WIDTH + MEM_WIDTH
B_IN_COLS = 2 * DIFF_HEADS * DIFF_QK_DIM + MEM_WIDTH
KV_COLS = 2 * DIFF_HEADS * DIFF_QK_DIM + DIFF_HEADS * DIFF_V_DIM

kernel_name = "yoco_gmlp_diffattn_memxattn_moe_deepnorm"


def layer_norm(x, g, b):
    xf = x.astype(jnp.float32)
    mu = jnp.mean(xf, axis=-1, keepdims=True)
    var = jnp.mean(jnp.square(xf - mu), axis=-1, keepdims=True)
    return ((xf - mu) * lax.rsqrt(var + LN_EPS) * g + b).astype(x.dtype)


def rms_norm(x, g):
    xf = x.astype(jnp.float32)
    return (xf * lax.rsqrt(jnp.mean(jnp.square(xf), axis=-1, keepdims=True) + LN_EPS) * g).astype(x.dtype)


def chunked_spatial_gating(z, vnorm_g, vnorm_b, w_s, b_s):
    bsz, seq, _ = z.shape
    u, v = z[..., :MAIN_WIDTH], z[..., MAIN_WIDTH:]
    v = layer_norm(v, vnorm_g, vnorm_b)
    vc = v.reshape(bsz, seq // CHUNK, CHUNK, SG_GROUPS, SG_GROUP_DIM)
    causal = jnp.tril(jnp.ones((CHUNK, CHUNK), dtype=bool))
    w = jnp.where(causal[None], w_s, jnp.zeros_like(w_s))
    s = jnp.einsum('gts,bnsgd->bntgd', w, vc) + jnp.transpose(b_s)[:, :, None]
    return u * s.reshape(bsz, seq, MAIN_WIDTH)


def diff_attention(q, k, v, lam, lambda_init, subln_g):
    bsz, seq = q.shape[0], q.shape[1]
    nb = seq // Q_BLOCK
    scale = DIFF_QK_DIM ** -0.5
    qb = jnp.moveaxis(q.reshape(bsz, nb, Q_BLOCK, 2, DIFF_HEADS, DIFF_QK_DIM), 1, 0)
    kpos = jnp.arange(seq)

    def one_block(args):
        qi, i = args
        s = jnp.einsum('bqmhd,bkmhd->bmhqk', qi, k).astype(jnp.float32) * scale
        qpos = i * Q_BLOCK + jnp.arange(Q_BLOCK)
        mask = kpos[None, :] <= qpos[:, None]
        s = jnp.where(mask, s, jnp.finfo(jnp.float32).min)
        p = jax.nn.softmax(s, axis=-1)
        a = p[:, 0] - lam * p[:, 1]
        return jnp.einsum('bhqk,bkhd->bqhd', a.astype(v.dtype), v)

    out = lax.map(one_block, (qb, jnp.arange(nb)))
    out = jnp.moveaxis(out, 0, 1).reshape(bsz, seq, DIFF_HEADS, DIFF_V_DIM)
    out = rms_norm(out, subln_g) * (1.0 - lambda_init)
    return out.reshape(bsz, seq, MAIN_WIDTH)


def memory_attention(q_mem, mem, w_mem_kv):
    bsz, seq, _ = q_mem.shape
    kv = mem @ w_mem_kv
    k = kv[..., :MEM_WIDTH].reshape(bsz, -1, MEM_HEADS, MEM_HEAD_DIM)
    v = kv[..., MEM_WIDTH:].reshape(bsz, -1, MEM_HEADS, MEM_HEAD_DIM)
    q = q_mem.reshape(bsz, seq, MEM_HEADS, MEM_HEAD_DIM)
    s = jnp.einsum('bshd,bmhd->bhsm', q, k).astype(jnp.float32) * (MEM_HEAD_DIM ** -0.5)
    p = jax.nn.softmax(s, axis=-1).astype(v.dtype)
    return jnp.einsum('bhsm,bmhd->bshd', p, v).reshape(bsz, seq, MEM_WIDTH)


def swiglu(x, w_gate, w_up, w_down):
    return (jax.nn.silu(x @ w_gate) * (x @ w_up)) @ w_down


def moe_swiglu(x, w_router, w_gate, w_up, w_down):
    bsz, seq, d = x.shape
    xf = x.reshape(-1, d)
    logits = (xf @ w_router).astype(jnp.float32)
    top_v, top_i = lax.top_k(logits, TOP_K)
    top_w = jax.nn.softmax(top_v, axis=-1)
    gates = jnp.sum(jax.nn.one_hot(top_i, N_EXPERTS, dtype=jnp.float32) * top_w[..., None], axis=1)
    gates = gates.astype(x.dtype)
    y = jnp.zeros_like(xf)
    for e in range(N_EXPERTS):
        y = y + gates[:, e:e + 1] * swiglu(xf, w_gate[e], w_up[e], w_down[e])
    return y.reshape(bsz, seq, d)


def setup_inputs(seed: int = 0) -> dict:
    key = jax.random.key(seed)
    ks = jax.random.split(key, 32)
    f32 = jnp.float32
    d = D_MODEL

    def nrm(k, shape, scale):
        return jax.random.normal(k, shape, f32) * scale

    w_mem_kv = jnp.concatenate([
        nrm(ks[4], (DEPTH, d, MEM_WIDTH), d ** -0.5),
        nrm(ks[5], (DEPTH, d, MEM_WIDTH), d ** -0.5 * BETA)], axis=-1)
    shared_w_kv = jnp.concatenate([
        nrm(ks[11], (d, 2 * DIFF_HEADS * DIFF_QK_DIM), d ** -0.5),
        nrm(ks[12], (d, DIFF_HEADS * DIFF_V_DIM), d ** -0.5 * BETA)], axis=-1)
    return {
        "x": nrm(ks[0], (BATCH, SEQ, d), 1.0),
        "mem": nrm(ks[1], (BATCH, N_MEM, d), 1.0),
        "ln_g": 1.0 + nrm(ks[2], (DEPTH, 2, d), 0.02),
        "ln_b": nrm(ks[3], (DEPTH, 2, d), 0.02),
        "w_mix_out": nrm(ks[6], (DEPTH, MIX_WIDTH, d), MIX_WIDTH ** -0.5 * BETA),
        "w_mem_kv": w_mem_kv,
        "a_w_in": nrm(ks[7], (N_A_LAYERS, d, A_IN_COLS), d ** -0.5),
        "a_vnorm_g": 1.0 + nrm(ks[8], (N_A_LAYERS, MAIN_WIDTH), 0.02),
        "a_vnorm_b": nrm(ks[9], (N_A_LAYERS, MAIN_WIDTH), 0.02),
        "a_w_s": nrm(ks[10], (N_A_LAYERS, SG_GROUPS, CHUNK, CHUNK), CHUNK ** -0.5),
        "a_b_s": 1.0 + nrm(ks[13], (N_A_LAYERS, SG_GROUPS, CHUNK), 0.02),
        "shared_w_kv": shared_w_kv,
        "b_w_in": nrm(ks[14], (N_B_LAYERS, d, B_IN_COLS), d ** -0.5),
        "b_lambda_q1": nrm(ks[15], (N_B_LAYERS, DIFF_QK_DIM), 0.1),
        "b_lambda_k1": nrm(ks[16], (N_B_LAYERS, DIFF_QK_DIM), 0.1),
        "b_lambda_q2": nrm(ks[17], (N_B_LAYERS, DIFF_QK_DIM), 0.1),
        "b_lambda_k2": nrm(ks[18], (N_B_LAYERS, DIFF_QK_DIM), 0.1),
        "b_subln_g": 1.0 + nrm(ks[19], (N_B_LAYERS, DIFF_V_DIM), 0.02),
        "ffn_w_gate": nrm(ks[20], (N_DENSE_LAYERS, d, D_FF_DENSE), d ** -0.5),
        "ffn_w_up": nrm(ks[21], (N_DENSE_LAYERS, d, D_FF_DENSE), d ** -0.5 * BETA),
        "ffn_w_down": nrm(ks[22], (N_DENSE_LAYERS, D_FF_DENSE, d), D_FF_DENSE ** -0.5 * BETA),
        "moe_w_router": nrm(ks[23], (N_MOE_LAYERS, d, N_EXPERTS), d ** -0.5),
        "moe_w_gate": nrm(ks[24], (N_MOE_LAYERS, N_EXPERTS, d, D_FF_EXPERT), d ** -0.5),
        "moe_w_up": nrm(ks[25], (N_MOE_LAYERS, N_EXPERTS, d, D_FF_EXPERT), d ** -0.5 * BETA),
        "moe_w_down": nrm(ks[26], (N_MOE_LAYERS, N_EXPERTS, D_FF_EXPERT, d), D_FF_EXPERT ** -0.5 * BETA),
    }


def reference(x, mem, ln_g, ln_b, w_mix_out, w_mem_kv, a_w_in, a_vnorm_g, a_vnorm_b, a_w_s, a_b_s,
              shared_w_kv, b_w_in, b_lambda_q1, b_lambda_k1, b_lambda_q2, b_lambda_k2, b_subln_g,
              ffn_w_gate, ffn_w_up, ffn_w_down, moe_w_router, moe_w_gate, moe_w_up, moe_w_down):
    bsz, seq, _ = x.shape
    qk_cols = 2 * DIFF_HEADS * DIFF_QK_DIM
    k_sh = None
    v_sh = None
    for l in range(DEPTH):
        if l < N_A_LAYERS:
            a = l
            h = x @ a_w_in[a]
            z = jax.nn.gelu(h[..., :2 * MAIN_WIDTH])
            main = chunked_spatial_gating(z, a_vnorm_g[a], a_vnorm_b[a], a_w_s[a], a_b_s[a])
            q_mem = h[..., 2 * MAIN_WIDTH:]
        else:
            bi = l - N_A_LAYERS
            if bi == 0:
                kv = x @ shared_w_kv
                k_sh = kv[..., :qk_cols].reshape(bsz, seq, 2, DIFF_HEADS, DIFF_QK_DIM)
                v_sh = kv[..., qk_cols:].reshape(bsz, seq, DIFF_HEADS, DIFF_V_DIM)
            h = x @ b_w_in[bi]
            q = h[..., :qk_cols].reshape(bsz, seq, 2, DIFF_HEADS, DIFF_QK_DIM)
            lambda_init = 0.8 - 0.6 * math.exp(-0.3 * l)
            lam = (jnp.exp(jnp.sum(b_lambda_q1[bi] * b_lambda_k1[bi]).astype(jnp.float32))
                   - jnp.exp(jnp.sum(b_lambda_q2[bi] * b_lambda_k2[bi]).astype(jnp.float32))
                   + lambda_init)
            main = diff_attention(q, k_sh, v_sh, lam, lambda_init, b_subln_g[bi])
            q_mem = h[..., qk_cols:]
        mem_out = memory_attention(q_mem, mem, w_mem_kv[l])
        mix = jnp.concatenate([main, mem_out], axis=-1) @ w_mix_out[l]
        x = layer_norm(ALPHA * x + mix, ln_g[l, 0], ln_b[l, 0])
        if l % 2 == 0:
            j = l // 2
            f = swiglu(x, ffn_w_gate[j], ffn_w_up[j], ffn_w_down[j])
        else:
            j = l // 2
            f = moe_swiglu(x, moe_w_router[j], moe_w_gate[j], moe_w_up[j], moe_w_down[j])
        x = layer_norm(ALPHA * x + f, ln_g[l, 1], ln_b[l, 1])
    return x
```

```python
import functools
import math

import numpy as np
import jax
import jax.numpy as jnp
from jax import lax
from jax.experimental import pallas as pl
from jax.experimental.pallas import tpu as pltpu

BF16 = jnp.bfloat16
F32 = jnp.float32

D_MODEL = 2048
MEM_WIDTH = 512
MAIN_WIDTH = 1536
MEM_HEADS = 4
MEM_HEAD_DIM = 128
CHUNK = 128
SG_GROUPS = 12
QK_DIM = 128
V_DIM = 256
DIFF_HEADS = 6
N_EXPERTS = 8
DEPTH = 2
ALPHA = (2.0 * DEPTH) ** 0.25
LN_EPS = 1e-5
LANES = 128
NEG_BIG = -1e30
MIB = 1024 * 1024


def _tiles(n_tokens, seq):
    def fit(pref, total):
        t = min(pref, total)
        while total % t:
            t //= 2
        return t
    return dict(
        mixer=fit(512, seq),
        proj=fit(512, n_tokens),
        ffn=fit(512, n_tokens),
        ffn_f=512,
        mm=fit(1024, n_tokens),
        mm_n=1024,
        attn_q=fit(512, seq),
        attn_k=fit(512, seq),
        mem=fit(512, seq),
        route=fit(512, n_tokens),
        moe=fit(1024, n_tokens),
        moe_f=512,
        comb=fit(256, n_tokens),
    )


def _params(sem, vmem_mib):
    return pltpu.CompilerParams(dimension_semantics=sem, vmem_limit_bytes=vmem_mib * MIB)


def _const_spec(shape, index_map):
    return pl.BlockSpec(shape, index_map, pipeline_mode=pl.Buffered(1))


def _dot(a, b):
    return jnp.dot(a, b, preferred_element_type=F32)


def _dot_nt(a, b):
    return lax.dot_general(a, b, (((1,), (1,)), ((), ())), preferred_element_type=F32)


def _layer_norm(v, g, b):
    mu = jnp.mean(v, axis=-1, keepdims=True)
    c = v - mu
    var = jnp.mean(c * c, axis=-1, keepdims=True)
    return c * lax.rsqrt(var + LN_EPS) * g + b


def _gelu_tanh(v):
    return 0.5 * v * (1.0 + jnp.tanh(math.sqrt(2.0 / math.pi) * (v + 0.044715 * (v * v * v))))


def _silu(v):
    return v / (1.0 + jnp.exp(-v))


def _mem_attention(q, kv_ref, o_ref, col0):
    scale = MEM_HEAD_DIM ** -0.5
    for h in range(MEM_HEADS):
        lo = h * MEM_HEAD_DIM
        qh = q[:, lo:lo + MEM_HEAD_DIM].astype(BF16)
        kh = kv_ref[:, lo:lo + MEM_HEAD_DIM]
        vh = kv_ref[:, MEM_WIDTH + lo:MEM_WIDTH + lo + MEM_HEAD_DIM]
        s = _dot_nt(qh, kh) * scale
        e = jnp.exp(s - jnp.max(s, axis=-1, keepdims=True))
        o = _dot(e.astype(BF16), vh) / jnp.sum(e, axis=-1, keepdims=True)
        o_ref[:, col0 + lo:col0 + lo + MEM_HEAD_DIM] = o.astype(o_ref.dtype)


def _memkv_kernel(mem_ref, w_ref, o_ref):
    o_ref[...] = _dot(mem_ref[...].astype(BF16), w_ref[...]).astype(o_ref.dtype)


def _memkv(mem, w_mem_kv_bf):
    bsz, n_mem, d = mem.shape
    depth, _, cols = w_mem_kv_bf.shape
    return pl.pallas_call(
        _memkv_kernel,
        grid=(depth, bsz),
        in_specs=[pl.BlockSpec((None, n_mem, d), lambda l, b: (b, 0, 0)),
                  pl.BlockSpec((None, d, cols), lambda l, b: (l, 0, 0))],
        out_specs=pl.BlockSpec((None, None, n_mem, cols), lambda l, b: (l, b, 0, 0)),
        out_shape=jax.ShapeDtypeStruct((depth, bsz, n_mem, cols), BF16),
        compiler_params=_params(("arbitrary", "arbitrary"), 32),
        name="memkv",
    )(mem, w_mem_kv_bf)


def _mixer_a_kernel(x_ref, w_ref, vg_ref, vb_ref, ws_ref, bs_ref, kv_ref, o_ref):
    tm = x_ref.shape[0]
    xb = x_ref[...].astype(BF16)
    v = _gelu_tanh(_dot(xb, w_ref[:, MAIN_WIDTH:2 * MAIN_WIDTH]))
    vn = _layer_norm(v, vg_ref[...], vb_ref[...]).astype(BF16)
    u = _gelu_tanh(_dot(xb, w_ref[:, :MAIN_WIDTH]))
    row = lax.broadcasted_iota(jnp.int32, (CHUNK, CHUNK), 0)
    col = lax.broadcasted_iota(jnp.int32, (CHUNK, CHUNK), 1)
    causal = col <= row
    for g in range(SG_GROUPS):
        wg = jnp.where(causal, ws_ref[g], 0.0).astype(BF16)
        bias = bs_ref[:, g:g + 1]
        for c in range(tm // CHUNK):
            rows = slice(c * CHUNK, (c + 1) * CHUNK)
            cols = slice(g * CHUNK, (g + 1) * CHUNK)
            s = _dot(wg, vn[rows, cols]) + bias
            o_ref[rows, cols] = (u[rows, cols] * s).astype(o_ref.dtype)
    q_mem = _dot(xb, w_ref[:, 2 * MAIN_WIDTH:])
    _mem_attention(q_mem, kv_ref, o_ref, MAIN_WIDTH)


def _mixer_a(x, w_in_bf, vnorm_g, vnorm_b, w_s, b_s_t, memkv, seq, tm):
    n, d = x.shape
    in_cols = w_in_bf.shape[1]
    return pl.pallas_call(
        _mixer_a_kernel,
        grid=(n // tm,),
        in_specs=[pl.BlockSpec((tm, d), lambda i: (i, 0)),
                  _const_spec((d, in_cols), lambda i: (0, 0)),
                  _const_spec((1, MAIN_WIDTH), lambda i: (0, 0)),
                  _const_spec((1, MAIN_WIDTH), lambda i: (0, 0)),
                  _const_spec((SG_GROUPS, CHUNK, CHUNK), lambda i: (0, 0, 0)),
                  _const_spec((CHUNK, SG_GROUPS), lambda i: (0, 0)),
                  pl.BlockSpec((None, None) + memkv.shape[2:], lambda i: (0, (i * tm) // seq, 0, 0))],
        out_specs=pl.BlockSpec((tm, d), lambda i: (i, 0)),
        out_shape=jax.ShapeDtypeStruct((n, d), BF16),
        compiler_params=_params(("arbitrary",), 56),
        name="mixer_a",
    )(x, w_in_bf, vnorm_g, vnorm_b, w_s, b_s_t, memkv)


def _proj_ln_kernel(a1_ref, a2_ref, w_ref, x_ref, g_ref, b_ref, o32_ref, o16_ref):
    mix = _dot(a1_ref[...], w_ref[:MAIN_WIDTH, :]) + _dot(a2_ref[...], w_ref[MAIN_WIDTH:, :])
    y = _layer_norm(ALPHA * x_ref[...] + mix, g_ref[...], b_ref[...])
    o32_ref[...] = y
    o16_ref[...] = y.astype(BF16)


def _proj_ln(a_main, main_blk, a_mem, mem_blk, w_bf, x, g, b, tm):
    n, d = x.shape
    return pl.pallas_call(
        _proj_ln_kernel,
        grid=(n // tm,),
        in_specs=[pl.BlockSpec((tm, MAIN_WIDTH), lambda i: (i, main_blk)),
                  pl.BlockSpec((tm, MEM_WIDTH), lambda i: (i, mem_blk)),
                  _const_spec((d, d), lambda i: (0, 0)),
                  pl.BlockSpec((tm, d), lambda i: (i, 0)),
                  _const_spec((1, d), lambda i: (0, 0)),
                  _const_spec((1, d), lambda i: (0, 0))],
        out_specs=[pl.BlockSpec((tm, d), lambda i: (i, 0)),
                   pl.BlockSpec((tm, d), lambda i: (i, 0))],
        out_shape=[jax.ShapeDtypeStruct((n, d), F32), jax.ShapeDtypeStruct((n, d), BF16)],
        compiler_params=_params(("arbitrary",), 48),
        name="proj_ln",
    )(a_main, a_mem, w_bf, x, g, b)


def _ffn_ln_kernel(x_ref, xb_ref, wg_ref, wu_ref, wd_ref, g_ref, b_ref, o32_ref, o16_ref, acc_ref):
    j = pl.program_id(1)

    @pl.when(j == 0)
    def _():
        acc_ref[...] = jnp.zeros_like(acc_ref)

    xb = xb_ref[...]
    h = (_silu(_dot(xb, wg_ref[...])) * _dot(xb, wu_ref[...])).astype(BF16)
    acc_ref[...] += _dot(h, wd_ref[...])

    @pl.when(j == pl.num_programs(1) - 1)
    def _():
        y = _layer_norm(ALPHA * x_ref[...] + acc_ref[...], g_ref[...], b_ref[...])
        o32_ref[...] = y
        o16_ref[...] = y.astype(BF16)


def _ffn_ln(x, xb, wg_bf, wu_bf, wd_bf, g, b, tm, tf):
    n, d = x.shape
    f = wg_bf.shape[1]
    return pl.pallas_call(
        _ffn_ln_kernel,
        grid=(n // tm, f // tf),
        in_specs=[pl.BlockSpec((tm, d), lambda i, j: (i, 0)),
                  pl.BlockSpec((tm, d), lambda i, j: (i, 0)),
                  pl.BlockSpec((d, tf), lambda i, j: (0, j)),
                  pl.BlockSpec((d, tf), lambda i, j: (0, j)),
                  pl.BlockSpec((tf, d), lambda i, j: (j, 0)),
                  _const_spec((1, d), lambda i, j: (0, 0)),
                  _const_spec((1, d), lambda i, j: (0, 0))],
        out_specs=[pl.BlockSpec((tm, d), lambda i, j: (i, 0)),
                   pl.BlockSpec((tm, d), lambda i, j: (i, 0))],
        out_shape=[jax.ShapeDtypeStruct((n, d), F32), jax.ShapeDtypeStruct((n, d), BF16)],
        scratch_shapes=[pltpu.VMEM((tm, d), F32)],
        compiler_params=_params(("arbitrary", "arbitrary"), 56),
        name="ffn_ln",
    )(x, xb, wg_bf, wu_bf, wd_bf, g, b)


def _matmul_kernel(x_ref, w_ref, s_ref, o_ref):
    o_ref[...] = (_dot(x_ref[...], w_ref[...]) * s_ref[...]).astype(o_ref.dtype)


def _matmul_colscale(xb, w_bf, col_scale, tm, tn):
    n, d = xb.shape
    cols = w_bf.shape[1]
    return pl.pallas_call(
        _matmul_kernel,
        grid=(n // tm, cols // tn),
        in_specs=[pl.BlockSpec((tm, d), lambda i, j: (i, 0)),
                  pl.BlockSpec((d, tn), lambda i, j: (0, j)),
                  pl.BlockSpec((1, tn), lambda i, j: (0, j))],
        out_specs=pl.BlockSpec((tm, tn), lambda i, j: (i, j)),
        out_shape=jax.ShapeDtypeStruct((n, cols), BF16),
        compiler_params=_params(("arbitrary", "arbitrary"), 48),
        name="in_proj_b",
    )(xb, w_bf, col_scale)


def _diff_attn_kernel(qi_tab, kj_tab, q1_ref, q2_ref, k1_ref, k2_ref, v_ref,
                      lq1_ref, lk1_ref, lq2_ref, lk2_ref, sg_ref, o_ref,
                      m1_ref, l1_ref, a1_ref, m2_ref, l2_ref, a2_ref, *, lambda_init):
    tq, tk = q1_ref.shape[0], k1_ref.shape[0]
    p = pl.program_id(2)
    qi = qi_tab[p]
    kj = kj_tab[p]

    @pl.when(kj == 0)
    def _():
        for m_ref, l_ref, a_ref in ((m1_ref, l1_ref, a1_ref), (m2_ref, l2_ref, a2_ref)):
            m_ref[...] = jnp.full_like(m_ref, NEG_BIG)
            l_ref[...] = jnp.zeros_like(l_ref)
            a_ref[...] = jnp.zeros_like(a_ref)

    def step(masked):
        v = v_ref[...]
        if masked:
            row = qi * tq + lax.broadcasted_iota(jnp.int32, (tq, tk), 0)
            col = kj * tk + lax.broadcasted_iota(jnp.int32, (tq, tk), 1)
            keep = col <= row
        for q_ref, k_ref, m_ref, l_ref, a_ref in ((q1_ref, k1_ref, m1_ref, l1_ref, a1_ref),
                                                  (q2_ref, k2_ref, m2_ref, l2_ref, a2_ref)):
            s = _dot_nt(q_ref[...], k_ref[...])
            if masked:
                s = jnp.where(keep, s, NEG_BIG)
            m_old = m_ref[...]
            m_new = jnp.maximum(m_old, jnp.max(s, axis=-1, keepdims=True))
            alpha = jnp.exp(m_old - m_new)
            e = jnp.exp(s - m_new)
            l_ref[...] = alpha * l_ref[...] + jnp.sum(e, axis=-1, keepdims=True)
            a_ref[...] = alpha * a_ref[...] + _dot(e.astype(BF16), v)
            m_ref[...] = m_new

    on_diagonal = (kj + 1) * tk - 1 > qi * tq
    pl.when(on_diagonal)(lambda: step(True))
    pl.when(jnp.logical_not(on_diagonal))(lambda: step(False))

    @pl.when(kj == ((qi + 1) * tq - 1) // tk)
    def _():
        lam = (jnp.exp(jnp.sum(lq1_ref[...] * lk1_ref[...], axis=-1, keepdims=True))
               - jnp.exp(jnp.sum(lq2_ref[...] * lk2_ref[...], axis=-1, keepdims=True))
               + lambda_init)
        o = a1_ref[...] / l1_ref[...] - lam * (a2_ref[...] / l2_ref[...])
        o = o * lax.rsqrt(jnp.mean(o * o, axis=-1, keepdims=True) + LN_EPS) * sg_ref[...]
        o_ref[...] = (o * (1.0 - lambda_init)).astype(o_ref.dtype)


def _diff_attn(hk, lq1, lk1, lq2, lk2, subln_g, bsz, seq, lambda_init, tq, tk):
    n = hk.shape[0]
    nq = seq // tq
    pairs = [(qi, kj) for qi in range(nq) for kj in range(((qi + 1) * tq - 1) // tk + 1)]
    qi_tab = jnp.asarray(np.array([p[0] for p in pairs], np.int32))
    kj_tab = jnp.asarray(np.array([p[1] for p in pairs], np.int32))
    nqb, nkb = seq // tq, seq // tk
    k_blk0 = (MAIN_WIDTH + MEM_WIDTH) // QK_DIM
    v_blk0 = (2 * MAIN_WIDTH + MEM_WIDTH) // V_DIM
    qspec = lambda off: pl.BlockSpec((tq, QK_DIM), lambda b, h, p, qt, kt: (b * nqb + qt[p], off + h))
    kspec = lambda off: pl.BlockSpec((tk, QK_DIM), lambda b, h, p, qt, kt: (b * nkb + kt[p], off + h))
    vec = lambda w: pl.BlockSpec((1, w), lambda b, h, p, qt, kt: (0, 0))
    grid_spec = pltpu.PrefetchScalarGridSpec(
        num_scalar_prefetch=2,
        grid=(bsz, DIFF_HEADS, len(pairs)),
        in_specs=[qspec(0), qspec(DIFF_HEADS), kspec(k_blk0), kspec(k_blk0 + DIFF_HEADS),
                  pl.BlockSpec((tk, V_DIM), lambda b, h, p, qt, kt: (b * nkb + kt[p], v_blk0 + h)),
                  vec(QK_DIM), vec(QK_DIM), vec(QK_DIM), vec(QK_DIM), vec(V_DIM)],
        out_specs=pl.BlockSpec((tq, V_DIM), lambda b, h, p, qt, kt: (b * nqb + qt[p], h)),
        scratch_shapes=[pltpu.VMEM((tq, 1), F32), pltpu.VMEM((tq, 1), F32), pltpu.VMEM((tq, V_DIM), F32),
                        pltpu.VMEM((tq, 1), F32), pltpu.VMEM((tq, 1), F32), pltpu.VMEM((tq, V_DIM), F32)],
    )
    return pl.pallas_call(
        functools.partial(_diff_attn_kernel, lambda_init=lambda_init),
        grid_spec=grid_spec,
        out_shape=jax.ShapeDtypeStruct((n, MAIN_WIDTH), BF16),
        compiler_params=_params(("arbitrary", "arbitrary", "arbitrary"), 48),
        name="diff_attn",
    )(qi_tab, kj_tab, hk, hk, hk, hk, hk, lq1, lk1, lq2, lk2, subln_g)


def _mem_attn_kernel(q_ref, kv_ref, o_ref):
    _mem_attention(q_ref[...].astype(F32), kv_ref, o_ref, 0)


def _mem_attn_b(hk, memkv, seq, tm):
    n = hk.shape[0]
    return pl.pallas_call(
        _mem_attn_kernel,
        grid=(n // tm,),
        in_specs=[pl.BlockSpec((tm, MEM_WIDTH), lambda i: (i, MAIN_WIDTH // MEM_WIDTH)),
                  pl.BlockSpec((None, None) + memkv.shape[2:], lambda i: (1, (i * tm) // seq, 0, 0))],
        out_specs=pl.BlockSpec((tm, MEM_WIDTH), lambda i: (i, 0)),
        out_shape=jax.ShapeDtypeStruct((n, MEM_WIDTH), BF16),
        compiler_params=_params(("arbitrary",), 32),
        name="mem_attn_b",
    )(hk, memkv)


def _lane_cumsum(v, lane):
    for shift in (1, 2, 4):
        v = v + jnp.where(lane >= shift, pltpu.roll(v, shift, 1), 0.0)
    return v


def _route_kernel(x_ref, wr_ref, pos_ref, gate_ref, tile_e_ref, cnt_ref, base_ref, start_ref, *, moe_tile):
    phase = pl.program_id(0)
    i = pl.program_id(1)
    tm = x_ref.shape[0]
    lane = lax.broadcasted_iota(jnp.int32, (tm, LANES), 1).astype(F32)

    x = x_ref[...]
    xh = x.astype(BF16)
    xl = (x - xh.astype(F32)).astype(BF16)
    w = wr_ref[...]
    wh = w.astype(BF16)
    wl = (w - wh.astype(F32)).astype(BF16)
    logits = _dot(xh, wh) + (_dot(xh, wl) + _dot(xl, wh))
    logits = jnp.where(lane < N_EXPERTS, logits, NEG_BIG)

    v1 = jnp.max(logits, axis=-1, keepdims=True)
    i1 = jnp.min(jnp.where(logits == v1, lane, float(LANES)), axis=-1, keepdims=True)
    rest = jnp.where(lane == i1, NEG_BIG, logits)
    v2 = jnp.max(rest, axis=-1, keepdims=True)
    i2 = jnp.min(jnp.where(rest == v2, lane, float(LANES)), axis=-1, keepdims=True)
    sel = jnp.logical_or(lane == i1, lane == i2)
    tile_cnt = jnp.sum(sel.astype(F32), axis=0, keepdims=True)

    @pl.when(jnp.logical_and(phase == 0, i == 0))
    def _():
        cnt_ref[...] = jnp.zeros_like(cnt_ref)

    @pl.when(phase == 0)
    def _():
        cnt_ref[...] += tile_cnt

    @pl.when(jnp.logical_and(phase == 1, i == 0))
    def _():
        lane8 = lax.broadcasted_iota(jnp.int32, (8, LANES), 1).astype(F32)
        sub8 = lax.broadcasted_iota(jnp.int32, (8, LANES), 0).astype(F32)
        cnt = jnp.broadcast_to(cnt_ref[...], (8, LANES))
        padded = jnp.ceil(cnt * (1.0 / moe_tile)) * moe_tile
        ends = _lane_cumsum(padded, lane8)
        start_ref[...] = (ends - padded)[0:1, :]
        base_ref[...] = jnp.zeros_like(base_ref)
        tile_row0 = (sub8 * LANES + lane8) * moe_tile
        tile_e = jnp.zeros((8, LANES), F32)
        for e in range(N_EXPERTS):
            end_e = jnp.sum(jnp.where(lane8 == e, ends, 0.0), axis=-1, keepdims=True)
            tile_e = tile_e + (tile_row0 >= end_e).astype(F32)
        tile_e_ref[0:8, :] = jnp.minimum(tile_e, N_EXPERTS - 1.0).astype(jnp.int32)
        total = jnp.sum(jnp.where(lane8 == N_EXPERTS - 1, ends, 0.0), axis=-1, keepdims=True)
        tile_e_ref[8:16, :] = jnp.broadcast_to(total * (1.0 / moe_tile), (8, LANES)).astype(jnp.int32)

    @pl.when(phase == 1)
    def _():
        r = lax.broadcasted_iota(jnp.int32, (tm, tm), 0)
        c = lax.broadcasted_iota(jnp.int32, (tm, tm), 1)
        before = (c < r).astype(BF16)
        rank = _dot(before, sel.astype(BF16))
        slot = start_ref[...] + base_ref[...] + rank
        p1 = jnp.sum(jnp.where(lane == i1, slot, 0.0), axis=-1, keepdims=True)
        p2 = jnp.sum(jnp.where(lane == i2, slot, 0.0), axis=-1, keepdims=True)
        pos_ref[...] = jnp.where(lane == 0, p1, jnp.where(lane == 1, p2, 0.0)).astype(jnp.int32)
        g1 = 1.0 / (1.0 + jnp.exp(v2 - v1))
        g2 = jnp.exp(v2 - v1) * g1
        gate_ref[...] = jnp.where(lane == 0, g1, jnp.where(lane == 1, g2, 0.0))
        base_ref[...] += tile_cnt


def _route(x, w_router_pad, tm, moe_tile):
    n, d = x.shape
    return pl.pallas_call(
        functools.partial(_route_kernel, moe_tile=moe_tile),
        grid=(2, n // tm),
        in_specs=[pl.BlockSpec((tm, d), lambda ph, i: (i, 0)),
                  _const_spec((d, LANES), lambda ph, i: (0, 0))],
        out_specs=[pl.BlockSpec((tm, LANES), lambda ph, i: (i * ph, 0)),
                   pl.BlockSpec((tm, LANES), lambda ph, i: (i * ph, 0)),
                   pl.BlockSpec((16, LANES), lambda ph, i: (0, 0))],
        out_shape=[jax.ShapeDtypeStruct((n, LANES), jnp.int32),
                   jax.ShapeDtypeStruct((n, LANES), F32),
                   jax.ShapeDtypeStruct((16, LANES), jnp.int32)],
        scratch_shapes=[pltpu.VMEM((1, LANES), F32), pltpu.VMEM((1, LANES), F32), pltpu.VMEM((1, LANES), F32)],
        compiler_params=_params(("arbitrary", "arbitrary"), 32),
        name="route",
    )(x, w_router_pad)


def _dispatch_kernel(pos_ref, x_ref, xs_in_ref, xs_ref, sem):
    del xs_in_ref
    tm = x_ref.shape[0]

    def issue(t, carry):
        for k in range(2):
            dst = pos_ref[0, 0, 2 * t + k]
            pltpu.make_async_copy(x_ref.at[pl.ds(t, 1)], xs_ref.at[pl.ds(dst, 1)], sem).start()
        return carry

    lax.fori_loop(0, tm, issue, 0)

    def drain(t, carry):
        pltpu.make_async_copy(x_ref.at[pl.ds(0, 1)], xs_ref.at[pl.ds(0, 1)], sem).wait()
        return carry

    lax.fori_loop(0, 2 * tm, drain, 0)


def _dispatch(pos_flat, x, xs_zero, tm):
    n, d = x.shape
    return pl.pallas_call(
        _dispatch_kernel,
        grid=(n // tm,),
        in_specs=[pl.BlockSpec((1, 1, 2 * tm), lambda i: (i, 0, 0), memory_space=pltpu.SMEM),
                  pl.BlockSpec((tm, d), lambda i: (i, 0)),
                  pl.BlockSpec(memory_space=pl.ANY)],
        out_specs=pl.BlockSpec(memory_space=pl.ANY),
        out_shape=jax.ShapeDtypeStruct(xs_zero.shape, xs_zero.dtype),
        scratch_shapes=[pltpu.SemaphoreType.DMA(())],
        input_output_aliases={2: 0},
        compiler_params=_params(("arbitrary",), 32),
        name="moe_dispatch",
    )(pos_flat, x, xs_zero)


def _experts_kernel(tile_e, n_valid, xs_ref, wg_ref, wu_ref, wd_ref, o_ref, xb_ref):
    del tile_e
    i = pl.program_id(0)
    j = pl.program_id(1)
    valid = i < n_valid[0]

    @pl.when(jnp.logical_and(valid, j == 0))
    def _():
        xb_ref[...] = xs_ref[...].astype(BF16)

    @pl.when(jnp.logical_and(jnp.logical_not(valid), j == 0))
    def _():
        o_ref[...] = jnp.zeros_like(o_ref)

    @pl.when(valid)
    def _():
        xb = xb_ref[...]
        h = (_silu(_dot(xb, wg_ref[...])) * _dot(xb, wu_ref[...])).astype(BF16)
        y = _dot(h, wd_ref[...])

        @pl.when(j == 0)
        def _():
            o_ref[...] = y

        @pl.when(j > 0)
        def _():
            o_ref[...] += y


def _experts(tile_e, n_valid, xs, wg_bf, wu_bf, wd_bf, tm, tf):
    m, d = xs.shape
    f = wg_bf.shape[2]
    nj = f // tf

    def row_idx(i, j, te, nv):
        return (jnp.minimum(i, nv[0] - 1), 0)

    def col_of(i, j, nv):
        return jnp.where(i < nv[0], j, nj - 1)

    grid_spec = pltpu.PrefetchScalarGridSpec(
        num_scalar_prefetch=2,
        grid=(m // tm, nj),
        in_specs=[pl.BlockSpec((tm, d), row_idx),
                  pl.BlockSpec((None, d, tf), lambda i, j, te, nv: (te[i], 0, col_of(i, j, nv))),
                  pl.BlockSpec((None, d, tf), lambda i, j, te, nv: (te[i], 0, col_of(i, j, nv))),
                  pl.BlockSpec((None, tf, d), lambda i, j, te, nv: (te[i], col_of(i, j, nv), 0))],
        out_specs=pl.BlockSpec((tm, d), lambda i, j, te, nv: (i, 0)),
        scratch_shapes=[pltpu.VMEM((tm, d), BF16)],
    )
    return pl.pallas_call(
        _experts_kernel,
        grid_spec=grid_spec,
        out_shape=jax.ShapeDtypeStruct((m, d), F32),
        compiler_params=_params(("arbitrary", "arbitrary"), 56),
        name="moe_experts",
    )(tile_e, n_valid, xs, wg_bf, wu_bf, wd_bf)


def _combine_kernel(pos_ref, gate_ref, x_ref, g_ref, b_ref, ys_ref, o_ref, buf_ref, sem):
    tm = x_ref.shape[0]

    def issue(t, carry):
        for k in range(2):
            src = pos_ref[0, 0, 2 * t + k]
            pltpu.make_async_copy(ys_ref.at[pl.ds(src, 1)], buf_ref.at[k, pl.ds(t, 1)], sem).start()
        return carry

    lax.fori_loop(0, tm, issue, 0)

    def drain(t, carry):
        pltpu.make_async_copy(ys_ref.at[pl.ds(0, 1)], buf_ref.at[0, pl.ds(0, 1)], sem).wait()
        return carry

    lax.fori_loop(0, 2 * tm, drain, 0)

    gates = gate_ref[...]
    y = gates[:, 0:1] * buf_ref[0] + gates[:, 1:2] * buf_ref[1]
    o_ref[...] = _layer_norm(ALPHA * x_ref[...] + y, g_ref[...], b_ref[...])


def _combine(pos_flat, gates, x, g, b, ys, tm):
    n, d = x.shape
    return pl.pallas_call(
        _combine_kernel,
        grid=(n // tm,),
        in_specs=[pl.BlockSpec((1, 1, 2 * tm), lambda i: (i, 0, 0), memory_space=pltpu.SMEM),
                  pl.BlockSpec((tm, LANES), lambda i: (i, 0)),
                  pl.BlockSpec((tm, d), lambda i: (i, 0)),
                  _const_spec((1, d), lambda i: (0, 0)),
                  _const_spec((1, d), lambda i: (0, 0)),
                  pl.BlockSpec(memory_space=pl.ANY)],
        out_specs=pl.BlockSpec((tm, d), lambda i: (i, 0)),
        out_shape=jax.ShapeDtypeStruct((n, d), F32),
        scratch_shapes=[pltpu.VMEM((2, tm, d), F32), pltpu.SemaphoreType.DMA(())],
        compiler_params=_params(("arbitrary",), 32),
        name="moe_combine",
    )(pos_flat, gates, x, g, b, ys)


def kernel(x, mem, ln_g, ln_b, w_mix_out, w_mem_kv, a_w_in, a_vnorm_g, a_vnorm_b, a_w_s, a_b_s,
           shared_w_kv, b_w_in, b_lambda_q1, b_lambda_k1, b_lambda_q2, b_lambda_k2, b_subln_g,
           ffn_w_gate, ffn_w_up, ffn_w_down, moe_w_router, moe_w_gate, moe_w_up, moe_w_down):
    bsz, seq, d = x.shape
    n = bsz * seq
    t = _tiles(n, seq)
    xf = x.reshape(n, d)
    row = lambda v: v.reshape(1, -1)

    memkv = _memkv(mem, w_mem_kv.astype(BF16))

    mixed = _mixer_a(xf, a_w_in[0].astype(BF16), row(a_vnorm_g[0]), row(a_vnorm_b[0]),
                     a_w_s[0], jnp.transpose(a_b_s[0]), memkv, seq, t["mixer"])
    x1, x1b = _proj_ln(mixed, 0, mixed, MAIN_WIDTH // MEM_WIDTH, w_mix_out[0].astype(BF16), xf,
                       row(ln_g[0, 0]), row(ln_b[0, 0]), t["proj"])
    x2, x2b = _ffn_ln(x1, x1b, ffn_w_gate[0].astype(BF16), ffn_w_up[0].astype(BF16),
                      ffn_w_down[0].astype(BF16), row(ln_g[0, 1]), row(ln_b[0, 1]), t["ffn"], t["ffn_f"])

    w_cat = jnp.concatenate([b_w_in[0], shared_w_kv], axis=1).astype(BF16)
    col_scale = jnp.concatenate([jnp.full((MAIN_WIDTH,), QK_DIM ** -0.5, F32),
                                 jnp.ones((w_cat.shape[1] - MAIN_WIDTH,), F32)]).reshape(1, -1)
    hk = _matmul_colscale(x2b, w_cat, col_scale, t["mm"], t["mm_n"])
    lambda_init = 0.8 - 0.6 * math.exp(-0.3 * 1)
    main = _diff_attn(hk, row(b_lambda_q1[0]), row(b_lambda_k1[0]), row(b_lambda_q2[0]),
                      row(b_lambda_k2[0]), row(b_subln_g[0]), bsz, seq, lambda_init,
                      t["attn_q"], t["attn_k"])
    mem_out = _mem_attn_b(hk, memkv, seq, t["mem"])
    x3, _ = _proj_ln(main, 0, mem_out, 0, w_mix_out[1].astype(BF16), x2,
                     row(ln_g[1, 0]), row(ln_b[1, 0]), t["proj"])

    moe_tile = t["moe"]
    n_row_tiles = (2 * n) // moe_tile + N_EXPERTS
    w_router_pad = jnp.pad(moe_w_router[0], ((0, 0), (0, LANES - N_EXPERTS)))
    pos, gates, tile_info = _route(x3, w_router_pad, t["route"], moe_tile)
    tile_e = tile_info[0:8].reshape(-1)[:n_row_tiles]
    n_valid = tile_info[8, 0:1]
    tc = t["comb"]
    pos_flat = pos[:, :2].reshape(n // tc, 1, 2 * tc)
    xs = _dispatch(pos_flat, x3, jnp.zeros((n_row_tiles * moe_tile, d), F32), tc)
    ys = _experts(tile_e, n_valid, xs, moe_w_gate[0].astype(BF16), moe_w_up[0].astype(BF16),
                  moe_w_down[0].astype(BF16), moe_tile, t["moe_f"])
    x4 = _combine(pos_flat, gates, x3, row(ln_g[1, 1]), row(ln_b[1, 1]), ys, tc)
    return x4.reshape(bsz, seq, d)
```

```python
import functools
import math

import numpy as np
import jax
import jax.numpy as jnp
from jax import lax
from jax.experimental import pallas as pl
from jax.experimental.pallas import tpu as pltpu

BF16 = jnp.bfloat16
F32 = jnp.float32

D_MODEL = 2048
MEM_WIDTH = 512
MAIN_WIDTH = 1536
MEM_HEADS = 4
MEM_HEAD_DIM = 128
CHUNK = 128
SG_GROUPS = 12
QK_DIM = 128
V_DIM = 256
DIFF_HEADS = 6
N_EXPERTS = 8
DEPTH = 2
ALPHA = (2.0 * DEPTH) ** 0.25
LN_EPS = 1e-5
LANES = 128
ONES_ROWS = 16
V_AUG = V_DIM + ONES_ROWS
NEG_BIG = -1e30
MIB = 1024 * 1024


def _tiles(n_tokens, seq):
    def fit(pref, total):
        t = min(pref, total)
        while total % t:
            t //= 2
        return t
    return dict(
        mixer=fit(512, seq),
        proj=fit(512, n_tokens),
        ffn=fit(512, n_tokens),
        ffn_f=512,
        mm=fit(1024, n_tokens),
        mm_n=512,
        mm_t=fit(512, n_tokens),
        attn_q=fit(1024, seq),
        attn_k=fit(512, seq),
        mem=fit(512, seq),
        route=fit(512, n_tokens),
        moe=fit(1024, n_tokens),
        moe_f=512,
        comb=fit(256, n_tokens),
    )


def _params(sem, vmem_mib):
    return pltpu.CompilerParams(dimension_semantics=sem, vmem_limit_bytes=vmem_mib * MIB)


def _const_spec(shape, index_map):
    return pl.BlockSpec(shape, index_map, pipeline_mode=pl.Buffered(1))


def _dot(a, b):
    return jnp.dot(a, b, preferred_element_type=F32)


def _dot_nt(a, b):
    return lax.dot_general(a, b, (((1,), (1,)), ((), ())), preferred_element_type=F32)


def _layer_norm(v, g, b):
    mu = jnp.mean(v, axis=-1, keepdims=True)
    c = v - mu
    var = jnp.mean(c * c, axis=-1, keepdims=True)
    return c * lax.rsqrt(var + LN_EPS) * g + b


def _gelu_tanh(v):
    return 0.5 * v * (1.0 + jnp.tanh(math.sqrt(2.0 / math.pi) * (v + 0.044715 * (v * v * v))))


def _silu(v):
    return v / (1.0 + jnp.exp(-v))


def _mem_attention(q, kv_ref, o_ref, col0):
    scale = MEM_HEAD_DIM ** -0.5
    for h in range(MEM_HEADS):
        lo = h * MEM_HEAD_DIM
        qh = q[:, lo:lo + MEM_HEAD_DIM].astype(BF16)
        kh = kv_ref[:, lo:lo + MEM_HEAD_DIM]
        vh = kv_ref[:, MEM_WIDTH + lo:MEM_WIDTH + lo + MEM_HEAD_DIM]
        s = _dot_nt(qh, kh) * scale
        e = jnp.exp(s - jnp.max(s, axis=-1, keepdims=True))
        o = _dot(e.astype(BF16), vh) / jnp.sum(e, axis=-1, keepdims=True)
        o_ref[:, col0 + lo:col0 + lo + MEM_HEAD_DIM] = o.astype(o_ref.dtype)


def _memkv_kernel(mem_ref, w_ref, o_ref):
    o_ref[...] = _dot(mem_ref[...].astype(BF16), w_ref[...]).astype(o_ref.dtype)


def _memkv(mem, w_mem_kv_bf):
    bsz, n_mem, d = mem.shape
    depth, _, cols = w_mem_kv_bf.shape
    return pl.pallas_call(
        _memkv_kernel,
        grid=(depth, bsz),
        in_specs=[pl.BlockSpec((None, n_mem, d), lambda l, b: (b, 0, 0)),
                  pl.BlockSpec((None, d, cols), lambda l, b: (l, 0, 0))],
        out_specs=pl.BlockSpec((None, None, n_mem, cols), lambda l, b: (l, b, 0, 0)),
        out_shape=jax.ShapeDtypeStruct((depth, bsz, n_mem, cols), BF16),
        compiler_params=_params(("arbitrary", "arbitrary"), 32),
        name="memkv",
    )(mem, w_mem_kv_bf)


def _mixer_a_kernel(x_ref, w_ref, vg_ref, vb_ref, ws_ref, bs_ref, kv_ref, o_ref):
    tm = x_ref.shape[0]
    xb = x_ref[...].astype(BF16)
    v = _gelu_tanh(_dot(xb, w_ref[:, MAIN_WIDTH:2 * MAIN_WIDTH]))
    vn = _layer_norm(v, vg_ref[...], vb_ref[...]).astype(BF16)
    u = _gelu_tanh(_dot(xb, w_ref[:, :MAIN_WIDTH]))
    row = lax.broadcasted_iota(jnp.int32, (CHUNK, CHUNK), 0)
    col = lax.broadcasted_iota(jnp.int32, (CHUNK, CHUNK), 1)
    causal = col <= row
    for g in range(SG_GROUPS):
        wg = jnp.where(causal, ws_ref[g], 0.0).astype(BF16)
        bias = bs_ref[:, g:g + 1]
        for c in range(tm // CHUNK):
            rows = slice(c * CHUNK, (c + 1) * CHUNK)
            cols = slice(g * CHUNK, (g + 1) * CHUNK)
            s = _dot(wg, vn[rows, cols]) + bias
            o_ref[rows, cols] = (u[rows, cols] * s).astype(o_ref.dtype)
    q_mem = _dot(xb, w_ref[:, 2 * MAIN_WIDTH:])
    _mem_attention(q_mem, kv_ref, o_ref, MAIN_WIDTH)


def _mixer_a(x, w_in_bf, vnorm_g, vnorm_b, w_s, b_s_t, memkv, seq, tm):
    n, d = x.shape
    in_cols = w_in_bf.shape[1]
    return pl.pallas_call(
        _mixer_a_kernel,
        grid=(n // tm,),
        in_specs=[pl.BlockSpec((tm, d), lambda i: (i, 0)),
                  _const_spec((d, in_cols), lambda i: (0, 0)),
                  _const_spec((1, MAIN_WIDTH), lambda i: (0, 0)),
                  _const_spec((1, MAIN_WIDTH), lambda i: (0, 0)),
                  _const_spec((SG_GROUPS, CHUNK, CHUNK), lambda i: (0, 0, 0)),
                  _const_spec((CHUNK, SG_GROUPS), lambda i: (0, 0)),
                  pl.BlockSpec((None, None) + memkv.shape[2:], lambda i: (0, (i * tm) // seq, 0, 0))],
        out_specs=pl.BlockSpec((tm, d), lambda i: (i, 0)),
        out_shape=jax.ShapeDtypeStruct((n, d), BF16),
        compiler_params=_params(("arbitrary",), 56),
        name="mixer_a",
    )(x, w_in_bf, vnorm_g, vnorm_b, w_s, b_s_t, memkv)


def _proj_ln_kernel(a1_ref, a2_ref, w_ref, x_ref, g_ref, b_ref, o32_ref, o16_ref):
    mix = _dot(a1_ref[...], w_ref[:MAIN_WIDTH, :]) + _dot(a2_ref[...], w_ref[MAIN_WIDTH:, :])
    y = _layer_norm(ALPHA * x_ref[...] + mix, g_ref[...], b_ref[...])
    o32_ref[...] = y
    o16_ref[...] = y.astype(BF16)


def _proj_ln(a_main, main_blk, a_mem, mem_blk, w_bf, x, g, b, tm):
    n, d = x.shape
    return pl.pallas_call(
        _proj_ln_kernel,
        grid=(n // tm,),
        in_specs=[pl.BlockSpec((tm, MAIN_WIDTH), lambda i: (i, main_blk)),
                  pl.BlockSpec((tm, MEM_WIDTH), lambda i: (i, mem_blk)),
                  _const_spec((d, d), lambda i: (0, 0)),
                  pl.BlockSpec((tm, d), lambda i: (i, 0)),
                  _const_spec((1, d), lambda i: (0, 0)),
                  _const_spec((1, d), lambda i: (0, 0))],
        out_specs=[pl.BlockSpec((tm, d), lambda i: (i, 0)),
                   pl.BlockSpec((tm, d), lambda i: (i, 0))],
        out_shape=[jax.ShapeDtypeStruct((n, d), F32), jax.ShapeDtypeStruct((n, d), BF16)],
        compiler_params=_params(("arbitrary",), 48),
        name="proj_ln",
    )(a_main, a_mem, w_bf, x, g, b)


def _ffn_ln_kernel(x_ref, xb_ref, wg_ref, wu_ref, wd_ref, g_ref, b_ref, o32_ref, o16_ref, acc_ref):
    j = pl.program_id(1)

    @pl.when(j == 0)
    def _():
        acc_ref[...] = jnp.zeros_like(acc_ref)

    xb = xb_ref[...]
    h = (_silu(_dot(xb, wg_ref[...])) * _dot(xb, wu_ref[...])).astype(BF16)
    acc_ref[...] += _dot(h, wd_ref[...])

    @pl.when(j == pl.num_programs(1) - 1)
    def _():
        y = _layer_norm(ALPHA * x_ref[...] + acc_ref[...], g_ref[...], b_ref[...])
        o32_ref[...] = y
        o16_ref[...] = y.astype(BF16)


def _ffn_ln(x, xb, wg_bf, wu_bf, wd_bf, g, b, tm, tf):
    n, d = x.shape
    f = wg_bf.shape[1]
    return pl.pallas_call(
        _ffn_ln_kernel,
        grid=(n // tm, f // tf),
        in_specs=[pl.BlockSpec((tm, d), lambda i, j: (i, 0)),
                  pl.BlockSpec((tm, d), lambda i, j: (i, 0)),
                  pl.BlockSpec((d, tf), lambda i, j: (0, j)),
                  pl.BlockSpec((d, tf), lambda i, j: (0, j)),
                  pl.BlockSpec((tf, d), lambda i, j: (j, 0)),
                  _const_spec((1, d), lambda i, j: (0, 0)),
                  _const_spec((1, d), lambda i, j: (0, 0))],
        out_specs=[pl.BlockSpec((tm, d), lambda i, j: (i, 0)),
                   pl.BlockSpec((tm, d), lambda i, j: (i, 0))],
        out_shape=[jax.ShapeDtypeStruct((n, d), F32), jax.ShapeDtypeStruct((n, d), BF16)],
        scratch_shapes=[pltpu.VMEM((tm, d), F32)],
        compiler_params=_params(("arbitrary", "arbitrary"), 56),
        name="ffn_ln",
    )(x, xb, wg_bf, wu_bf, wd_bf, g, b)


def _matmul_kernel(x_ref, w_ref, s_ref, o_ref):
    o_ref[...] = (_dot(x_ref[...], w_ref[...]) * s_ref[...]).astype(o_ref.dtype)


def _matmul_colscale(xb, w_bf, col_scale, tm, tn):
    n, d = xb.shape
    cols = w_bf.shape[1]
    return pl.pallas_call(
        _matmul_kernel,
        grid=(n // tm, cols // tn),
        in_specs=[pl.BlockSpec((tm, d), lambda i, j: (i, 0)),
                  pl.BlockSpec((d, tn), lambda i, j: (0, j)),
                  pl.BlockSpec((1, tn), lambda i, j: (0, j))],
        out_specs=pl.BlockSpec((tm, tn), lambda i, j: (i, j)),
        out_shape=jax.ShapeDtypeStruct((n, cols), BF16),
        compiler_params=_params(("arbitrary", "arbitrary"), 48),
        name="in_proj_b",
    )(xb, w_bf, col_scale)


def _proj_t_kernel(wt_ref, bias_ref, x_ref, o_ref):
    o_ref[...] = (_dot_nt(wt_ref[...], x_ref[...]) + bias_ref[...]).astype(o_ref.dtype)


def _proj_transposed(xb, wt_bf, bias_col, tn):
    n, d = xb.shape
    cols = wt_bf.shape[0]
    return pl.pallas_call(
        _proj_t_kernel,
        grid=(n // tn,),
        in_specs=[_const_spec((cols, d), lambda i: (0, 0)),
                  _const_spec((cols, 1), lambda i: (0, 0)),
                  pl.BlockSpec((tn, d), lambda i: (i, 0))],
        out_specs=pl.BlockSpec((cols, tn), lambda i: (0, i)),
        out_shape=jax.ShapeDtypeStruct((cols, n), BF16),
        compiler_params=_params(("arbitrary",), 48),
        name="v_proj_t",
    )(wt_bf, bias_col, xb)


def _diff_attn_kernel(qi_tab, kj_tab, q1_ref, q2_ref, k1_ref, k2_ref, vt_ref,
                      lq1_ref, lk1_ref, lq2_ref, lk2_ref, sg_ref, o_ref,
                      m1_ref, a1_ref, m2_ref, a2_ref, *, lambda_init, qw):
    tq, tk = q1_ref.shape[0], k1_ref.shape[0]
    p = pl.program_id(2)
    qi = qi_tab[p]
    kj = kj_tab[p]

    @pl.when(kj == 0)
    def _():
        for m_ref, a_ref in ((m1_ref, a1_ref), (m2_ref, a2_ref)):
            m_ref[...] = jnp.full_like(m_ref, NEG_BIG)
            a_ref[...] = jnp.zeros_like(a_ref)

    chains = [(q_ref, k_ref, m_ref, a_ref, j)
              for j in range(tq // qw)
              for q_ref, k_ref, m_ref, a_ref in ((q1_ref, k1_ref, m1_ref, a1_ref),
                                                 (q2_ref, k2_ref, m2_ref, a2_ref))]

    def step(key0):
        if key0 is None:
            live = chains
        else:
            live = [c for c in chains if key0 <= (c[4] + 1) * qw - 1]

        def scores(chain):
            q_ref, k_ref, _, _, j = chain
            s = _dot_nt(k_ref[...], q_ref[j * qw:(j + 1) * qw, :])
            if key0 is not None and key0 + tk - 1 > j * qw:
                key = key0 + lax.broadcasted_iota(jnp.int32, (tk, qw), 0)
                qry = j * qw + lax.broadcasted_iota(jnp.int32, (tk, qw), 1)
                s = jnp.where(key <= qry, s, NEG_BIG)
            return s.astype(BF16)

        def softmax(chain, s):
            m_ref, j = chain[2], chain[4]
            cols = slice(j * qw, (j + 1) * qw)
            m_old = m_ref[:, cols]
            m_new = jnp.maximum(m_old, jnp.max(s, axis=0, keepdims=True).astype(F32))
            alpha = jnp.exp(m_old - m_new)
            e = jnp.exp(s - m_new.astype(BF16))
            m_ref[:, cols] = m_new
            return e, alpha

        def weighted_values(chain, e, alpha):
            a_ref, j = chain[3], chain[4]
            cols = slice(j * qw, (j + 1) * qw)
            a_ref[:, cols] = alpha * a_ref[:, cols] + _dot(vt_ref[...], e)

        n = len(live)
        s_live, e_live = {}, {}
        for t in range(n + 2):
            if t < n:
                s_live[t] = scores(live[t])
            if 0 <= t - 1 < n:
                e_live[t - 1] = softmax(live[t - 1], s_live.pop(t - 1))
            if 0 <= t - 2 < n:
                weighted_values(live[t - 2], *e_live.pop(t - 2))

    key0 = kj * tk - qi * tq
    pl.when(key0 < 0)(lambda: step(None))
    for static_key0 in range(0, tq, tk):
        pl.when(key0 == static_key0)(functools.partial(step, static_key0))

    @pl.when(kj == ((qi + 1) * tq - 1) // tk)
    def _():
        lam = (jnp.exp(jnp.sum(lq1_ref[...] * lk1_ref[...], axis=-1, keepdims=True))
               - jnp.exp(jnp.sum(lq2_ref[...] * lk2_ref[...], axis=-1, keepdims=True))
               + lambda_init)
        o1 = a1_ref[0:V_DIM, :] / a1_ref[V_DIM:V_DIM + 1, :]
        o2 = a2_ref[0:V_DIM, :] / a2_ref[V_DIM:V_DIM + 1, :]
        o = o1 - lam * o2
        o = o * lax.rsqrt(jnp.mean(o * o, axis=0, keepdims=True) + LN_EPS) * sg_ref[...]
        o_ref[...] = jnp.transpose(o * (1.0 - lambda_init)).astype(o_ref.dtype)


def _diff_attn(hk, vt, lq1, lk1, lq2, lk2, subln_g, bsz, seq, lambda_init, tq, tk):
    n = hk.shape[0]
    assert tq % tk == 0 and tk >= 2
    nq = seq // tq
    pairs =[(qi, kj) for qi in range(nq) for kj in range(((qi + 1) * tq - 1) // tk + 1)]
    qi_tab = jnp.asarray(np.array([p[0] for p in pairs], np.int32))
    kj_tab = jnp.asarray(np.array([p[1] for p in pairs], np.int32))
    nqb, nkb = seq // tq, seq // tk
    k_blk0 = (MAIN_WIDTH + MEM_WIDTH) // QK_DIM
    qspec = lambda off: pl.BlockSpec((tq, QK_DIM), lambda b, h, p, qt, kt: (b * nqb + qt[p], off + h))
    kspec = lambda off: pl.BlockSpec((tk, QK_DIM), lambda b, h, p, qt, kt: (b * nkb + kt[p], off + h))
    vec = lambda w: pl.BlockSpec((1, w), lambda b, h, p, qt, kt: (0, 0))
    grid_spec = pltpu.PrefetchScalarGridSpec(
        num_scalar_prefetch=2,
        grid=(bsz, DIFF_HEADS, len(pairs)),
        in_specs=[qspec(0), qspec(DIFF_HEADS), kspec(k_blk0), kspec(k_blk0 + DIFF_HEADS),
                  pl.BlockSpec((V_AUG, tk), lambda b, h, p, qt, kt: (h, b * nkb + kt[p])),
                  vec(QK_DIM), vec(QK_DIM), vec(QK_DIM), vec(QK_DIM),
                  pl.BlockSpec((V_DIM, 1), lambda b, h, p, qt, kt: (0, 0))],
        out_specs=pl.BlockSpec((tq, V_DIM), lambda b, h, p, qt, kt: (b * nqb + qt[p], h)),
        scratch_shapes=[pltpu.VMEM((1, tq), F32), pltpu.VMEM((V_AUG, tq), F32),
                        pltpu.VMEM((1, tq), F32), pltpu.VMEM((V_AUG, tq), F32)],
    )
    return pl.pallas_call(
        functools.partial(_diff_attn_kernel, lambda_init=lambda_init, qw=min(tq, 2 * LANES)),
        grid_spec=grid_spec,
        out_shape=jax.ShapeDtypeStruct((n, MAIN_WIDTH), BF16),
        compiler_params=_params(("arbitrary", "arbitrary", "arbitrary"), 48),
        name="diff_attn",
    )(qi_tab, kj_tab, hk, hk, hk, hk, vt, lq1, lk1, lq2, lk2, subln_g.reshape(V_DIM, 1))


def _mem_attn_kernel(q_ref, kv_ref, o_ref):
    _mem_attention(q_ref[...].astype(F32), kv_ref, o_ref, 0)


def _mem_attn_b(hk, memkv, seq, tm):
    n = hk.shape[0]
    return pl.pallas_call(
        _mem_attn_kernel,
        grid=(n // tm,),
        in_specs=[pl.BlockSpec((tm, MEM_WIDTH), lambda i: (i, MAIN_WIDTH // MEM_WIDTH)),
                  pl.BlockSpec((None, None) + memkv.shape[2:], lambda i: (1, (i * tm) // seq, 0, 0))],
        out_specs=pl.BlockSpec((tm, MEM_WIDTH), lambda i: (i, 0)),
        out_shape=jax.ShapeDtypeStruct((n, MEM_WIDTH), BF16),
        compiler_params=_params(("arbitrary",), 32),
        name="mem_attn_b",
    )(hk, memkv)


def _lane_cumsum(v, lane):
    for shift in (1, 2, 4):
        v = v + jnp.where(lane >= shift, pltpu.roll(v, shift, 1), 0.0)
    return v


def _route_kernel(x_ref, wr_ref, pos_ref, gate_ref, tile_e_ref, cnt_ref, base_ref, start_ref, *, moe_tile):
    phase = pl.program_id(0)
    i = pl.program_id(1)
    tm = x_ref.shape[0]
    lane = lax.broadcasted_iota(jnp.int32, (tm, LANES), 1).astype(F32)

    x = x_ref[...]
    xh = x.astype(BF16)
    xl = (x - xh.astype(F32)).astype(BF16)
    w = wr_ref[...]
    wh = w.astype(BF16)
    wl = (w - wh.astype(F32)).astype(BF16)
    logits = _dot(xh, wh) + (_dot(xh, wl) + _dot(xl, wh))
    logits = jnp.where(lane < N_EXPERTS, logits, NEG_BIG)

    v1 = jnp.max(logits, axis=-1, keepdims=True)
    i1 = jnp.min(jnp.where(logits == v1, lane, float(LANES)), axis=-1, keepdims=True)
    rest = jnp.where(lane == i1, NEG_BIG, logits)
    v2 = jnp.max(rest, axis=-1, keepdims=True)
    i2 = jnp.min(jnp.where(rest == v2, lane, float(LANES)), axis=-1, keepdims=True)
    sel = jnp.logical_or(lane == i1, lane == i2)
    tile_cnt = jnp.sum(sel.astype(F32), axis=0, keepdims=True)

    @pl.when(jnp.logical_and(phase == 0, i == 0))
    def _():
        cnt_ref[...] = jnp.zeros_like(cnt_ref)

    @pl.when(phase == 0)
    def _():
        cnt_ref[...] += tile_cnt

    @pl.when(jnp.logical_and(phase == 1, i == 0))
    def _():
        lane8 = lax.broadcasted_iota(jnp.int32, (8, LANES), 1).astype(F32)
        sub8 = lax.broadcasted_iota(jnp.int32, (8, LANES), 0).astype(F32)
        cnt = jnp.broadcast_to(cnt_ref[...], (8, LANES))
        padded = jnp.ceil(cnt * (1.0 / moe_tile)) * moe_tile
        ends = _lane_cumsum(padded, lane8)
        start_ref[...] = (ends - padded)[0:1, :]
        base_ref[...] = jnp.zeros_like(base_ref)
        tile_row0 = (sub8 * LANES + lane8) * moe_tile
        tile_e = jnp.zeros((8, LANES), F32)
        for e in range(N_EXPERTS):
            end_e = jnp.sum(jnp.where(lane8 == e, ends, 0.0), axis=-1, keepdims=True)
            tile_e = tile_e + (tile_row0 >= end_e).astype(F32)
        tile_e_ref[0:8, :] = jnp.minimum(tile_e, N_EXPERTS - 1.0).astype(jnp.int32)
        total = jnp.sum(jnp.where(lane8 == N_EXPERTS - 1, ends, 0.0), axis=-1, keepdims=True)
        tile_e_ref[8:16, :] = jnp.broadcast_to(total * (1.0 / moe_tile), (8, LANES)).astype(jnp.int32)

    @pl.when(phase == 1)
    def _():
        r = lax.broadcasted_iota(jnp.int32, (tm, tm), 0)
        c = lax.broadcasted_iota(jnp.int32, (tm, tm), 1)
        before = (c < r).astype(BF16)
        rank = _dot(before, sel.astype(BF16))
        slot = start_ref[...] + base_ref[...] + rank
        p1 = jnp.sum(jnp.where(lane == i1, slot, 0.0), axis=-1, keepdims=True)
        p2 = jnp.sum(jnp.where(lane == i2, slot, 0.0), axis=-1, keepdims=True)
        pos_ref[...] = jnp.where(lane == 0, p1, jnp.where(lane == 1, p2, 0.0)).astype(jnp.int32)
        g1 = 1.0 / (1.0 + jnp.exp(v2 - v1))
        g2 = jnp.exp(v2 - v1) * g1
        gate_ref[...] = jnp.where(lane == 0, g1, jnp.where(lane == 1, g2, 0.0))
        base_ref[...] += tile_cnt


def _route(x, w_router_pad, tm, moe_tile):
    n, d = x.shape
    return pl.pallas_call(
        functools.partial(_route_kernel, moe_tile=moe_tile),
        grid=(2, n // tm),
        in_specs=[pl.BlockSpec((tm, d), lambda ph, i: (i, 0)),
                  _const_spec((d, LANES), lambda ph, i: (0, 0))],
        out_specs=[pl.BlockSpec((tm, LANES), lambda ph, i: (i * ph, 0)),
                   pl.BlockSpec((tm, LANES), lambda ph, i: (i * ph, 0)),
                   pl.BlockSpec((16, LANES), lambda ph, i: (0, 0))],
        out_shape=[jax.ShapeDtypeStruct((n, LANES), jnp.int32),
                   jax.ShapeDtypeStruct((n, LANES), F32),
                   jax.ShapeDtypeStruct((16, LANES), jnp.int32)],
        scratch_shapes=[pltpu.VMEM((1, LANES), F32), pltpu.VMEM((1, LANES), F32), pltpu.VMEM((1, LANES), F32)],
        compiler_params=_params(("arbitrary", "arbitrary"), 32),
        name="route",
    )(x, w_router_pad)


def _dispatch_kernel(pos_ref, x_ref, xs_in_ref, xs_ref, sem):
    del xs_in_ref
    tm = x_ref.shape[0]

    def issue(t, carry):
        for k in range(2):
            dst = pos_ref[0, 0, 2 * t + k]
            pltpu.make_async_copy(x_ref.at[pl.ds(t, 1)], xs_ref.at[pl.ds(dst, 1)], sem).start()
        return carry

    lax.fori_loop(0, tm, issue, 0)

    def drain(t, carry):
        pltpu.make_async_copy(x_ref.at[pl.ds(0, 1)], xs_ref.at[pl.ds(0, 1)], sem).wait()
        return carry

    lax.fori_loop(0, 2 * tm, drain, 0)


def _dispatch(pos_flat, x, xs_zero, tm):
    n, d = x.shape
    return pl.pallas_call(
        _dispatch_kernel,
        grid=(n // tm,),
        in_specs=[pl.BlockSpec((1, 1, 2 * tm), lambda i: (i, 0, 0), memory_space=pltpu.SMEM),
                  pl.BlockSpec((tm, d), lambda i: (i, 0)),
                  pl.BlockSpec(memory_space=pl.ANY)],
        out_specs=pl.BlockSpec(memory_space=pl.ANY),
        out_shape=jax.ShapeDtypeStruct(xs_zero.shape, xs_zero.dtype),
        scratch_shapes=[pltpu.SemaphoreType.DMA(())],
        input_output_aliases={2: 0},
        compiler_params=_params(("arbitrary",), 32),
        name="moe_dispatch",
    )(pos_flat, x, xs_zero)


def _experts_kernel(tile_e, n_valid, xs_ref, wg_ref, wu_ref, wd_ref, o_ref, xb_ref):
    del tile_e
    i = pl.program_id(0)
    j = pl.program_id(1)
    valid = i < n_valid[0]

    @pl.when(jnp.logical_and(valid, j == 0))
    def _():
        xb_ref[...] = xs_ref[...].astype(BF16)

    @pl.when(jnp.logical_and(jnp.logical_not(valid), j == 0))
    def _():
        o_ref[...] = jnp.zeros_like(o_ref)

    @pl.when(valid)
    def _():
        xb = xb_ref[...]
        h = (_silu(_dot(xb, wg_ref[...])) * _dot(xb, wu_ref[...])).astype(BF16)
        y = _dot(h, wd_ref[...])

        @pl.when(j == 0)
        def _():
            o_ref[...] = y

        @pl.when(j > 0)
        def _():
            o_ref[...] += y


def _experts(tile_e, n_valid, xs, wg_bf, wu_bf, wd_bf, tm, tf):
    m, d = xs.shape
    f = wg_bf.shape[2]
    nj = f // tf

    def row_idx(i, j, te, nv):
        return (jnp.minimum(i, nv[0] - 1), 0)

    def col_of(i, j, nv):
        return jnp.where(i < nv[0], j, nj - 1)

    grid_spec = pltpu.PrefetchScalarGridSpec(
        num_scalar_prefetch=2,
        grid=(m // tm, nj),
        in_specs=[pl.BlockSpec((tm, d), row_idx),
                  pl.BlockSpec((None, d, tf), lambda i, j, te, nv: (te[i], 0, col_of(i, j, nv))),
                  pl.BlockSpec((None, d, tf), lambda i, j, te, nv: (te[i], 0, col_of(i, j, nv))),
                  pl.BlockSpec((None, tf, d), lambda i, j, te, nv: (te[i], col_of(i, j, nv), 0))],
        out_specs=pl.BlockSpec((tm, d), lambda i, j, te, nv: (i, 0)),
        scratch_shapes=[pltpu.VMEM((tm, d), BF16)],
    )
    return pl.pallas_call(
        _experts_kernel,
        grid_spec=grid_spec,
        out_shape=jax.ShapeDtypeStruct((m, d), F32),
        compiler_params=_params(("arbitrary", "arbitrary"), 56),
        name="moe_experts",
    )(tile_e, n_valid, xs, wg_bf, wu_bf, wd_bf)


def _combine_kernel(pos_ref, gate_ref, x_ref, g_ref, b_ref, ys_ref, o_ref, buf_ref, sem):
    tm = x_ref.shape[0]

    def issue(t, carry):
        for k in range(2):
            src = pos_ref[0, 0, 2 * t + k]
            pltpu.make_async_copy(ys_ref.at[pl.ds(src, 1)], buf_ref.at[k, pl.ds(t, 1)], sem).start()
        return carry

    lax.fori_loop(0, tm, issue, 0)

    def drain(t, carry):
        pltpu.make_async_copy(ys_ref.at[pl.ds(0, 1)], buf_ref.at[0, pl.ds(0, 1)], sem).wait()
        return carry

    lax.fori_loop(0, 2 * tm, drain, 0)

    gates = gate_ref[...]
    y = gates[:, 0:1] * buf_ref[0] + gates[:, 1:2] * buf_ref[1]
    o_ref[...] = _layer_norm(ALPHA * x_ref[...] + y, g_ref[...], b_ref[...])


def _combine(pos_flat, gates, x, g, b, ys, tm):
    n, d = x.shape
    return pl.pallas_call(
        _combine_kernel,
        grid=(n // tm,),
        in_specs=[pl.BlockSpec((1, 1, 2 * tm), lambda i: (i, 0, 0), memory_space=pltpu.SMEM),
                  pl.BlockSpec((tm, LANES), lambda i: (i, 0)),
                  pl.BlockSpec((tm, d), lambda i: (i, 0)),
                  _const_spec((1, d), lambda i: (0, 0)),
                  _const_spec((1, d), lambda i: (0, 0)),
                  pl.BlockSpec(memory_space=pl.ANY)],
        out_specs=pl.BlockSpec((tm, d), lambda i: (i, 0)),
        out_shape=jax.ShapeDtypeStruct((n, d), F32),
        scratch_shapes=[pltpu.VMEM((2, tm, d), F32), pltpu.SemaphoreType.DMA(())],
        compiler_params=_params(("arbitrary",), 32),
        name="moe_combine",
    )(pos_flat, gates, x, g, b, ys)


def kernel(x, mem, ln_g, ln_b, w_mix_out, w_mem_kv, a_w_in, a_vnorm_g, a_vnorm_b, a_w_s, a_b_s,
           shared_w_kv, b_w_in, b_lambda_q1, b_lambda_k1, b_lambda_q2, b_lambda_k2, b_subln_g,
           ffn_w_gate, ffn_w_up, ffn_w_down, moe_w_router, moe_w_gate, moe_w_up, moe_w_down):
    bsz, seq, d = x.shape
    n = bsz * seq
    t = _tiles(n, seq)
    xf = x.reshape(n, d)
    row = lambda v: v.reshape(1, -1)

    memkv = _memkv(mem, w_mem_kv.astype(BF16))

    mixed = _mixer_a(xf, a_w_in[0].astype(BF16), row(a_vnorm_g[0]), row(a_vnorm_b[0]),
                     a_w_s[0], jnp.transpose(a_b_s[0]), memkv, seq, t["mixer"])
    x1, x1b = _proj_ln(mixed, 0, mixed, MAIN_WIDTH // MEM_WIDTH, w_mix_out[0].astype(BF16), xf,
                       row(ln_g[0, 0]), row(ln_b[0, 0]), t["proj"])
    x2, x2b = _ffn_ln(x1, x1b, ffn_w_gate[0].astype(BF16), ffn_w_up[0].astype(BF16),
                      ffn_w_down[0].astype(BF16), row(ln_g[0, 1]), row(ln_b[0, 1]), t["ffn"], t["ffn_f"])

    w_cat = jnp.concatenate([b_w_in[0], shared_w_kv[:, :MAIN_WIDTH]], axis=1).astype(BF16)
    col_scale = jnp.concatenate([jnp.full((MAIN_WIDTH,), QK_DIM ** -0.5, F32),
                                 jnp.ones((w_cat.shape[1] - MAIN_WIDTH,), F32)]).reshape(1, -1)
    hk = _matmul_colscale(x2b, w_cat, col_scale, t["mm"], t["mm_n"])
    wv_t = jnp.transpose(shared_w_kv[:, MAIN_WIDTH:]).reshape(DIFF_HEADS, V_DIM, d)
    wv_t = jnp.pad(wv_t, ((0, 0), (0, ONES_ROWS), (0, 0))).reshape(DIFF_HEADS * V_AUG, d).astype(BF16)
    ones_bias = jnp.pad(jnp.zeros((DIFF_HEADS, V_DIM, 1), F32), ((0, 0), (0, ONES_ROWS), (0, 0)),
                        constant_values=1.0).reshape(DIFF_HEADS * V_AUG, 1)
    vt = _proj_transposed(x2b, wv_t, ones_bias, t["mm_t"])
    lambda_init = 0.8 - 0.6 * math.exp(-0.3 * 1)
    main = _diff_attn(hk, vt, row(b_lambda_q1[0]), row(b_lambda_k1[0]), row(b_lambda_q2[0]),
                      row(b_lambda_k2[0]), row(b_subln_g[0]), bsz, seq, lambda_init,
                      t["attn_q"], t["attn_k"])
    mem_out = _mem_attn_b(hk, memkv, seq, t["mem"])
    x3, _ = _proj_ln(main, 0, mem_out, 0, w_mix_out[1].astype(BF16), x2,
                     row(ln_g[1, 0]), row(ln_b[1, 0]), t["proj"])

    moe_tile = t["moe"]
    n_row_tiles = (2 * n) // moe_tile + N_EXPERTS
    w_router_pad = jnp.pad(moe_w_router[0], ((0, 0), (0, LANES - N_EXPERTS)))
    pos, gates, tile_info = _route(x3, w_router_pad, t["route"], moe_tile)
    tile_e = tile_info[0:8].reshape(-1)[:n_row_tiles]
    n_valid = tile_info[8, 0:1]
    tc = t["comb"]
    pos_flat = pos[:, :2].reshape(n // tc, 1, 2 * tc)
    xs = _dispatch(pos_flat, x3, jnp.zeros((n_row_tiles * moe_tile, d), F32), tc)
    ys = _experts(tile_e, n_valid, xs, moe_w_gate[0].astype(BF16), moe_w_up[0].astype(BF16),
                  moe_w_down[0].astype(BF16), moe_tile, t["moe_f"])
    x4 = _combine(pos_flat, gates, x3, row(ln_g[1, 1]), row(ln_b[1, 1]), ys, tc)
    return x4.reshape(bsz, seq, d)
```

```python
import functools
import math

import numpy as np
import jax
import jax.numpy as jnp
from jax import lax
from jax.experimental import pallas as pl
from jax.experimental.pallas import tpu as pltpu

BF16 = jnp.bfloat16
F32 = jnp.float32

D_MODEL = 2048
MEM_WIDTH = 512
MAIN_WIDTH = 1536
MEM_HEADS = 4
MEM_HEAD_DIM = 128
CHUNK = 128
SG_GROUPS = 12
QK_DIM = 128
V_DIM = 256
DIFF_HEADS = 6
N_EXPERTS = 8
DEPTH = 2
ALPHA = (2.0 * DEPTH) ** 0.25
LN_EPS = 1e-5
LANES = 128
ONES_ROWS = 16
V_AUG = V_DIM + ONES_ROWS
NEG_BIG = -1e30
DMA_LOOP_UNROLL = 8
MIB = 1024 * 1024


def _tiles(n_tokens, seq):
    def fit(pref, total):
        t = min(pref, total)
        while total % t:
            t //= 2
        return t
    return dict(
        mixer=fit(512, seq),
        proj=fit(512, n_tokens),
        ffn=fit(512, n_tokens),
        ffn_f=512,
        mm=fit(1024, n_tokens),
        mm_n=512,
        mm_t=fit(512, n_tokens),
        attn_q=fit(1024, seq),
        attn_k=fit(512, seq),
        mem=fit(512, seq),
        route=fit(512, n_tokens),
        moe=fit(512, n_tokens),
        moe_f=1024,
        comb=fit(256, n_tokens),
    )


def _params(sem, vmem_mib):
    return pltpu.CompilerParams(dimension_semantics=sem, vmem_limit_bytes=vmem_mib * MIB)


def _const_spec(shape, index_map):
    return pl.BlockSpec(shape, index_map, pipeline_mode=pl.Buffered(1))


def _dot(a, b):
    return jnp.dot(a, b, preferred_element_type=F32)


def _dot_nt(a, b):
    return lax.dot_general(a, b, (((1,), (1,)), ((), ())), preferred_element_type=F32)


def _layer_norm(v, g, b):
    mu = jnp.mean(v, axis=-1, keepdims=True)
    c = v - mu
    var = jnp.mean(c * c, axis=-1, keepdims=True)
    return c * lax.rsqrt(var + LN_EPS) * g + b


def _gelu_tanh(v):
    return 0.5 * v * (1.0 + jnp.tanh(math.sqrt(2.0 / math.pi) * (v + 0.044715 * (v * v * v))))


def _silu(v):
    return v / (1.0 + jnp.exp(-v))


def _mem_attention(q, kv_ref, o_ref, col0):
    scale = MEM_HEAD_DIM ** -0.5
    for h in range(MEM_HEADS):
        lo = h * MEM_HEAD_DIM
        qh = q[:, lo:lo + MEM_HEAD_DIM].astype(BF16)
        kh = kv_ref[:, lo:lo + MEM_HEAD_DIM]
        vh = kv_ref[:, MEM_WIDTH + lo:MEM_WIDTH + lo + MEM_HEAD_DIM]
        s = _dot_nt(qh, kh) * scale
        e = jnp.exp(s - jnp.max(s, axis=-1, keepdims=True))
        o = _dot(e.astype(BF16), vh) / jnp.sum(e, axis=-1, keepdims=True)
        o_ref[:, col0 + lo:col0 + lo + MEM_HEAD_DIM] = o.astype(o_ref.dtype)


def _memkv_kernel(mem_ref, w_ref, o_ref):
    o_ref[...] = _dot(mem_ref[...].astype(BF16), w_ref[...]).astype(o_ref.dtype)


def _memkv(mem, w_mem_kv_bf):
    bsz, n_mem, d = mem.shape
    depth, _, cols = w_mem_kv_bf.shape
    return pl.pallas_call(
        _memkv_kernel,
        grid=(depth, bsz),
        in_specs=[pl.BlockSpec((None, n_mem, d), lambda l, b: (b, 0, 0)),
                  pl.BlockSpec((None, d, cols), lambda l, b: (l, 0, 0))],
        out_specs=pl.BlockSpec((None, None, n_mem, cols), lambda l, b: (l, b, 0, 0)),
        out_shape=jax.ShapeDtypeStruct((depth, bsz, n_mem, cols), BF16),
        compiler_params=_params(("arbitrary", "arbitrary"), 32),
        name="memkv",
    )(mem, w_mem_kv_bf)


def _mixer_a_kernel(x_ref, w_ref, vg_ref, vb_ref, ws_ref, bs_ref, kv_ref, o_ref):
    tm = x_ref.shape[0]
    xb = x_ref[...].astype(BF16)
    v = _gelu_tanh(_dot(xb, w_ref[:, MAIN_WIDTH:2 * MAIN_WIDTH]))
    vn = _layer_norm(v, vg_ref[...], vb_ref[...]).astype(BF16)
    u = _gelu_tanh(_dot(xb, w_ref[:, :MAIN_WIDTH]))
    row = lax.broadcasted_iota(jnp.int32, (CHUNK, CHUNK), 0)
    col = lax.broadcasted_iota(jnp.int32, (CHUNK, CHUNK), 1)
    causal = col <= row
    for g in range(SG_GROUPS):
        wg = jnp.where(causal, ws_ref[g], 0.0).astype(BF16)
        bias = bs_ref[:, g:g + 1]
        for c in range(tm // CHUNK):
            rows = slice(c * CHUNK, (c + 1) * CHUNK)
            cols = slice(g * CHUNK, (g + 1) * CHUNK)
            s = _dot(wg, vn[rows, cols]) + bias
            o_ref[rows, cols] = (u[rows, cols] * s).astype(o_ref.dtype)
    q_mem = _dot(xb, w_ref[:, 2 * MAIN_WIDTH:])
    _mem_attention(q_mem, kv_ref, o_ref, MAIN_WIDTH)


def _mixer_a(x, w_in_bf, vnorm_g, vnorm_b, w_s, b_s_t, memkv, seq, tm):
    n, d = x.shape
    in_cols = w_in_bf.shape[1]
    return pl.pallas_call(
        _mixer_a_kernel,
        grid=(n // tm,),
        in_specs=[pl.BlockSpec((tm, d), lambda i: (i, 0)),
                  _const_spec((d, in_cols), lambda i: (0, 0)),
                  _const_spec((1, MAIN_WIDTH), lambda i: (0, 0)),
                  _const_spec((1, MAIN_WIDTH), lambda i: (0, 0)),
                  _const_spec((SG_GROUPS, CHUNK, CHUNK), lambda i: (0, 0, 0)),
                  _const_spec((CHUNK, SG_GROUPS), lambda i: (0, 0)),
                  pl.BlockSpec((None, None) + memkv.shape[2:], lambda i: (0, (i * tm) // seq, 0, 0))],
        out_specs=pl.BlockSpec((tm, d), lambda i: (i, 0)),
        out_shape=jax.ShapeDtypeStruct((n, d), BF16),
        compiler_params=_params(("arbitrary",), 56),
        name="mixer_a",
    )(x, w_in_bf, vnorm_g, vnorm_b, w_s, b_s_t, memkv)


def _proj_ln_kernel(a1_ref, a2_ref, w_ref, x_ref, g_ref, b_ref, o32_ref, o16_ref):
    mix = _dot(a1_ref[...], w_ref[:MAIN_WIDTH, :]) + _dot(a2_ref[...], w_ref[MAIN_WIDTH:, :])
    y = _layer_norm(ALPHA * x_ref[...] + mix, g_ref[...], b_ref[...])
    o32_ref[...] = y
    o16_ref[...] = y.astype(BF16)


def _proj_ln(a_main, main_blk, a_mem, mem_blk, w_bf, x, g, b, tm):
    n, d = x.shape
    return pl.pallas_call(
        _proj_ln_kernel,
        grid=(n // tm,),
        in_specs=[pl.BlockSpec((tm, MAIN_WIDTH), lambda i: (i, main_blk)),
                  pl.BlockSpec((tm, MEM_WIDTH), lambda i: (i, mem_blk)),
                  _const_spec((d, d), lambda i: (0, 0)),
                  pl.BlockSpec((tm, d), lambda i: (i, 0)),
                  _const_spec((1, d), lambda i: (0, 0)),
                  _const_spec((1, d), lambda i: (0, 0))],
        out_specs=[pl.BlockSpec((tm, d), lambda i: (i, 0)),
                   pl.BlockSpec((tm, d), lambda i: (i, 0))],
        out_shape=[jax.ShapeDtypeStruct((n, d), F32), jax.ShapeDtypeStruct((n, d), BF16)],
        compiler_params=_params(("arbitrary",), 48),
        name="proj_ln",
    )(a_main, a_mem, w_bf, x, g, b)


def _ffn_ln_kernel(x_ref, xb_ref, wg_ref, wu_ref, wd_ref, g_ref, b_ref, o32_ref, o16_ref, acc_ref):
    j = pl.program_id(1)

    @pl.when(j == 0)
    def _():
        acc_ref[...] = jnp.zeros_like(acc_ref)

    xb = xb_ref[...]
    h = (_silu(_dot(xb, wg_ref[...])) * _dot(xb, wu_ref[...])).astype(BF16)
    acc_ref[...] += _dot(h, wd_ref[...])

    @pl.when(j == pl.num_programs(1) - 1)
    def _():
        y = _layer_norm(ALPHA * x_ref[...] + acc_ref[...], g_ref[...], b_ref[...])
        o32_ref[...] = y
        o16_ref[...] = y.astype(BF16)


def _ffn_ln(x, xb, wg_bf, wu_bf, wd_bf, g, b, tm, tf):
    n, d = x.shape
    f = wg_bf.shape[1]
    return pl.pallas_call(
        _ffn_ln_kernel,
        grid=(n // tm, f // tf),
        in_specs=[pl.BlockSpec((tm, d), lambda i, j: (i, 0)),
                  pl.BlockSpec((tm, d), lambda i, j: (i, 0)),
                  pl.BlockSpec((d, tf), lambda i, j: (0, j)),
                  pl.BlockSpec((d, tf), lambda i, j: (0, j)),
                  pl.BlockSpec((tf, d), lambda i, j: (j, 0)),
                  _const_spec((1, d), lambda i, j: (0, 0)),
                  _const_spec((1, d), lambda i, j: (0, 0))],
        out_specs=[pl.BlockSpec((tm, d), lambda i, j: (i, 0)),
                   pl.BlockSpec((tm, d), lambda i, j: (i, 0))],
        out_shape=[jax.ShapeDtypeStruct((n, d), F32), jax.ShapeDtypeStruct((n, d), BF16)],
        scratch_shapes=[pltpu.VMEM((tm, d), F32)],
        compiler_params=_params(("arbitrary", "arbitrary"), 56),
        name="ffn_ln",
    )(x, xb, wg_bf, wu_bf, wd_bf, g, b)


def _matmul_kernel(x_ref, w_ref, s_ref, o_ref):
    o_ref[...] = (_dot(x_ref[...], w_ref[...]) * s_ref[...]).astype(o_ref.dtype)


def _matmul_colscale(xb, w_bf, col_scale, tm, tn):
    n, d = xb.shape
    cols = w_bf.shape[1]
    return pl.pallas_call(
        _matmul_kernel,
        grid=(n // tm, cols // tn),
        in_specs=[pl.BlockSpec((tm, d), lambda i, j: (i, 0)),
                  pl.BlockSpec((d, tn), lambda i, j: (0, j)),
                  pl.BlockSpec((1, tn), lambda i, j: (0, j))],
        out_specs=pl.BlockSpec((tm, tn), lambda i, j: (i, j)),
        out_shape=jax.ShapeDtypeStruct((n, cols), BF16),
        compiler_params=_params(("arbitrary", "arbitrary"), 48),
        name="in_proj_b",
    )(xb, w_bf, col_scale)


def _proj_t_kernel(wt_ref, bias_ref, x_ref, o_ref):
    o_ref[...] = (_dot_nt(wt_ref[...], x_ref[...]) + bias_ref[...]).astype(o_ref.dtype)


def _proj_transposed(xb, wt_bf, bias_col, tn):
    n, d = xb.shape
    cols = wt_bf.shape[0]
    return pl.pallas_call(
        _proj_t_kernel,
        grid=(n // tn,),
        in_specs=[_const_spec((cols, d), lambda i: (0, 0)),
                  _const_spec((cols, 1), lambda i: (0, 0)),
                  pl.BlockSpec((tn, d), lambda i: (i, 0))],
        out_specs=pl.BlockSpec((cols, tn), lambda i: (0, i)),
        out_shape=jax.ShapeDtypeStruct((cols, n), BF16),
        compiler_params=_params(("arbitrary",), 48),
        name="v_proj_t",
    )(wt_bf, bias_col, xb)


def _diff_attn_kernel(qi_tab, kj_tab, q1_ref, q2_ref, k1_ref, k2_ref, vt_ref,
                      lq1_ref, lk1_ref, lq2_ref, lk2_ref, sg_ref, o_ref,
                      m1_ref, a1_ref, m2_ref, a2_ref, *, lambda_init, qw):
    tq, tk = q1_ref.shape[0], k1_ref.shape[0]
    p = pl.program_id(2)
    qi = qi_tab[p]
    kj = kj_tab[p]

    @pl.when(kj == 0)
    def _():
        for m_ref, a_ref in ((m1_ref, a1_ref), (m2_ref, a2_ref)):
            m_ref[...] = jnp.full_like(m_ref, NEG_BIG)
            a_ref[...] = jnp.zeros_like(a_ref)

    chains = [(q_ref, k_ref, m_ref, a_ref, j)
              for j in range(tq // qw)
              for q_ref, k_ref, m_ref, a_ref in ((q1_ref, k1_ref, m1_ref, a1_ref),
                                                 (q2_ref, k2_ref, m2_ref, a2_ref))]

    def step(key0):
        if key0 is None:
            live = chains
        else:
            live = [c for c in chains if key0 <= (c[4] + 1) * qw - 1]

        def scores(chain):
            q_ref, k_ref, _, _, j = chain
            s = _dot_nt(k_ref[...], q_ref[j * qw:(j + 1) * qw, :])
            if key0 is not None and key0 + tk - 1 > j * qw:
                key = key0 + lax.broadcasted_iota(jnp.int32, (tk, qw), 0)
                qry = j * qw + lax.broadcasted_iota(jnp.int32, (tk, qw), 1)
                s = jnp.where(key <= qry, s, NEG_BIG)
            return s.astype(BF16)

        def softmax(chain, s):
            m_ref, j = chain[2], chain[4]
            cols = slice(j * qw, (j + 1) * qw)
            m_old = m_ref[:, cols]
            m_new = jnp.maximum(m_old, jnp.max(s, axis=0, keepdims=True).astype(F32))
            alpha = jnp.exp(m_old - m_new)
            e = jnp.exp(s - m_new.astype(BF16))
            m_ref[:, cols] = m_new
            return e, alpha

        def weighted_values(chain, e, alpha):
            a_ref, j = chain[3], chain[4]
            cols = slice(j * qw, (j + 1) * qw)
            a_ref[:, cols] = alpha * a_ref[:, cols] + _dot(vt_ref[...], e)

        n = len(live)
        s_live, e_live = {}, {}
        for t in range(n + 2):
            if t < n:
                s_live[t] = scores(live[t])
            if 0 <= t - 1 < n:
                e_live[t - 1] = softmax(live[t - 1], s_live.pop(t - 1))
            if 0 <= t - 2 < n:
                weighted_values(live[t - 2], *e_live.pop(t - 2))

    key0 = kj * tk - qi * tq
    pl.when(key0 < 0)(lambda: step(None))
    for static_key0 in range(0, tq, tk):
        pl.when(key0 == static_key0)(functools.partial(step, static_key0))

    @pl.when(kj == ((qi + 1) * tq - 1) // tk)
    def _():
        lam = (jnp.exp(jnp.sum(lq1_ref[...] * lk1_ref[...], axis=-1, keepdims=True))
               - jnp.exp(jnp.sum(lq2_ref[...] * lk2_ref[...], axis=-1, keepdims=True))
               + lambda_init)
        o1 = a1_ref[0:V_DIM, :] / a1_ref[V_DIM:V_DIM + 1, :]
        o2 = a2_ref[0:V_DIM, :] / a2_ref[V_DIM:V_DIM + 1, :]
        o = o1 - lam * o2
        o = o * lax.rsqrt(jnp.mean(o * o, axis=0, keepdims=True) + LN_EPS) * sg_ref[...]
        o_ref[...] = jnp.transpose(o * (1.0 - lambda_init)).astype(o_ref.dtype)


def _diff_attn(hk, vt, lq1, lk1, lq2, lk2, subln_g, bsz, seq, lambda_init, tq, tk):
    n = hk.shape[0]
    assert tq % tk == 0 and tk >= 2
    nq = seq // tq
    pairs =[(qi, kj) for qi in range(nq) for kj in range(((qi + 1) * tq - 1) // tk + 1)]
    qi_tab = jnp.asarray(np.array([p[0] for p in pairs], np.int32))
    kj_tab = jnp.asarray(np.array([p[1] for p in pairs], np.int32))
    nqb, nkb = seq // tq, seq // tk
    k_blk0 = (MAIN_WIDTH + MEM_WIDTH) // QK_DIM
    qspec = lambda off: pl.BlockSpec((tq, QK_DIM), lambda b, h, p, qt, kt: (b * nqb + qt[p], off + h))
    kspec = lambda off: pl.BlockSpec((tk, QK_DIM), lambda b, h, p, qt, kt: (b * nkb + kt[p], off + h))
    vec = lambda w: pl.BlockSpec((1, w), lambda b, h, p, qt, kt: (0, 0))
    grid_spec = pltpu.PrefetchScalarGridSpec(
        num_scalar_prefetch=2,
        grid=(bsz, DIFF_HEADS, len(pairs)),
        in_specs=[qspec(0), qspec(DIFF_HEADS), kspec(k_blk0), kspec(k_blk0 + DIFF_HEADS),
                  pl.BlockSpec((V_AUG, tk), lambda b, h, p, qt, kt: (h, b * nkb + kt[p])),
                  vec(QK_DIM), vec(QK_DIM), vec(QK_DIM), vec(QK_DIM),
                  pl.BlockSpec((V_DIM, 1), lambda b, h, p, qt, kt: (0, 0))],
        out_specs=pl.BlockSpec((tq, V_DIM), lambda b, h, p, qt, kt: (b * nqb + qt[p], h)),
        scratch_shapes=[pltpu.VMEM((1, tq), F32), pltpu.VMEM((V_AUG, tq), F32),
                        pltpu.VMEM((1, tq), F32), pltpu.VMEM((V_AUG, tq), F32)],
    )
    return pl.pallas_call(
        functools.partial(_diff_attn_kernel, lambda_init=lambda_init, qw=min(tq, 2 * LANES)),
        grid_spec=grid_spec,
        out_shape=jax.ShapeDtypeStruct((n, MAIN_WIDTH), BF16),
        compiler_params=_params(("arbitrary", "arbitrary", "arbitrary"), 48),
        name="diff_attn",
    )(qi_tab, kj_tab, hk, hk, hk, hk, vt, lq1, lk1, lq2, lk2, subln_g.reshape(V_DIM, 1))


def _mem_attn_kernel(q_ref, kv_ref, o_ref):
    _mem_attention(q_ref[...].astype(F32), kv_ref, o_ref, 0)


def _mem_attn_b(hk, memkv, seq, tm):
    n = hk.shape[0]
    return pl.pallas_call(
        _mem_attn_kernel,
        grid=(n // tm,),
        in_specs=[pl.BlockSpec((tm, MEM_WIDTH), lambda i: (i, MAIN_WIDTH // MEM_WIDTH)),
                  pl.BlockSpec((None, None) + memkv.shape[2:], lambda i: (1, (i * tm) // seq, 0, 0))],
        out_specs=pl.BlockSpec((tm, MEM_WIDTH), lambda i: (i, 0)),
        out_shape=jax.ShapeDtypeStruct((n, MEM_WIDTH), BF16),
        compiler_params=_params(("arbitrary",), 32),
        name="mem_attn_b",
    )(hk, memkv)


def _lane_cumsum(v, lane):
    for shift in (1, 2, 4):
        v = v + jnp.where(lane >= shift, pltpu.roll(v, shift, 1), 0.0)
    return v


def _route_kernel(x_ref, wr_ref, pos_ref, gate_ref, tile_e_ref, cnt_ref, base_ref, start_ref, *, moe_tile):
    phase = pl.program_id(0)
    i = pl.program_id(1)
    tm = x_ref.shape[0]
    lane = lax.broadcasted_iota(jnp.int32, (tm, LANES), 1).astype(F32)

    x = x_ref[...]
    xh = x.astype(BF16)
    xl = (x - xh.astype(F32)).astype(BF16)
    w = wr_ref[...]
    wh = w.astype(BF16)
    wl = (w - wh.astype(F32)).astype(BF16)
    logits = _dot(xh, wh) + (_dot(xh, wl) + _dot(xl, wh))
    logits = jnp.where(lane < N_EXPERTS, logits, NEG_BIG)

    v1 = jnp.max(logits, axis=-1, keepdims=True)
    i1 = jnp.min(jnp.where(logits == v1, lane, float(LANES)), axis=-1, keepdims=True)
    rest = jnp.where(lane == i1, NEG_BIG, logits)
    v2 = jnp.max(rest, axis=-1, keepdims=True)
    i2 = jnp.min(jnp.where(rest == v2, lane, float(LANES)), axis=-1, keepdims=True)
    sel = jnp.logical_or(lane == i1, lane == i2)
    tile_cnt = jnp.sum(sel.astype(F32), axis=0, keepdims=True)

    @pl.when(jnp.logical_and(phase == 0, i == 0))
    def _():
        cnt_ref[...] = jnp.zeros_like(cnt_ref)

    @pl.when(phase == 0)
    def _():
        cnt_ref[...] += tile_cnt

    @pl.when(jnp.logical_and(phase == 1, i == 0))
    def _():
        lane8 = lax.broadcasted_iota(jnp.int32, (8, LANES), 1).astype(F32)
        sub8 = lax.broadcasted_iota(jnp.int32, (8, LANES), 0).astype(F32)
        cnt = jnp.broadcast_to(cnt_ref[...], (8, LANES))
        padded = jnp.ceil(cnt * (1.0 / moe_tile)) * moe_tile
        ends = _lane_cumsum(padded, lane8)
        start_ref[...] = (ends - padded)[0:1, :]
        base_ref[...] = jnp.zeros_like(base_ref)
        tile_row0 = (sub8 * LANES + lane8) * moe_tile
        tile_e = jnp.zeros((8, LANES), F32)
        for e in range(N_EXPERTS):
            end_e = jnp.sum(jnp.where(lane8 == e, ends, 0.0), axis=-1, keepdims=True)
            tile_e = tile_e + (tile_row0 >= end_e).astype(F32)
        tile_e_ref[0:8, :] = jnp.minimum(tile_e, N_EXPERTS - 1.0).astype(jnp.int32)
        total = jnp.sum(jnp.where(lane8 == N_EXPERTS - 1, ends, 0.0), axis=-1, keepdims=True)
        tile_e_ref[8:16, :] = jnp.broadcast_to(total * (1.0 / moe_tile), (8, LANES)).astype(jnp.int32)

    @pl.when(phase == 1)
    def _():
        r = lax.broadcasted_iota(jnp.int32, (tm, tm), 0)
        c = lax.broadcasted_iota(jnp.int32, (tm, tm), 1)
        before = (c < r).astype(BF16)
        rank = _dot(before, sel.astype(BF16))
        slot = start_ref[...] + base_ref[...] + rank
        p1 = jnp.sum(jnp.where(lane == i1, slot, 0.0), axis=-1, keepdims=True)
        p2 = jnp.sum(jnp.where(lane == i2, slot, 0.0), axis=-1, keepdims=True)
        pos_ref[...] = jnp.where(lane == 0, p1, jnp.where(lane == 1, p2, 0.0)).astype(jnp.int32)
        g1 = 1.0 / (1.0 + jnp.exp(v2 - v1))
        g2 = jnp.exp(v2 - v1) * g1
        gate_ref[...] = jnp.where(lane == 0, g1, jnp.where(lane == 1, g2, 0.0))
        base_ref[...] += tile_cnt


def _route(x, w_router_pad, tm, moe_tile):
    n, d = x.shape
    return pl.pallas_call(
        functools.partial(_route_kernel, moe_tile=moe_tile),
        grid=(2, n // tm),
        in_specs=[pl.BlockSpec((tm, d), lambda ph, i: (i, 0)),
                  _const_spec((d, LANES), lambda ph, i: (0, 0))],
        out_specs=[pl.BlockSpec((tm, LANES), lambda ph, i: (i * ph, 0)),
                   pl.BlockSpec((tm, LANES), lambda ph, i: (i * ph, 0)),
                   pl.BlockSpec((16, LANES), lambda ph, i: (0, 0))],
        out_shape=[jax.ShapeDtypeStruct((n, LANES), jnp.int32),
                   jax.ShapeDtypeStruct((n, LANES), F32),
                   jax.ShapeDtypeStruct((16, LANES), jnp.int32)],
        scratch_shapes=[pltpu.VMEM((1, LANES), F32), pltpu.VMEM((1, LANES), F32), pltpu.VMEM((1, LANES), F32)],
        compiler_params=_params(("arbitrary", "arbitrary"), 32),
        name="route",
    )(x, w_router_pad)


def _dispatch_kernel(pos_ref, x_ref, xs_in_ref, xs_ref, sem):
    del xs_in_ref
    tm = x_ref.shape[0]

    def issue(t, carry):
        for k in range(2):
            dst = pos_ref[0, 0, 2 * t + k]
            pltpu.make_async_copy(x_ref.at[pl.ds(t, 1)], xs_ref.at[pl.ds(dst, 1)], sem).start(priority=k)
        return carry

    lax.fori_loop(0, tm, issue, 0, unroll=DMA_LOOP_UNROLL)

    def drain(t, carry):
        pltpu.make_async_copy(x_ref.at[pl.ds(0, 1)], xs_ref.at[pl.ds(0, 1)], sem).wait()
        return carry

    lax.fori_loop(0, 2 * tm, drain, 0, unroll=2 * DMA_LOOP_UNROLL)


def _dispatch(pos_flat, x, xs_zero, tm):
    n, d = x.shape
    return pl.pallas_call(
        _dispatch_kernel,
        grid=(n // tm,),
        in_specs=[pl.BlockSpec((1, 1, 2 * tm), lambda i: (i, 0, 0), memory_space=pltpu.SMEM),
                  pl.BlockSpec((tm, d), lambda i: (i, 0)),
                  pl.BlockSpec(memory_space=pl.ANY)],
        out_specs=pl.BlockSpec(memory_space=pl.ANY),
        out_shape=jax.ShapeDtypeStruct(xs_zero.shape, xs_zero.dtype),
        scratch_shapes=[pltpu.SemaphoreType.DMA(())],
        input_output_aliases={2: 0},
        compiler_params=_params(("arbitrary",), 32),
        name="moe_dispatch",
    )(pos_flat, x, xs_zero)


def _experts_kernel(tile_e, n_valid, xs_ref, wg_ref, wu_ref, wd_ref, o_ref, xb_ref):
    del tile_e
    i = pl.program_id(0)
    j = pl.program_id(1)
    valid = i < n_valid[0]

    @pl.when(jnp.logical_and(valid, j == 0))
    def _():
        xb_ref[...] = xs_ref[...].astype(BF16)

    @pl.when(j == 0)
    def _():
        o_ref[...] = jnp.zeros_like(o_ref)

    @pl.when(valid)
    def _():
        xb = xb_ref[...]
        h = (_silu(_dot(xb, wg_ref[...])) * _dot(xb, wu_ref[...])).astype(BF16)
        o_ref[...] += _dot(h, wd_ref[...])


def _experts(tile_e, n_valid, xs, wg_bf, wu_bf, wd_bf, tm, tf):
    m, d = xs.shape
    f = wg_bf.shape[2]
    nj = f // tf

    def row_idx(i, j, te, nv):
        return (jnp.minimum(i, nv[0] - 1), 0)

    def col_of(i, j, nv):
        return jnp.where(i < nv[0], j, nj - 1)

    grid_spec = pltpu.PrefetchScalarGridSpec(
        num_scalar_prefetch=2,
        grid=(m // tm, nj),
        in_specs=[pl.BlockSpec((tm, d), row_idx),
                  pl.BlockSpec((None, d, tf), lambda i, j, te, nv: (te[i], 0, col_of(i, j, nv))),
                  pl.BlockSpec((None, d, tf), lambda i, j, te, nv: (te[i], 0, col_of(i, j, nv))),
                  pl.BlockSpec((None, tf, d), lambda i, j, te, nv: (te[i], col_of(i, j, nv), 0))],
        out_specs=pl.BlockSpec((tm, d), lambda i, j, te, nv: (i, 0)),
        scratch_shapes=[pltpu.VMEM((tm, d), BF16)],
    )
    return pl.pallas_call(
        _experts_kernel,
        grid_spec=grid_spec,
        out_shape=jax.ShapeDtypeStruct((m, d), F32),
        compiler_params=_params(("arbitrary", "arbitrary"), 56),
        name="moe_experts",
    )(tile_e, n_valid, xs, wg_bf, wu_bf, wd_bf)


def _combine_kernel(pos_ref, gate_ref, x_ref, g_ref, b_ref, ys_ref, o_ref, buf_ref, sem):
    tm = x_ref.shape[0]

    def issue(t, carry):
        for k in range(2):
            src = pos_ref[0, 0, 2 * t + k]
            pltpu.make_async_copy(ys_ref.at[pl.ds(src, 1)], buf_ref.at[k, pl.ds(t, 1)], sem).start(priority=k)
        return carry

    lax.fori_loop(0, tm, issue, 0, unroll=DMA_LOOP_UNROLL)

    def drain(t, carry):
        pltpu.make_async_copy(ys_ref.at[pl.ds(0, 1)], buf_ref.at[0, pl.ds(0, 1)], sem).wait()
        return carry

    lax.fori_loop(0, 2 * tm, drain, 0, unroll=2 * DMA_LOOP_UNROLL)

    gates = gate_ref[...]
    y = gates[:, 0:1] * buf_ref[0] + gates[:, 1:2] * buf_ref[1]
    o_ref[...] = _layer_norm(ALPHA * x_ref[...] + y, g_ref[...], b_ref[...])


def _combine(pos_flat, gates, x, g, b, ys, tm):
    n, d = x.shape
    return pl.pallas_call(
        _combine_kernel,
        grid=(n // tm,),
        in_specs=[pl.BlockSpec((1, 1, 2 * tm), lambda i: (i, 0, 0), memory_space=pltpu.SMEM),
                  pl.BlockSpec((tm, LANES), lambda i: (i, 0)),
                  pl.BlockSpec((tm, d), lambda i: (i, 0)),
                  _const_spec((1, d), lambda i: (0, 0)),
                  _const_spec((1, d), lambda i: (0, 0)),
                  pl.BlockSpec(memory_space=pl.ANY)],
        out_specs=pl.BlockSpec((tm, d), lambda i: (i, 0)),
        out_shape=jax.ShapeDtypeStruct((n, d), F32),
        scratch_shapes=[pltpu.VMEM((2, tm, d), F32), pltpu.SemaphoreType.DMA(())],
        compiler_params=_params(("arbitrary",), 32),
        name="moe_combine",
    )(pos_flat, gates, x, g, b, ys)


def kernel(x, mem, ln_g, ln_b, w_mix_out, w_mem_kv, a_w_in, a_vnorm_g, a_vnorm_b, a_w_s, a_b_s,
           shared_w_kv, b_w_in, b_lambda_q1, b_lambda_k1, b_lambda_q2, b_lambda_k2, b_subln_g,
           ffn_w_gate, ffn_w_up, ffn_w_down, moe_w_router, moe_w_gate, moe_w_up, moe_w_down):
    bsz, seq, d = x.shape
    n = bsz * seq
    t = _tiles(n, seq)
    xf = x.reshape(n, d)
    row = lambda v: v.reshape(1, -1)

    memkv = _memkv(mem, w_mem_kv.astype(BF16))

    mixed = _mixer_a(xf, a_w_in[0].astype(BF16), row(a_vnorm_g[0]), row(a_vnorm_b[0]),
                     a_w_s[0], jnp.transpose(a_b_s[0]), memkv, seq, t["mixer"])
    x1, x1b = _proj_ln(mixed, 0, mixed, MAIN_WIDTH // MEM_WIDTH, w_mix_out[0].astype(BF16), xf,
                       row(ln_g[0, 0]), row(ln_b[0, 0]), t["proj"])
    x2, x2b = _ffn_ln(x1, x1b, ffn_w_gate[0].astype(BF16), ffn_w_up[0].astype(BF16),
                      ffn_w_down[0].astype(BF16), row(ln_g[0, 1]), row(ln_b[0, 1]), t["ffn"], t["ffn_f"])

    w_cat = jnp.concatenate([b_w_in[0], shared_w_kv[:, :MAIN_WIDTH]], axis=1).astype(BF16)
    col_scale = jnp.concatenate([jnp.full((MAIN_WIDTH,), QK_DIM ** -0.5, F32),
                                 jnp.ones((w_cat.shape[1] - MAIN_WIDTH,), F32)]).reshape(1, -1)
    hk = _matmul_colscale(x2b, w_cat, col_scale, t["mm"], t["mm_n"])
    wv_t = jnp.transpose(shared_w_kv[:, MAIN_WIDTH:]).reshape(DIFF_HEADS, V_DIM, d)
    wv_t = jnp.pad(wv_t, ((0, 0), (0, ONES_ROWS), (0, 0))).reshape(DIFF_HEADS * V_AUG, d).astype(BF16)
    ones_bias = jnp.pad(jnp.zeros((DIFF_HEADS, V_DIM, 1), F32), ((0, 0), (0, ONES_ROWS), (0, 0)),
                        constant_values=1.0).reshape(DIFF_HEADS * V_AUG, 1)
    vt = _proj_transposed(x2b, wv_t, ones_bias, t["mm_t"])
    lambda_init = 0.8 - 0.6 * math.exp(-0.3 * 1)
    main = _diff_attn(hk, vt, row(b_lambda_q1[0]), row(b_lambda_k1[0]), row(b_lambda_q2[0]),
                      row(b_lambda_k2[0]), row(b_subln_g[0]), bsz, seq, lambda_init,
                      t["attn_q"], t["attn_k"])
    mem_out = _mem_attn_b(hk, memkv, seq, t["mem"])
    x3, _ = _proj_ln(main, 0, mem_out, 0, w_mix_out[1].astype(BF16), x2,
                     row(ln_g[1, 0]), row(ln_b[1, 0]), t["proj"])

    moe_tile = t["moe"]
    n_row_tiles = (2 * n) // moe_tile + N_EXPERTS
    w_router_pad = jnp.pad(moe_w_router[0], ((0, 0), (0, LANES - N_EXPERTS)))
    pos, gates, tile_info = _route(x3, w_router_pad, t["route"], moe_tile)
    tile_e = tile_info[0:8].reshape(-1)[:n_row_tiles]
    n_valid = tile_info[8, 0:1]
    tc = t["comb"]
    pos_flat = pos[:, :2].reshape(n // tc, 1, 2 * tc)
    xs = _dispatch(pos_flat, x3, jnp.zeros((n_row_tiles * moe_tile, d), F32), tc)
    ys = _experts(tile_e, n_valid, xs, moe_w_gate[0].astype(BF16), moe_w_up[0].astype(BF16),
                  moe_w_down[0].astype(BF16), moe_tile, t["moe_f"])
    x4 = _combine(pos_flat, gates, x3, row(ln_g[1, 1]), row(ln_b[1, 1]), ys, tc)
    return x4.reshape(bsz, seq, d)
```

```python
import functools
import math

import numpy as np
import jax
import jax.numpy as jnp
from jax import lax
from jax.experimental import pallas as pl
from jax.experimental.pallas import tpu as pltpu

BF16 = jnp.bfloat16
F32 = jnp.float32

D_MODEL = 2048
MEM_WIDTH = 512
MAIN_WIDTH = 1536
MEM_HEADS = 4
MEM_HEAD_DIM = 128
CHUNK = 128
SG_GROUPS = 12
QK_DIM = 128
V_DIM = 256
DIFF_HEADS = 6
N_EXPERTS = 8
DEPTH = 2
ALPHA = (2.0 * DEPTH) ** 0.25
LN_EPS = 1e-5
LANES = 128
ONES_ROWS = 16
V_AUG = V_DIM + ONES_ROWS
NEG_BIG = -1e30
DMA_LOOP_UNROLL = 8
MIB = 1024 * 1024


def _tiles(n_tokens, seq):
    def fit(pref, total):
        t = min(pref, total)
        while total % t:
            t //= 2
        return t
    return dict(
        mixer=fit(512, seq),
        proj=fit(512, n_tokens),
        ffn=fit(512, n_tokens),
        ffn_f=512,
        mm=fit(1024, n_tokens),
        mm_n=512,
        mm_t=fit(512, n_tokens),
        attn_q=fit(1024, seq),
        attn_k=fit(1024, seq),
        mem=fit(512, seq),
        route=fit(512, n_tokens),
        moe=fit(512, n_tokens),
        moe_f=1024,
        comb=fit(512, n_tokens),
    )


def _params(sem, vmem_mib):
    return pltpu.CompilerParams(dimension_semantics=sem, vmem_limit_bytes=vmem_mib * MIB)


def _const_spec(shape, index_map):
    return pl.BlockSpec(shape, index_map, pipeline_mode=pl.Buffered(1))


def _dot(a, b):
    return jnp.dot(a, b, preferred_element_type=F32)


def _dot_nt(a, b):
    return lax.dot_general(a, b, (((1,), (1,)), ((), ())), preferred_element_type=F32)


def _layer_norm(v, g, b):
    mu = jnp.mean(v, axis=-1, keepdims=True)
    c = v - mu
    var = jnp.mean(c * c, axis=-1, keepdims=True)
    return c * lax.rsqrt(var + LN_EPS) * g + b


def _gelu_tanh(v):
    return 0.5 * v * (1.0 + jnp.tanh(math.sqrt(2.0 / math.pi) * (v + 0.044715 * (v * v * v))))


def _silu(v):
    return v / (1.0 + jnp.exp(-v))


def _mem_attention(q, kv_ref, o_ref, col0):
    scale = MEM_HEAD_DIM ** -0.5
    for h in range(MEM_HEADS):
        lo = h * MEM_HEAD_DIM
        qh = q[:, lo:lo + MEM_HEAD_DIM].astype(BF16)
        kh = kv_ref[:, lo:lo + MEM_HEAD_DIM]
        vh = kv_ref[:, MEM_WIDTH + lo:MEM_WIDTH + lo + MEM_HEAD_DIM]
        s = _dot_nt(qh, kh) * scale
        e = jnp.exp(s - jnp.max(s, axis=-1, keepdims=True))
        o = _dot(e.astype(BF16), vh) / jnp.sum(e, axis=-1, keepdims=True)
        o_ref[:, col0 + lo:col0 + lo + MEM_HEAD_DIM] = o.astype(o_ref.dtype)


def _memkv_kernel(mem_ref, w_ref, o_ref):
    o_ref[...] = _dot(mem_ref[...].astype(BF16), w_ref[...]).astype(o_ref.dtype)


def _memkv(mem, w_mem_kv_bf):
    bsz, n_mem, d = mem.shape
    depth, _, cols = w_mem_kv_bf.shape
    return pl.pallas_call(
        _memkv_kernel,
        grid=(depth, bsz),
        in_specs=[pl.BlockSpec((None, n_mem, d), lambda l, b: (b, 0, 0)),
                  pl.BlockSpec((None, d, cols), lambda l, b: (l, 0, 0))],
        out_specs=pl.BlockSpec((None, None, n_mem, cols), lambda l, b: (l, b, 0, 0)),
        out_shape=jax.ShapeDtypeStruct((depth, bsz, n_mem, cols), BF16),
        compiler_params=_params(("arbitrary", "arbitrary"), 32),
        name="memkv",
    )(mem, w_mem_kv_bf)


def _mixer_a_kernel(x_ref, w_ref, vg_ref, vb_ref, ws_ref, bs_ref, kv_ref, o_ref):
    tm = x_ref.shape[0]
    xb = x_ref[...].astype(BF16)
    v = _gelu_tanh(_dot(xb, w_ref[:, MAIN_WIDTH:2 * MAIN_WIDTH]))
    vn = _layer_norm(v, vg_ref[...], vb_ref[...]).astype(BF16)
    u = _gelu_tanh(_dot(xb, w_ref[:, :MAIN_WIDTH]))
    row = lax.broadcasted_iota(jnp.int32, (CHUNK, CHUNK), 0)
    col = lax.broadcasted_iota(jnp.int32, (CHUNK, CHUNK), 1)
    causal = col <= row
    for g in range(SG_GROUPS):
        wg = jnp.where(causal, ws_ref[g], 0.0).astype(BF16)
        bias = bs_ref[:, g:g + 1]
        for c in range(tm // CHUNK):
            rows = slice(c * CHUNK, (c + 1) * CHUNK)
            cols = slice(g * CHUNK, (g + 1) * CHUNK)
            s = _dot(wg, vn[rows, cols]) + bias
            o_ref[rows, cols] = (u[rows, cols] * s).astype(o_ref.dtype)
    q_mem = _dot(xb, w_ref[:, 2 * MAIN_WIDTH:])
    _mem_attention(q_mem, kv_ref, o_ref, MAIN_WIDTH)


def _mixer_a(x, w_in_bf, vnorm_g, vnorm_b, w_s, b_s_t, memkv, seq, tm):
    n, d = x.shape
    in_cols = w_in_bf.shape[1]
    return pl.pallas_call(
        _mixer_a_kernel,
        grid=(n // tm,),
        in_specs=[pl.BlockSpec((tm, d), lambda i: (i, 0)),
                  _const_spec((d, in_cols), lambda i: (0, 0)),
                  _const_spec((1, MAIN_WIDTH), lambda i: (0, 0)),
                  _const_spec((1, MAIN_WIDTH), lambda i: (0, 0)),
                  _const_spec((SG_GROUPS, CHUNK, CHUNK), lambda i: (0, 0, 0)),
                  _const_spec((CHUNK, SG_GROUPS), lambda i: (0, 0)),
                  pl.BlockSpec((None, None) + memkv.shape[2:], lambda i: (0, (i * tm) // seq, 0, 0))],
        out_specs=pl.BlockSpec((tm, d), lambda i: (i, 0)),
        out_shape=jax.ShapeDtypeStruct((n, d), BF16),
        compiler_params=_params(("arbitrary",), 56),
        name="mixer_a",
    )(x, w_in_bf, vnorm_g, vnorm_b, w_s, b_s_t, memkv)


def _proj_ln_kernel(a1_ref, a2_ref, w_ref, x_ref, g_ref, b_ref, o32_ref, o16_ref, *, sub):
    for r in range(0, x_ref.shape[0], sub):
        rows = slice(r, r + sub)
        mix = _dot(a1_ref[rows, :], w_ref[:MAIN_WIDTH, :]) + _dot(a2_ref[rows, :], w_ref[MAIN_WIDTH:, :])
        y = _layer_norm(ALPHA * x_ref[rows, :] + mix, g_ref[...], b_ref[...])
        o32_ref[rows, :] = y
        o16_ref[rows, :] = y.astype(BF16)


def _proj_ln(a_main, main_blk, a_mem, mem_blk, w_bf, x, g, b, tm):
    n, d = x.shape
    return pl.pallas_call(
        functools.partial(_proj_ln_kernel, sub=min(tm, 2 * LANES)),
        grid=(n // tm,),
        in_specs=[pl.BlockSpec((tm, MAIN_WIDTH), lambda i: (i, main_blk)),
                  pl.BlockSpec((tm, MEM_WIDTH), lambda i: (i, mem_blk)),
                  _const_spec((d, d), lambda i: (0, 0)),
                  pl.BlockSpec((tm, d), lambda i: (i, 0)),
                  _const_spec((1, d), lambda i: (0, 0)),
                  _const_spec((1, d), lambda i: (0, 0))],
        out_specs=[pl.BlockSpec((tm, d), lambda i: (i, 0)),
                   pl.BlockSpec((tm, d), lambda i: (i, 0))],
        out_shape=[jax.ShapeDtypeStruct((n, d), F32), jax.ShapeDtypeStruct((n, d), BF16)],
        compiler_params=_params(("arbitrary",), 48),
        name="proj_ln",
    )(a_main, a_mem, w_bf, x, g, b)


def _ffn_ln_kernel(x_ref, xb_ref, wg_ref, wu_ref, wd_ref, g_ref, b_ref, o32_ref, o16_ref, acc_ref):
    j = pl.program_id(1)

    @pl.when(j == 0)
    def _():
        acc_ref[...] = jnp.zeros_like(acc_ref)

    xb = xb_ref[...]
    h = (_silu(_dot(xb, wg_ref[...])) * _dot(xb, wu_ref[...])).astype(BF16)
    acc_ref[...] += _dot(h, wd_ref[...])

    @pl.when(j == pl.num_programs(1) - 1)
    def _():
        y = _layer_norm(ALPHA * x_ref[...] + acc_ref[...], g_ref[...], b_ref[...])
        o32_ref[...] = y
        o16_ref[...] = y.astype(BF16)


def _ffn_ln(x, xb, wg_bf, wu_bf, wd_bf, g, b, tm, tf):
    n, d = x.shape
    f = wg_bf.shape[1]
    return pl.pallas_call(
        _ffn_ln_kernel,
        grid=(n // tm, f // tf),
        in_specs=[pl.BlockSpec((tm, d), lambda i, j: (i, 0)),
                  pl.BlockSpec((tm, d), lambda i, j: (i, 0)),
                  pl.BlockSpec((d, tf), lambda i, j: (0, j)),
                  pl.BlockSpec((d, tf), lambda i, j: (0, j)),
                  pl.BlockSpec((tf, d), lambda i, j: (j, 0)),
                  _const_spec((1, d), lambda i, j: (0, 0)),
                  _const_spec((1, d), lambda i, j: (0, 0))],
        out_specs=[pl.BlockSpec((tm, d), lambda i, j: (i, 0)),
                   pl.BlockSpec((tm, d), lambda i, j: (i, 0))],
        out_shape=[jax.ShapeDtypeStruct((n, d), F32), jax.ShapeDtypeStruct((n, d), BF16)],
        scratch_shapes=[pltpu.VMEM((tm, d), F32)],
        compiler_params=_params(("arbitrary", "arbitrary"), 56),
        name="ffn_ln",
    )(x, xb, wg_bf, wu_bf, wd_bf, g, b)


def _matmul_kernel(x_ref, w_ref, s_ref, o_ref):
    o_ref[...] = (_dot(x_ref[...], w_ref[...]) * s_ref[...]).astype(o_ref.dtype)


def _matmul_colscale(xb, w_bf, col_scale, tm, tn):
    n, d = xb.shape
    cols = w_bf.shape[1]
    return pl.pallas_call(
        _matmul_kernel,
        grid=(n // tm, cols // tn),
        in_specs=[pl.BlockSpec((tm, d), lambda i, j: (i, 0)),
                  pl.BlockSpec((d, tn), lambda i, j: (0, j)),
                  pl.BlockSpec((1, tn), lambda i, j: (0, j))],
        out_specs=pl.BlockSpec((tm, tn), lambda i, j: (i, j)),
        out_shape=jax.ShapeDtypeStruct((n, cols), BF16),
        compiler_params=_params(("arbitrary", "arbitrary"), 48),
        name="in_proj_b",
    )(xb, w_bf, col_scale)


def _proj_t_kernel(wt_ref, bias_ref, x_ref, o_ref):
    o_ref[...] = (_dot_nt(wt_ref[...], x_ref[...]) + bias_ref[...]).astype(o_ref.dtype)


def _proj_transposed(xb, wt_bf, bias_col, tn):
    n, d = xb.shape
    cols = wt_bf.shape[0]
    return pl.pallas_call(
        _proj_t_kernel,
        grid=(n // tn,),
        in_specs=[_const_spec((cols, d), lambda i: (0, 0)),
                  _const_spec((cols, 1), lambda i: (0, 0)),
                  pl.BlockSpec((tn, d), lambda i: (i, 0))],
        out_specs=pl.BlockSpec((cols, tn), lambda i: (0, i)),
        out_shape=jax.ShapeDtypeStruct((cols, n), BF16),
        compiler_params=_params(("arbitrary",), 48),
        name="v_proj_t",
    )(wt_bf, bias_col, xb)


def _diff_attn_kernel(qi_tab, kj_tab, q1_ref, q2_ref, k1_ref, k2_ref, vt_ref,
                      lq1_ref, lk1_ref, lq2_ref, lk2_ref, sg_ref, o_ref,
                      m1_ref, a1_ref, m2_ref, a2_ref, *, lambda_init, qw):
    tq, tk = q1_ref.shape[0], k1_ref.shape[0]
    p = pl.program_id(2)
    qi = qi_tab[p]
    kj = kj_tab[p]

    @pl.when(kj == 0)
    def _():
        for m_ref, a_ref in ((m1_ref, a1_ref), (m2_ref, a2_ref)):
            m_ref[...] = jnp.full_like(m_ref, NEG_BIG)
            a_ref[...] = jnp.zeros_like(a_ref)

    chains = [(q_ref, k_ref, m_ref, a_ref, j)
              for j in range(tq // qw)
              for q_ref, k_ref, m_ref, a_ref in ((q1_ref, k1_ref, m1_ref, a1_ref),
                                                 (q2_ref, k2_ref, m2_ref, a2_ref))]

    def step(key0):
        if key0 is None:
            live = chains
        else:
            live = [c for c in chains if key0 <= (c[4] + 1) * qw - 1]

        def scores(chain):
            q_ref, k_ref, _, _, j = chain
            s = _dot_nt(k_ref[...], q_ref[j * qw:(j + 1) * qw, :])
            if key0 is not None and key0 + tk - 1 > j * qw:
                key = key0 + lax.broadcasted_iota(jnp.int32, (tk, qw), 0)
                qry = j * qw + lax.broadcasted_iota(jnp.int32, (tk, qw), 1)
                s = jnp.where(key <= qry, s, NEG_BIG)
            return s.astype(BF16)

        def softmax(chain, s):
            m_ref, j = chain[2], chain[4]
            cols = slice(j * qw, (j + 1) * qw)
            m_old = m_ref[:, cols]
            m_new = jnp.maximum(m_old, jnp.max(s, axis=0, keepdims=True).astype(F32))
            alpha = jnp.exp(m_old - m_new)
            e = jnp.exp(s - m_new.astype(BF16))
            m_ref[:, cols] = m_new
            return e, alpha

        def weighted_values(chain, e, alpha):
            a_ref, j = chain[3], chain[4]
            cols = slice(j * qw, (j + 1) * qw)
            a_ref[:, cols] = alpha * a_ref[:, cols] + _dot(vt_ref[...], e)

        n = len(live)
        s_live, e_live = {}, {}
        for t in range(n + 2):
            if t < n:
                s_live[t] = scores(live[t])
            if 0 <= t - 1 < n:
                e_live[t - 1] = softmax(live[t - 1], s_live.pop(t - 1))
            if 0 <= t - 2 < n:
                weighted_values(live[t - 2], *e_live.pop(t - 2))

    key0 = kj * tk - qi * tq
    pl.when(key0 < 0)(lambda: step(None))
    for static_key0 in range(0, tq, tk):
        pl.when(key0 == static_key0)(functools.partial(step, static_key0))

    @pl.when(kj == ((qi + 1) * tq - 1) // tk)
    def _():
        lam = (jnp.exp(jnp.sum(lq1_ref[...] * lk1_ref[...], axis=-1, keepdims=True))
               - jnp.exp(jnp.sum(lq2_ref[...] * lk2_ref[...], axis=-1, keepdims=True))
               + lambda_init)
        o1 = a1_ref[0:V_DIM, :] / a1_ref[V_DIM:V_DIM + 1, :]
        o2 = a2_ref[0:V_DIM, :] / a2_ref[V_DIM:V_DIM + 1, :]
        o = o1 - lam * o2
        o = o * lax.rsqrt(jnp.mean(o * o, axis=0, keepdims=True) + LN_EPS) * sg_ref[...]
        o_ref[...] = jnp.transpose(o * (1.0 - lambda_init)).astype(o_ref.dtype)


def _diff_attn(hk, vt, lq1, lk1, lq2, lk2, subln_g, bsz, seq, lambda_init, tq, tk):
    n = hk.shape[0]
    assert tq % tk == 0 and tk >= 2
    nq = seq // tq
    pairs =[(qi, kj) for qi in range(nq) for kj in range(((qi + 1) * tq - 1) // tk + 1)]
    qi_tab = jnp.asarray(np.array([p[0] for p in pairs], np.int32))
    kj_tab = jnp.asarray(np.array([p[1] for p in pairs], np.int32))
    nqb, nkb = seq // tq, seq // tk
    k_blk0 = (MAIN_WIDTH + MEM_WIDTH) // QK_DIM
    qspec = lambda off: pl.BlockSpec((tq, QK_DIM), lambda b, h, p, qt, kt: (b * nqb + qt[p], off + h))
    kspec = lambda off: pl.BlockSpec((tk, QK_DIM), lambda b, h, p, qt, kt: (b * nkb + kt[p], off + h))
    vec = lambda w: pl.BlockSpec((1, w), lambda b, h, p, qt, kt: (0, 0))
    grid_spec = pltpu.PrefetchScalarGridSpec(
        num_scalar_prefetch=2,
        grid=(bsz, DIFF_HEADS, len(pairs)),
        in_specs=[qspec(0), qspec(DIFF_HEADS), kspec(k_blk0), kspec(k_blk0 + DIFF_HEADS),
                  pl.BlockSpec((V_AUG, tk), lambda b, h, p, qt, kt: (h, b * nkb + kt[p])),
                  vec(QK_DIM), vec(QK_DIM), vec(QK_DIM), vec(QK_DIM),
                  pl.BlockSpec((V_DIM, 1), lambda b, h, p, qt, kt: (0, 0))],
        out_specs=pl.BlockSpec((tq, V_DIM), lambda b, h, p, qt, kt: (b * nqb + qt[p], h)),
        scratch_shapes=[pltpu.VMEM((1, tq), F32), pltpu.VMEM((V_AUG, tq), F32),
                        pltpu.VMEM((1, tq), F32), pltpu.VMEM((V_AUG, tq), F32)],
    )
    return pl.pallas_call(
        functools.partial(_diff_attn_kernel, lambda_init=lambda_init, qw=min(tq, 2 * LANES)),
        grid_spec=grid_spec,
        out_shape=jax.ShapeDtypeStruct((n, MAIN_WIDTH), BF16),
        compiler_params=_params(("arbitrary", "arbitrary", "arbitrary"), 48),
        name="diff_attn",
    )(qi_tab, kj_tab, hk, hk, hk, hk, vt, lq1, lk1, lq2, lk2, subln_g.reshape(V_DIM, 1))


def _mem_attn_kernel(q_ref, kv_ref, o_ref):
    _mem_attention(q_ref[...].astype(F32), kv_ref, o_ref, 0)


def _mem_attn_b(hk, memkv, seq, tm):
    n = hk.shape[0]
    return pl.pallas_call(
        _mem_attn_kernel,
        grid=(n // tm,),
        in_specs=[pl.BlockSpec((tm, MEM_WIDTH), lambda i: (i, MAIN_WIDTH // MEM_WIDTH)),
                  pl.BlockSpec((None, None) + memkv.shape[2:], lambda i: (1, (i * tm) // seq, 0, 0))],
        out_specs=pl.BlockSpec((tm, MEM_WIDTH), lambda i: (i, 0)),
        out_shape=jax.ShapeDtypeStruct((n, MEM_WIDTH), BF16),
        compiler_params=_params(("arbitrary",), 32),
        name="mem_attn_b",
    )(hk, memkv)


def _lane_cumsum(v, lane):
    for shift in (1, 2, 4):
        v = v + jnp.where(lane >= shift, pltpu.roll(v, shift, 1), 0.0)
    return v


def _route_kernel(x_ref, wr_ref, pos_ref, gate_ref, tile_e_ref, cnt_ref, base_ref, start_ref, *, moe_tile):
    phase = pl.program_id(0)
    i = pl.program_id(1)
    tm = x_ref.shape[0]
    lane = lax.broadcasted_iota(jnp.int32, (tm, LANES), 1).astype(F32)

    x = x_ref[...]
    xh = x.astype(BF16)
    xl = (x - xh.astype(F32)).astype(BF16)
    w = wr_ref[...]
    wh = w.astype(BF16)
    wl = (w - wh.astype(F32)).astype(BF16)
    logits = _dot(xh, wh) + (_dot(xh, wl) + _dot(xl, wh))
    logits = jnp.where(lane < N_EXPERTS, logits, NEG_BIG)

    v1 = jnp.max(logits, axis=-1, keepdims=True)
    i1 = jnp.min(jnp.where(logits == v1, lane, float(LANES)), axis=-1, keepdims=True)
    rest = jnp.where(lane == i1, NEG_BIG, logits)
    v2 = jnp.max(rest, axis=-1, keepdims=True)
    i2 = jnp.min(jnp.where(rest == v2, lane, float(LANES)), axis=-1, keepdims=True)
    sel = jnp.logical_or(lane == i1, lane == i2)
    tile_cnt = jnp.sum(sel.astype(F32), axis=0, keepdims=True)

    @pl.when(jnp.logical_and(phase == 0, i == 0))
    def _():
        cnt_ref[...] = jnp.zeros_like(cnt_ref)

    @pl.when(phase == 0)
    def _():
        cnt_ref[...] += tile_cnt

    @pl.when(jnp.logical_and(phase == 1, i == 0))
    def _():
        lane8 = lax.broadcasted_iota(jnp.int32, (8, LANES), 1).astype(F32)
        sub8 = lax.broadcasted_iota(jnp.int32, (8, LANES), 0).astype(F32)
        cnt = jnp.broadcast_to(cnt_ref[...], (8, LANES))
        padded = jnp.ceil(cnt * (1.0 / moe_tile)) * moe_tile
        ends = _lane_cumsum(padded, lane8)
        start_ref[...] = (ends - padded)[0:1, :]
        base_ref[...] = jnp.zeros_like(base_ref)
        tile_row0 = (sub8 * LANES + lane8) * moe_tile
        tile_e = jnp.zeros((8, LANES), F32)
        for e in range(N_EXPERTS):
            end_e = jnp.sum(jnp.where(lane8 == e, ends, 0.0), axis=-1, keepdims=True)
            tile_e = tile_e + (tile_row0 >= end_e).astype(F32)
        tile_e_ref[0:8, :] = jnp.minimum(tile_e, N_EXPERTS - 1.0).astype(jnp.int32)
        total = jnp.sum(jnp.where(lane8 == N_EXPERTS - 1, ends, 0.0), axis=-1, keepdims=True)
        tile_e_ref[8:16, :] = jnp.broadcast_to(total * (1.0 / moe_tile), (8, LANES)).astype(jnp.int32)
        tile_e_ref[16:24, :] = (ends - padded + cnt).astype(jnp.int32)

    @pl.when(phase == 1)
    def _():
        r = lax.broadcasted_iota(jnp.int32, (tm, tm), 0)
        c = lax.broadcasted_iota(jnp.int32, (tm, tm), 1)
        before = (c < r).astype(BF16)
        rank = _dot(before, sel.astype(BF16))
        slot = start_ref[...] + base_ref[...] + rank
        p1 = jnp.sum(jnp.where(lane == i1, slot, 0.0), axis=-1, keepdims=True)
        p2 = jnp.sum(jnp.where(lane == i2, slot, 0.0), axis=-1, keepdims=True)
        pos_ref[...] = jnp.where(lane == 0, p1, jnp.where(lane == 1, p2, 0.0)).astype(jnp.int32)
        g1 = 1.0 / (1.0 + jnp.exp(v2 - v1))
        g2 = jnp.exp(v2 - v1) * g1
        gate_ref[...] = jnp.where(lane == 0, g1, jnp.where(lane == 1, g2, 0.0))
        base_ref[...] += tile_cnt


def _route(x, w_router_pad, tm, moe_tile):
    n, d = x.shape
    return pl.pallas_call(
        functools.partial(_route_kernel, moe_tile=moe_tile),
        grid=(2, n // tm),
        in_specs=[pl.BlockSpec((tm, d), lambda ph, i: (i, 0)),
                  _const_spec((d, LANES), lambda ph, i: (0, 0))],
        out_specs=[pl.BlockSpec((tm, LANES), lambda ph, i: (i * ph, 0)),
                   pl.BlockSpec((tm, LANES), lambda ph, i: (i * ph, 0)),
                   pl.BlockSpec((24, LANES), lambda ph, i: (0, 0))],
        out_shape=[jax.ShapeDtypeStruct((n, LANES), jnp.int32),
                   jax.ShapeDtypeStruct((n, LANES), F32),
                   jax.ShapeDtypeStruct((24, LANES), jnp.int32)],
        scratch_shapes=[pltpu.VMEM((1, LANES), F32), pltpu.VMEM((1, LANES), F32), pltpu.VMEM((1, LANES), F32)],
        compiler_params=_params(("arbitrary", "arbitrary"), 32),
        name="route",
    )(x, w_router_pad)


def _dispatch_kernel(pad_row0, pos_ref, x_ref, xs_ref, zero_ref, sem, *, moe_tile):
    tm = x_ref.shape[0]

    @pl.when(pl.program_id(0) == 0)
    def _():
        zero_ref[...] = jnp.zeros_like(zero_ref)
        zr = zero_ref.shape[0]

        def drain_fill(r, carry):
            pltpu.make_async_copy(zero_ref.at[pl.ds(0, 1)], xs_ref.at[pl.ds(0, 1)], sem).wait()
            return carry

        for e in range(N_EXPERTS):
            def fill(r, carry, e=e):
                pltpu.make_async_copy(zero_ref.at[pl.ds(0, 1)],
                                      xs_ref.at[pl.ds(pad_row0[e] + r, 1)], sem).start()
                return carry

            lax.fori_loop(0, moe_tile, fill, 0, unroll=DMA_LOOP_UNROLL)
            lax.fori_loop(0, moe_tile, drain_fill, 0, unroll=2 * DMA_LOOP_UNROLL)

        n_alloc = xs_ref.shape[0]
        tail = [pltpu.make_async_copy(
                    zero_ref,
                    xs_ref.at[pl.ds(pl.multiple_of(
                        jnp.minimum(pad_row0[N_EXPERTS] + c * zr, n_alloc - zr), zr), zr)], sem)
                for c in range((N_EXPERTS + 1) * moe_tile // zr)]
        for f in tail:
            f.start()
            f.wait()

    def issue(t, carry):
        for k in range(2):
            dst = pos_ref[0, 0, 2 * t + k]
            pltpu.make_async_copy(x_ref.at[pl.ds(t, 1)], xs_ref.at[pl.ds(dst, 1)], sem).start(priority=k)
        return carry

    lax.fori_loop(0, tm, issue, 0, unroll=DMA_LOOP_UNROLL)

    def drain(t, carry):
        pltpu.make_async_copy(x_ref.at[pl.ds(0, 1)], xs_ref.at[pl.ds(0, 1)], sem).wait()
        return carry

    lax.fori_loop(0, 2 * tm, drain, 0, unroll=2 * DMA_LOOP_UNROLL)


def _dispatch(pad_row0, pos_flat, x, n_rows, moe_tile, tm):
    n, d = x.shape
    zero_rows = min(moe_tile, LANES)
    grid_spec = pltpu.PrefetchScalarGridSpec(
        num_scalar_prefetch=1,
        grid=(n // tm,),
        in_specs=[pl.BlockSpec((1, 1, 2 * tm), lambda i, pr: (i, 0, 0), memory_space=pltpu.SMEM),
                  pl.BlockSpec((tm, d), lambda i, pr: (i, 0))],
        out_specs=pl.BlockSpec(memory_space=pl.ANY),
        scratch_shapes=[pltpu.VMEM((zero_rows, d), F32), pltpu.SemaphoreType.DMA(())],
    )
    return pl.pallas_call(
        functools.partial(_dispatch_kernel, moe_tile=moe_tile),
        grid_spec=grid_spec,
        out_shape=jax.ShapeDtypeStruct((n_rows + moe_tile, d), F32),
        compiler_params=_params(("arbitrary",), 32),
        name="moe_dispatch",
    )(pad_row0, pos_flat, x)


def _experts_kernel(tile_e, n_valid, xs_ref, wg_ref, wu_ref, wd_ref, o_ref, xb_ref):
    del tile_e
    i = pl.program_id(0)
    j = pl.program_id(1)
    valid = i < n_valid[0]

    @pl.when(jnp.logical_and(valid, j == 0))
    def _():
        xb_ref[...] = xs_ref[...].astype(BF16)

    @pl.when(j == 0)
    def _():
        o_ref[...] = jnp.zeros_like(o_ref)

    @pl.when(valid)
    def _():
        xb = xb_ref[...]
        h = (_silu(_dot(xb, wg_ref[...])) * _dot(xb, wu_ref[...])).astype(BF16)
        o_ref[...] += _dot(h, wd_ref[...])


def _experts(tile_e, n_valid, xs, wg_bf, wu_bf, wd_bf, tm, tf):
    d = xs.shape[1]
    m = tile_e.shape[0] * tm
    f = wg_bf.shape[2]
    nj = f // tf

    def row_idx(i, j, te, nv):
        return (jnp.minimum(i, nv[0] - 1), 0)

    def col_of(i, j, nv):
        return jnp.where(i < nv[0], j, nj - 1)

    grid_spec = pltpu.PrefetchScalarGridSpec(
        num_scalar_prefetch=2,
        grid=(m // tm, nj),
        in_specs=[pl.BlockSpec((tm, d), row_idx),
                  pl.BlockSpec((None, d, tf), lambda i, j, te, nv: (te[i], 0, col_of(i, j, nv))),
                  pl.BlockSpec((None, d, tf), lambda i, j, te, nv: (te[i], 0, col_of(i, j, nv))),
                  pl.BlockSpec((None, tf, d), lambda i, j, te, nv: (te[i], col_of(i, j, nv), 0))],
        out_specs=pl.BlockSpec((tm, d), lambda i, j, te, nv: (i, 0)),
        scratch_shapes=[pltpu.VMEM((tm, d), BF16)],
    )
    return pl.pallas_call(
        _experts_kernel,
        grid_spec=grid_spec,
        out_shape=jax.ShapeDtypeStruct((m, d), F32),
        compiler_params=_params(("arbitrary", "arbitrary"), 56),
        name="moe_experts",
    )(tile_e, n_valid, xs, wg_bf, wu_bf, wd_bf)


def _combine_kernel(pos_ref, gate_ref, x_ref, g_ref, b_ref, ys_ref, o_ref, buf_ref, sem):
    tm = x_ref.shape[0]

    def issue(t, carry):
        for k in range(2):
            src = pos_ref[0, 0, 2 * t + k]
            pltpu.make_async_copy(ys_ref.at[pl.ds(src, 1)], buf_ref.at[k, pl.ds(t, 1)], sem).start(priority=k)
        return carry

    lax.fori_loop(0, tm, issue, 0, unroll=DMA_LOOP_UNROLL)

    def drain(t, carry):
        pltpu.make_async_copy(ys_ref.at[pl.ds(0, 1)], buf_ref.at[0, pl.ds(0, 1)], sem).wait()
        return carry

    lax.fori_loop(0, 2 * tm, drain, 0, unroll=2 * DMA_LOOP_UNROLL)

    gates = gate_ref[...]
    y = gates[:, 0:1] * buf_ref[0] + gates[:, 1:2] * buf_ref[1]
    o_ref[...] = _layer_norm(ALPHA * x_ref[...] + y, g_ref[...], b_ref[...])


def _combine(pos_flat, gates, x, g, b, ys, tm):
    n, d = x.shape
    return pl.pallas_call(
        _combine_kernel,
        grid=(n // tm,),
        in_specs=[pl.BlockSpec((1, 1, 2 * tm), lambda i: (i, 0, 0), memory_space=pltpu.SMEM),
                  pl.BlockSpec((tm, LANES), lambda i: (i, 0)),
                  pl.BlockSpec((tm, d), lambda i: (i, 0)),
                  _const_spec((1, d), lambda i: (0, 0)),
                  _const_spec((1, d), lambda i: (0, 0)),
                  pl.BlockSpec(memory_space=pl.ANY)],
        out_specs=pl.BlockSpec((tm, d), lambda i: (i, 0)),
        out_shape=jax.ShapeDtypeStruct((n, d), F32),
        scratch_shapes=[pltpu.VMEM((2, tm, d), F32), pltpu.SemaphoreType.DMA(())],
        compiler_params=_params(("arbitrary",), 32),
        name="moe_combine",
    )(pos_flat, gates, x, g, b, ys)


def kernel(x, mem, ln_g, ln_b, w_mix_out, w_mem_kv, a_w_in, a_vnorm_g, a_vnorm_b, a_w_s, a_b_s,
           shared_w_kv, b_w_in, b_lambda_q1, b_lambda_k1, b_lambda_q2, b_lambda_k2, b_subln_g,
           ffn_w_gate, ffn_w_up, ffn_w_down, moe_w_router, moe_w_gate, moe_w_up, moe_w_down):
    bsz, seq, d = x.shape
    n = bsz * seq
    t = _tiles(n, seq)
    xf = x.reshape(n, d)
    row = lambda v: v.reshape(1, -1)

    memkv = _memkv(mem, w_mem_kv.astype(BF16))

    mixed = _mixer_a(xf, a_w_in[0].astype(BF16), row(a_vnorm_g[0]), row(a_vnorm_b[0]),
                     a_w_s[0], jnp.transpose(a_b_s[0]), memkv, seq, t["mixer"])
    x1, x1b = _proj_ln(mixed, 0, mixed, MAIN_WIDTH // MEM_WIDTH, w_mix_out[0].astype(BF16), xf,
                       row(ln_g[0, 0]), row(ln_b[0, 0]), t["proj"])
    x2, x2b = _ffn_ln(x1, x1b, ffn_w_gate[0].astype(BF16), ffn_w_up[0].astype(BF16),
                      ffn_w_down[0].astype(BF16), row(ln_g[0, 1]), row(ln_b[0, 1]), t["ffn"], t["ffn_f"])

    w_cat = jnp.concatenate([b_w_in[0], shared_w_kv[:, :MAIN_WIDTH]], axis=1).astype(BF16)
    col_scale = jnp.concatenate([jnp.full((MAIN_WIDTH,), QK_DIM ** -0.5, F32),
                                 jnp.ones((w_cat.shape[1] - MAIN_WIDTH,), F32)]).reshape(1, -1)
    hk = _matmul_colscale(x2b, w_cat, col_scale, t["mm"], t["mm_n"])
    wv_t = jnp.transpose(shared_w_kv[:, MAIN_WIDTH:]).reshape(DIFF_HEADS, V_DIM, d)
    wv_t = jnp.pad(wv_t, ((0, 0), (0, ONES_ROWS), (0, 0))).reshape(DIFF_HEADS * V_AUG, d).astype(BF16)
    ones_bias = jnp.pad(jnp.zeros((DIFF_HEADS, V_DIM, 1), F32), ((0, 0), (0, ONES_ROWS), (0, 0)),
                        constant_values=1.0).reshape(DIFF_HEADS * V_AUG, 1)
    vt = _proj_transposed(x2b, wv_t, ones_bias, t["mm_t"])
    lambda_init = 0.8 - 0.6 * math.exp(-0.3 * 1)
    main = _diff_attn(hk, vt, row(b_lambda_q1[0]), row(b_lambda_k1[0]), row(b_lambda_q2[0]),
                      row(b_lambda_k2[0]), row(b_subln_g[0]), bsz, seq, lambda_init,
                      t["attn_q"], t["attn_k"])
    mem_out = _mem_attn_b(hk, memkv, seq, t["mem"])
    x3, _ = _proj_ln(main, 0, mem_out, 0, w_mix_out[1].astype(BF16), x2,
                     row(ln_g[1, 0]), row(ln_b[1, 0]), t["proj"])

    moe_tile = t["moe"]
    n_row_tiles = (2 * n) // moe_tile + N_EXPERTS
    w_router_pad = jnp.pad(moe_w_router[0], ((0, 0), (0, LANES - N_EXPERTS)))
    pos, gates, tile_info = _route(x3, w_router_pad, t["route"], moe_tile)
    tile_e = tile_info[0:8].reshape(-1)[:n_row_tiles]
    n_valid = tile_info[8, 0:1]
    tc = t["comb"]
    pos_flat = pos[:, :2].reshape(n // tc, 1, 2 * tc)
    fill_rows = jnp.concatenate([tile_info[16, :N_EXPERTS], n_valid * moe_tile])
    xs = _dispatch(fill_rows, pos_flat, x3, n_row_tiles * moe_tile, moe_tile, tc)
    ys = _experts(tile_e, n_valid, xs, moe_w_gate[0].astype(BF16), moe_w_up[0].astype(BF16),
                  moe_w_down[0].astype(BF16), moe_tile, t["moe_f"])
    x4 = _combine(pos_flat, gates, x3, row(ln_g[1, 1]), row(ln_b[1, 1]), ys, tc)
    return x4.reshape(bsz, seq, d)
```

```python
import functools
import math

import numpy as np
import jax
import jax.numpy as jnp
from jax import lax
from jax.experimental import pallas as pl
from jax.experimental.pallas import tpu as pltpu

BF16 = jnp.bfloat16
F32 = jnp.float32

D_MODEL = 2048
MEM_WIDTH = 512
MAIN_WIDTH = 1536
MEM_HEADS = 4
MEM_HEAD_DIM = 128
CHUNK = 128
SG_GROUPS = 12
QK_DIM = 128
V_DIM = 256
DIFF_HEADS = 6
N_EXPERTS = 8
DEPTH = 2
ALPHA = (2.0 * DEPTH) ** 0.25
LN_EPS = 1e-5
LANES = 128
ONES_ROWS = 16
V_AUG = V_DIM + ONES_ROWS
NEG_BIG = -1e30
DMA_LOOP_UNROLL = 8
MIB = 1024 * 1024


def _tiles(n_tokens, seq):
    def fit(pref, total):
        t = min(pref, total)
        while total % t:
            t //= 2
        return t
    return dict(
        mixer=fit(512, seq),
        proj=fit(512, n_tokens),
        ffn=fit(512, n_tokens),
        ffn_f=512,
        mm=fit(1024, n_tokens),
        mm_n=512,
        mm_t=fit(512, n_tokens),
        attn_q=fit(1024, seq),
        attn_k=fit(1024, seq),
        mem=fit(512, seq),
        route=fit(512, n_tokens),
        moe=fit(512, n_tokens),
        moe_f=1024,
        comb=fit(512, n_tokens),
    )


def _params(sem, vmem_mib):
    return pltpu.CompilerParams(dimension_semantics=sem, vmem_limit_bytes=vmem_mib * MIB)


def _const_spec(shape, index_map):
    return pl.BlockSpec(shape, index_map, pipeline_mode=pl.Buffered(1))


def _dot(a, b):
    return jnp.dot(a, b, preferred_element_type=F32)


def _dot_nt(a, b):
    return lax.dot_general(a, b, (((1,), (1,)), ((), ())), preferred_element_type=F32)


def _layer_norm(v, g, b):
    mu = jnp.mean(v, axis=-1, keepdims=True)
    c = v - mu
    var = jnp.mean(c * c, axis=-1, keepdims=True)
    return c * lax.rsqrt(var + LN_EPS) * g + b


def _gelu_tanh(v):
    return 0.5 * v * (1.0 + jnp.tanh(math.sqrt(2.0 / math.pi) * (v + 0.044715 * (v * v * v))))


def _silu(v):
    return v / (1.0 + jnp.exp(-v))


def _mem_attention(q, kv_ref, o_ref, col0):
    scale = MEM_HEAD_DIM ** -0.5
    for h in range(MEM_HEADS):
        lo = h * MEM_HEAD_DIM
        qh = q[:, lo:lo + MEM_HEAD_DIM].astype(BF16)
        kh = kv_ref[:, lo:lo + MEM_HEAD_DIM]
        vh = kv_ref[:, MEM_WIDTH + lo:MEM_WIDTH + lo + MEM_HEAD_DIM]
        s = _dot_nt(qh, kh) * scale
        e = jnp.exp(s - jnp.max(s, axis=-1, keepdims=True))
        o = _dot(e.astype(BF16), vh) / jnp.sum(e, axis=-1, keepdims=True)
        o_ref[:, col0 + lo:col0 + lo + MEM_HEAD_DIM] = o.astype(o_ref.dtype)


def _memkv_kernel(mem_ref, w_ref, o_ref):
    o_ref[...] = _dot(mem_ref[...].astype(BF16), w_ref[...]).astype(o_ref.dtype)


def _memkv(mem, w_mem_kv_bf):
    bsz, n_mem, d = mem.shape
    depth, _, cols = w_mem_kv_bf.shape
    return pl.pallas_call(
        _memkv_kernel,
        grid=(depth, bsz),
        in_specs=[pl.BlockSpec((None, n_mem, d), lambda l, b: (b, 0, 0)),
                  pl.BlockSpec((None, d, cols), lambda l, b: (l, 0, 0))],
        out_specs=pl.BlockSpec((None, None, n_mem, cols), lambda l, b: (l, b, 0, 0)),
        out_shape=jax.ShapeDtypeStruct((depth, bsz, n_mem, cols), BF16),
        compiler_params=_params(("arbitrary", "arbitrary"), 32),
        name="memkv",
    )(mem, w_mem_kv_bf)


def _mixer_a_kernel(x_ref, w_ref, vg_ref, vb_ref, ws_ref, bs_ref, kv_ref, o_ref):
    tm = x_ref.shape[0]
    xb = x_ref[...].astype(BF16)
    v = _gelu_tanh(_dot(xb, w_ref[:, MAIN_WIDTH:2 * MAIN_WIDTH]))
    vn = _layer_norm(v, vg_ref[...], vb_ref[...]).astype(BF16)
    u = _gelu_tanh(_dot(xb, w_ref[:, :MAIN_WIDTH]))
    row = lax.broadcasted_iota(jnp.int32, (CHUNK, CHUNK), 0)
    col = lax.broadcasted_iota(jnp.int32, (CHUNK, CHUNK), 1)
    causal = col <= row
    for g in range(SG_GROUPS):
        wg = jnp.where(causal, ws_ref[g], 0.0).astype(BF16)
        bias = bs_ref[:, g:g + 1]
        for c in range(tm // CHUNK):
            rows = slice(c * CHUNK, (c + 1) * CHUNK)
            cols = slice(g * CHUNK, (g + 1) * CHUNK)
            s = _dot(wg, vn[rows, cols]) + bias
            o_ref[rows, cols] = (u[rows, cols] * s).astype(o_ref.dtype)
    q_mem = _dot(xb, w_ref[:, 2 * MAIN_WIDTH:])
    _mem_attention(q_mem, kv_ref, o_ref, MAIN_WIDTH)


def _mixer_a(x, w_in_bf, vnorm_g, vnorm_b, w_s, b_s_t, memkv, seq, tm):
    n, d = x.shape
    in_cols = w_in_bf.shape[1]
    return pl.pallas_call(
        _mixer_a_kernel,
        grid=(n // tm,),
        in_specs=[pl.BlockSpec((tm, d), lambda i: (i, 0)),
                  _const_spec((d, in_cols), lambda i: (0, 0)),
                  _const_spec((1, MAIN_WIDTH), lambda i: (0, 0)),
                  _const_spec((1, MAIN_WIDTH), lambda i: (0, 0)),
                  _const_spec((SG_GROUPS, CHUNK, CHUNK), lambda i: (0, 0, 0)),
                  _const_spec((CHUNK, SG_GROUPS), lambda i: (0, 0)),
                  pl.BlockSpec((None, None) + memkv.shape[2:], lambda i: (0, (i * tm) // seq, 0, 0))],
        out_specs=pl.BlockSpec((tm, d), lambda i: (i, 0)),
        out_shape=jax.ShapeDtypeStruct((n, d), BF16),
        compiler_params=_params(("arbitrary",), 56),
        name="mixer_a",
    )(x, w_in_bf, vnorm_g, vnorm_b, w_s, b_s_t, memkv)


def _proj_ln_kernel(a1_ref, a2_ref, w_ref, x_ref, g_ref, b_ref, o32_ref, o16_ref, *, sub):
    for r in range(0, x_ref.shape[0], sub):
        rows = slice(r, r + sub)
        mix = _dot(a1_ref[rows, :], w_ref[:MAIN_WIDTH, :]) + _dot(a2_ref[rows, :], w_ref[MAIN_WIDTH:, :])
        y = _layer_norm(ALPHA * x_ref[rows, :] + mix, g_ref[...], b_ref[...])
        o32_ref[rows, :] = y
        o16_ref[rows, :] = y.astype(BF16)


def _proj_ln(a_main, main_blk, a_mem, mem_blk, w_bf, x, g, b, tm):
    n, d = x.shape
    return pl.pallas_call(
        functools.partial(_proj_ln_kernel, sub=min(tm, 2 * LANES)),
        grid=(n // tm,),
        in_specs=[pl.BlockSpec((tm, MAIN_WIDTH), lambda i: (i, main_blk)),
                  pl.BlockSpec((tm, MEM_WIDTH), lambda i: (i, mem_blk)),
                  _const_spec((d, d), lambda i: (0, 0)),
                  pl.BlockSpec((tm, d), lambda i: (i, 0)),
                  _const_spec((1, d), lambda i: (0, 0)),
                  _const_spec((1, d), lambda i: (0, 0))],
        out_specs=[pl.BlockSpec((tm, d), lambda i: (i, 0)),
                   pl.BlockSpec((tm, d), lambda i: (i, 0))],
        out_shape=[jax.ShapeDtypeStruct((n, d), F32), jax.ShapeDtypeStruct((n, d), BF16)],
        compiler_params=_params(("arbitrary",), 48),
        name="proj_ln",
    )(a_main, a_mem, w_bf, x, g, b)


def _ffn_ln_kernel(x_ref, xb_ref, wg_ref, wu_ref, wd_ref, g_ref, b_ref, o32_ref, o16_ref, acc_ref):
    j = pl.program_id(1)

    @pl.when(j == 0)
    def _():
        acc_ref[...] = jnp.zeros_like(acc_ref)

    xb = xb_ref[...]
    h = (_silu(_dot(xb, wg_ref[...])) * _dot(xb, wu_ref[...])).astype(BF16)
    acc_ref[...] += _dot(h, wd_ref[...])

    @pl.when(j == pl.num_programs(1) - 1)
    def _():
        y = _layer_norm(ALPHA * x_ref[...] + acc_ref[...], g_ref[...], b_ref[...])
        o32_ref[...] = y
        o16_ref[...] = y.astype(BF16)


def _ffn_ln(x, xb, wg_bf, wu_bf, wd_bf, g, b, tm, tf):
    n, d = x.shape
    f = wg_bf.shape[1]
    return pl.pallas_call(
        _ffn_ln_kernel,
        grid=(n // tm, f // tf),
        in_specs=[pl.BlockSpec((tm, d), lambda i, j: (i, 0)),
                  pl.BlockSpec((tm, d), lambda i, j: (i, 0)),
                  pl.BlockSpec((d, tf), lambda i, j: (0, j)),
                  pl.BlockSpec((d, tf), lambda i, j: (0, j)),
                  pl.BlockSpec((tf, d), lambda i, j: (j, 0)),
                  _const_spec((1, d), lambda i, j: (0, 0)),
                  _const_spec((1, d), lambda i, j: (0, 0))],
        out_specs=[pl.BlockSpec((tm, d), lambda i, j: (i, 0)),
                   pl.BlockSpec((tm, d), lambda i, j: (i, 0))],
        out_shape=[jax.ShapeDtypeStruct((n, d), F32), jax.ShapeDtypeStruct((n, d), BF16)],
        scratch_shapes=[pltpu.VMEM((tm, d), F32)],
        compiler_params=_params(("arbitrary", "arbitrary"), 56),
        name="ffn_ln",
    )(x, xb, wg_bf, wu_bf, wd_bf, g, b)


def _matmul_kernel(x_ref, w_ref, s_ref, o_ref):
    o_ref[...] = (_dot(x_ref[...], w_ref[...]) * s_ref[...]).astype(o_ref.dtype)


def _matmul_colscale(xb, w_bf, col_scale, tm, tn):
    n, d = xb.shape
    cols = w_bf.shape[1]
    return pl.pallas_call(
        _matmul_kernel,
        grid=(n // tm, cols // tn),
        in_specs=[pl.BlockSpec((tm, d), lambda i, j: (i, 0)),
                  pl.BlockSpec((d, tn), lambda i, j: (0, j)),
                  pl.BlockSpec((1, tn), lambda i, j: (0, j))],
        out_specs=pl.BlockSpec((tm, tn), lambda i, j: (i, j)),
        out_shape=jax.ShapeDtypeStruct((n, cols), BF16),
        compiler_params=_params(("arbitrary", "arbitrary"), 48),
        name="in_proj_b",
    )(xb, w_bf, col_scale)


def _proj_t_kernel(wt_ref, bias_ref, x_ref, o_ref):
    o_ref[...] = (_dot_nt(wt_ref[...], x_ref[...]) + bias_ref[...]).astype(o_ref.dtype)


def _proj_transposed(xb, wt_bf, bias_col, tn):
    n, d = xb.shape
    cols = wt_bf.shape[0]
    return pl.pallas_call(
        _proj_t_kernel,
        grid=(n // tn,),
        in_specs=[_const_spec((cols, d), lambda i: (0, 0)),
                  _const_spec((cols, 1), lambda i: (0, 0)),
                  pl.BlockSpec((tn, d), lambda i: (i, 0))],
        out_specs=pl.BlockSpec((cols, tn), lambda i: (0, i)),
        out_shape=jax.ShapeDtypeStruct((cols, n), BF16),
        compiler_params=_params(("arbitrary",), 48),
        name="v_proj_t",
    )(wt_bf, bias_col, xb)


def _diff_attn_kernel(qi_tab, kj_tab, q1_ref, q2_ref, k1_ref, k2_ref, vt_ref,
                      lq1_ref, lk1_ref, lq2_ref, lk2_ref, sg_ref, *rest, lambda_init, qw, n_cast):
    cast_src = rest[:n_cast]
    o_ref = rest[n_cast]
    cast_dst = rest[n_cast + 1:2 * n_cast + 1]
    m1_ref, a1_ref, m2_ref, a2_ref = rest[2 * n_cast + 1:]
    tq, tk = q1_ref.shape[0], k1_ref.shape[0]
    p = pl.program_id(2)
    qi = qi_tab[p]
    kj = kj_tab[p]

    @pl.when(kj == 0)
    def _():
        for m_ref, a_ref in ((m1_ref, a1_ref), (m2_ref, a2_ref)):
            m_ref[...] = jnp.full_like(m_ref, NEG_BIG)
            a_ref[...] = jnp.zeros_like(a_ref)

    chains = [(q_ref, k_ref, m_ref, a_ref, j)
              for j in range(tq // qw)
              for q_ref, k_ref, m_ref, a_ref in ((q1_ref, k1_ref, m1_ref, a1_ref),
                                                 (q2_ref, k2_ref, m2_ref, a2_ref))]

    def step(key0):
        if key0 is None:
            live = chains
        else:
            live = [c for c in chains if key0 <= (c[4] + 1) * qw - 1]

        def scores(chain):
            q_ref, k_ref, _, _, j = chain
            s = _dot_nt(k_ref[...], q_ref[j * qw:(j + 1) * qw, :])
            if key0 is not None and key0 + tk - 1 > j * qw:
                key = key0 + lax.broadcasted_iota(jnp.int32, (tk, qw), 0)
                qry = j * qw + lax.broadcasted_iota(jnp.int32, (tk, qw), 1)
                s = jnp.where(key <= qry, s, NEG_BIG)
            return s.astype(BF16)

        def softmax(chain, s):
            m_ref, j = chain[2], chain[4]
            cols = slice(j * qw, (j + 1) * qw)
            m_old = m_ref[:, cols]
            m_new = jnp.maximum(m_old, jnp.max(s, axis=0, keepdims=True).astype(F32))
            alpha = jnp.exp(m_old - m_new)
            e = jnp.exp(s - m_new.astype(BF16))
            m_ref[:, cols] = m_new
            return e, alpha

        def weighted_values(chain, e, alpha):
            a_ref, j = chain[3], chain[4]
            cols = slice(j * qw, (j + 1) * qw)
            a_ref[:, cols] = alpha * a_ref[:, cols] + _dot(vt_ref[...], e)

        n = len(live)
        s_live, e_live = {}, {}
        for t in range(n + 2):
            if t < len(cast_src):
                cast_dst[t][...] = cast_src[t][...].astype(BF16)
            if t < n:
                s_live[t] = scores(live[t])
            if 0 <= t - 1 < n:
                e_live[t - 1] = softmax(live[t - 1], s_live.pop(t - 1))
            if 0 <= t - 2 < n:
                weighted_values(live[t - 2], *e_live.pop(t - 2))

    key0 = kj * tk - qi * tq
    pl.when(key0 < 0)(lambda: step(None))
    for static_key0 in range(0, tq, tk):
        pl.when(key0 == static_key0)(functools.partial(step, static_key0))

    @pl.when(kj == ((qi + 1) * tq - 1) // tk)
    def _():
        lam = (jnp.exp(jnp.sum(lq1_ref[...] * lk1_ref[...], axis=-1, keepdims=True))
               - jnp.exp(jnp.sum(lq2_ref[...] * lk2_ref[...], axis=-1, keepdims=True))
               + lambda_init)
        o1 = a1_ref[0:V_DIM, :] / a1_ref[V_DIM:V_DIM + 1, :]
        o2 = a2_ref[0:V_DIM, :] / a2_ref[V_DIM:V_DIM + 1, :]
        o = o1 - lam * o2
        o = o * lax.rsqrt(jnp.mean(o * o, axis=0, keepdims=True) + LN_EPS) * sg_ref[...]
        o_ref[...] = jnp.transpose(o * (1.0 - lambda_init)).astype(o_ref.dtype)


def _diff_attn(hk, vt, lq1, lk1, lq2, lk2, subln_g, to_cast, bsz, seq, lambda_init, tq, tk):
    n = hk.shape[0]
    assert tq % tk == 0 and tk >= 2
    nq = seq // tq
    pairs =[(qi, kj) for qi in range(nq) for kj in range(((qi + 1) * tq - 1) // tk + 1)]
    qi_tab = jnp.asarray(np.array([p[0] for p in pairs], np.int32))
    kj_tab = jnp.asarray(np.array([p[1] for p in pairs], np.int32))
    nqb, nkb = seq // tq, seq // tk
    n_pairs = len(pairs)
    n_steps = bsz * DIFF_HEADS * n_pairs
    k_blk0 = (MAIN_WIDTH + MEM_WIDTH) // QK_DIM
    qspec = lambda off: pl.BlockSpec((tq, QK_DIM), lambda b, h, p, qt, kt: (b * nqb + qt[p], off + h))
    kspec = lambda off: pl.BlockSpec((tk, QK_DIM), lambda b, h, p, qt, kt: (b * nkb + kt[p], off + h))
    vec = lambda w: pl.BlockSpec((1, w), lambda b, h, p, qt, kt: (0, 0))

    def cast_spec(arr):
        rows, cols = arr.shape
        n_blk = 1
        while n_blk * 2 <= n_steps and rows % (n_blk * 2) == 0 and (rows // (n_blk * 2)) % 16 == 0:
            n_blk *= 2
        return pl.BlockSpec((rows // n_blk, cols), lambda b, h, p, qt, kt: (
            jnp.minimum((b * DIFF_HEADS + h) * n_pairs + p, n_blk - 1), 0))

    cast_specs = [cast_spec(a) for a in to_cast]
    grid_spec = pltpu.PrefetchScalarGridSpec(
        num_scalar_prefetch=2,
        grid=(bsz, DIFF_HEADS, n_pairs),
        in_specs=[qspec(0), qspec(DIFF_HEADS), kspec(k_blk0), kspec(k_blk0 + DIFF_HEADS),
                  pl.BlockSpec((V_AUG, tk), lambda b, h, p, qt, kt: (h, b * nkb + kt[p])),
                  vec(QK_DIM), vec(QK_DIM), vec(QK_DIM), vec(QK_DIM),
                  pl.BlockSpec((V_DIM, 1), lambda b, h, p, qt, kt: (0, 0))] + cast_specs,
        out_specs=[pl.BlockSpec((tq, V_DIM), lambda b, h, p, qt, kt: (b * nqb + qt[p], h))] + cast_specs,
        scratch_shapes=[pltpu.VMEM((1, tq), F32), pltpu.VMEM((V_AUG, tq), F32),
                        pltpu.VMEM((1, tq), F32), pltpu.VMEM((V_AUG, tq), F32)],
    )
    return pl.pallas_call(
        functools.partial(_diff_attn_kernel, lambda_init=lambda_init, qw=min(tq, 2 * LANES),
                          n_cast=len(to_cast)),
        grid_spec=grid_spec,
        out_shape=[jax.ShapeDtypeStruct((n, MAIN_WIDTH), BF16)]
                  + [jax.ShapeDtypeStruct(a.shape, BF16) for a in to_cast],
        compiler_params=_params(("arbitrary", "arbitrary", "arbitrary"), 48),
        name="diff_attn",
    )(qi_tab, kj_tab, hk, hk, hk, hk, vt, lq1, lk1, lq2, lk2, subln_g.reshape(V_DIM, 1), *to_cast)


def _mem_attn_kernel(q_ref, kv_ref, o_ref):
    _mem_attention(q_ref[...].astype(F32), kv_ref, o_ref, 0)


def _mem_attn_b(hk, memkv, seq, tm):
    n = hk.shape[0]
    return pl.pallas_call(
        _mem_attn_kernel,
        grid=(n // tm,),
        in_specs=[pl.BlockSpec((tm, MEM_WIDTH), lambda i: (i, MAIN_WIDTH // MEM_WIDTH)),
                  pl.BlockSpec((None, None) + memkv.shape[2:], lambda i: (1, (i * tm) // seq, 0, 0))],
        out_specs=pl.BlockSpec((tm, MEM_WIDTH), lambda i: (i, 0)),
        out_shape=jax.ShapeDtypeStruct((n, MEM_WIDTH), BF16),
        compiler_params=_params(("arbitrary",), 32),
        name="mem_attn_b",
    )(hk, memkv)


def _lane_cumsum(v, lane):
    for shift in (1, 2, 4):
        v = v + jnp.where(lane >= shift, pltpu.roll(v, shift, 1), 0.0)
    return v


def _route_kernel(x_ref, wr_ref, pos_ref, gate_ref, tile_e_ref, cnt_ref, base_ref, start_ref, *, moe_tile):
    phase = pl.program_id(0)
    i = pl.program_id(1)
    tm = x_ref.shape[0]
    lane = lax.broadcasted_iota(jnp.int32, (tm, LANES), 1).astype(F32)

    x = x_ref[...]
    xh = x.astype(BF16)
    xl = (x - xh.astype(F32)).astype(BF16)
    w = wr_ref[...]
    wh = w.astype(BF16)
    wl = (w - wh.astype(F32)).astype(BF16)
    logits = _dot(xh, wh) + (_dot(xh, wl) + _dot(xl, wh))
    logits = jnp.where(lane < N_EXPERTS, logits, NEG_BIG)

    v1 = jnp.max(logits, axis=-1, keepdims=True)
    i1 = jnp.min(jnp.where(logits == v1, lane, float(LANES)), axis=-1, keepdims=True)
    rest = jnp.where(lane == i1, NEG_BIG, logits)
    v2 = jnp.max(rest, axis=-1, keepdims=True)
    i2 = jnp.min(jnp.where(rest == v2, lane, float(LANES)), axis=-1, keepdims=True)
    sel = jnp.logical_or(lane == i1, lane == i2)
    tile_cnt = jnp.sum(sel.astype(F32), axis=0, keepdims=True)

    @pl.when(jnp.logical_and(phase == 0, i == 0))
    def _():
        cnt_ref[...] = jnp.zeros_like(cnt_ref)

    @pl.when(phase == 0)
    def _():
        cnt_ref[...] += tile_cnt

    @pl.when(jnp.logical_and(phase == 1, i == 0))
    def _():
        lane8 = lax.broadcasted_iota(jnp.int32, (8, LANES), 1).astype(F32)
        sub8 = lax.broadcasted_iota(jnp.int32, (8, LANES), 0).astype(F32)
        cnt = jnp.broadcast_to(cnt_ref[...], (8, LANES))
        padded = jnp.ceil(cnt * (1.0 / moe_tile)) * moe_tile
        ends = _lane_cumsum(padded, lane8)
        start_ref[...] = (ends - padded)[0:1, :]
        base_ref[...] = jnp.zeros_like(base_ref)
        tile_row0 = (sub8 * LANES + lane8) * moe_tile
        tile_e = jnp.zeros((8, LANES), F32)
        for e in range(N_EXPERTS):
            end_e = jnp.sum(jnp.where(lane8 == e, ends, 0.0), axis=-1, keepdims=True)
            tile_e = tile_e + (tile_row0 >= end_e).astype(F32)
        tile_e_ref[0:8, :] = jnp.minimum(tile_e, N_EXPERTS - 1.0).astype(jnp.int32)
        total = jnp.sum(jnp.where(lane8 == N_EXPERTS - 1, ends, 0.0), axis=-1, keepdims=True)
        tile_e_ref[8:16, :] = jnp.broadcast_to(total * (1.0 / moe_tile), (8, LANES)).astype(jnp.int32)
        tile_e_ref[16:24, :] = (ends - padded + cnt).astype(jnp.int32)

    @pl.when(phase == 1)
    def _():
        r = lax.broadcasted_iota(jnp.int32, (tm, tm), 0)
        c = lax.broadcasted_iota(jnp.int32, (tm, tm), 1)
        before = (c < r).astype(BF16)
        rank = _dot(before, sel.astype(BF16))
        slot = start_ref[...] + base_ref[...] + rank
        p1 = jnp.sum(jnp.where(lane == i1, slot, 0.0), axis=-1, keepdims=True)
        p2 = jnp.sum(jnp.where(lane == i2, slot, 0.0), axis=-1, keepdims=True)
        pos_ref[...] = jnp.where(lane == 0, p1, jnp.where(lane == 1, p2, 0.0)).astype(jnp.int32)
        g1 = 1.0 / (1.0 + jnp.exp(v2 - v1))
        g2 = jnp.exp(v2 - v1) * g1
        gate_ref[...] = jnp.where(lane == 0, g1, jnp.where(lane == 1, g2, 0.0))
        base_ref[...] += tile_cnt


def _route(x, w_router_pad, tm, moe_tile):
    n, d = x.shape
    return pl.pallas_call(
        functools.partial(_route_kernel, moe_tile=moe_tile),
        grid=(2, n // tm),
        in_specs=[pl.BlockSpec((tm, d), lambda ph, i: (i, 0)),
                  _const_spec((d, LANES), lambda ph, i: (0, 0))],
        out_specs=[pl.BlockSpec((tm, LANES), lambda ph, i: (i * ph, 0)),
                   pl.BlockSpec((tm, LANES), lambda ph, i: (i * ph, 0)),
                   pl.BlockSpec((24, LANES), lambda ph, i: (0, 0))],
        out_shape=[jax.ShapeDtypeStruct((n, LANES), jnp.int32),
                   jax.ShapeDtypeStruct((n, LANES), F32),
                   jax.ShapeDtypeStruct((24, LANES), jnp.int32)],
        scratch_shapes=[pltpu.VMEM((1, LANES), F32), pltpu.VMEM((1, LANES), F32), pltpu.VMEM((1, LANES), F32)],
        compiler_params=_params(("arbitrary", "arbitrary"), 32),
        name="route",
    )(x, w_router_pad)


def _dispatch_kernel(pad_row0, pos_ref, x_ref, xs_ref, zero_ref, sem, *, moe_tile):
    tm = x_ref.shape[0]

    @pl.when(pl.program_id(0) == 0)
    def _():
        zero_ref[...] = jnp.zeros_like(zero_ref)
        zr = zero_ref.shape[0]

        def drain_fill(r, carry):
            pltpu.make_async_copy(zero_ref.at[pl.ds(0, 1)], xs_ref.at[pl.ds(0, 1)], sem).wait()
            return carry

        for e in range(N_EXPERTS):
            def fill(r, carry, e=e):
                pltpu.make_async_copy(zero_ref.at[pl.ds(0, 1)],
                                      xs_ref.at[pl.ds(pad_row0[e] + r, 1)], sem).start()
                return carry

            lax.fori_loop(0, moe_tile, fill, 0, unroll=DMA_LOOP_UNROLL)
            lax.fori_loop(0, moe_tile, drain_fill, 0, unroll=2 * DMA_LOOP_UNROLL)

        n_alloc = xs_ref.shape[0]
        tail = [pltpu.make_async_copy(
                    zero_ref,
                    xs_ref.at[pl.ds(pl.multiple_of(
                        jnp.minimum(pad_row0[N_EXPERTS] + c * zr, n_alloc - zr), zr), zr)], sem)
                for c in range((N_EXPERTS + 1) * moe_tile // zr)]
        for f in tail:
            f.start()
            f.wait()

    def issue(t, carry):
        for k in range(2):
            dst = pos_ref[0, 0, 2 * t + k]
            pltpu.make_async_copy(x_ref.at[pl.ds(t, 1)], xs_ref.at[pl.ds(dst, 1)], sem).start(priority=k)
        return carry

    lax.fori_loop(0, tm, issue, 0, unroll=DMA_LOOP_UNROLL)

    def drain(t, carry):
        pltpu.make_async_copy(x_ref.at[pl.ds(0, 1)], xs_ref.at[pl.ds(0, 1)], sem).wait()
        return carry

    lax.fori_loop(0, 2 * tm, drain, 0, unroll=2 * DMA_LOOP_UNROLL)


def _dispatch(pad_row0, pos_flat, x, n_rows, moe_tile, tm):
    n, d = x.shape
    zero_rows = min(moe_tile, LANES)
    grid_spec = pltpu.PrefetchScalarGridSpec(
        num_scalar_prefetch=1,
        grid=(n // tm,),
        in_specs=[pl.BlockSpec((1, 1, 2 * tm), lambda i, pr: (i, 0, 0), memory_space=pltpu.SMEM),
                  pl.BlockSpec((tm, d), lambda i, pr: (i, 0))],
        out_specs=pl.BlockSpec(memory_space=pl.ANY),
        scratch_shapes=[pltpu.VMEM((zero_rows, d), F32), pltpu.SemaphoreType.DMA(())],
    )
    return pl.pallas_call(
        functools.partial(_dispatch_kernel, moe_tile=moe_tile),
        grid_spec=grid_spec,
        out_shape=jax.ShapeDtypeStruct((n_rows + moe_tile, d), F32),
        compiler_params=_params(("arbitrary",), 32),
        name="moe_dispatch",
    )(pad_row0, pos_flat, x)


def _experts_kernel(tile_e, n_valid, xs_ref, wg_ref, wu_ref, wd_ref, o_ref, xb_ref):
    del tile_e
    i = pl.program_id(0)
    j = pl.program_id(1)
    valid = i < n_valid[0]

    @pl.when(jnp.logical_and(valid, j == 0))
    def _():
        xb_ref[...] = xs_ref[...].astype(BF16)

    @pl.when(j == 0)
    def _():
        o_ref[...] = jnp.zeros_like(o_ref)

    @pl.when(valid)
    def _():
        xb = xb_ref[...]
        h = (_silu(_dot(xb, wg_ref[...])) * _dot(xb, wu_ref[...])).astype(BF16)
        o_ref[...] += _dot(h, wd_ref[...])


def _experts(tile_e, n_valid, xs, wg_bf, wu_bf, wd_bf, tm, tf):
    d = xs.shape[1]
    m = tile_e.shape[0] * tm
    f = wg_bf.shape[2]
    nj = f // tf

    def row_idx(i, j, te, nv):
        return (jnp.minimum(i, nv[0] - 1), 0)

    def col_of(i, j, nv):
        return jnp.where(i < nv[0], j, nj - 1)

    grid_spec = pltpu.PrefetchScalarGridSpec(
        num_scalar_prefetch=2,
        grid=(m // tm, nj),
        in_specs=[pl.BlockSpec((tm, d), row_idx),
                  pl.BlockSpec((None, d, tf), lambda i, j, te, nv: (te[i], 0, col_of(i, j, nv))),
                  pl.BlockSpec((None, d, tf), lambda i, j, te, nv: (te[i], 0, col_of(i, j, nv))),
                  pl.BlockSpec((None, tf, d), lambda i, j, te, nv: (te[i], col_of(i, j, nv), 0))],
        out_specs=pl.BlockSpec((tm, d), lambda i, j, te, nv: (i, 0)),
        scratch_shapes=[pltpu.VMEM((tm, d), BF16)],
    )
    return pl.pallas_call(
        _experts_kernel,
        grid_spec=grid_spec,
        out_shape=jax.ShapeDtypeStruct((m, d), F32),
        compiler_params=_params(("arbitrary", "arbitrary"), 56),
        name="moe_experts",
    )(tile_e, n_valid, xs, wg_bf, wu_bf, wd_bf)


def _combine_kernel(pos_ref, gate_ref, x_ref, g_ref, b_ref, ys_ref, o_ref, buf_ref, sem):
    tm = x_ref.shape[0]

    def issue(t, carry):
        for k in range(2):
            src = pos_ref[0, 0, 2 * t + k]
            pltpu.make_async_copy(ys_ref.at[pl.ds(src, 1)], buf_ref.at[k, pl.ds(t, 1)], sem).start(priority=k)
        return carry

    lax.fori_loop(0, tm, issue, 0, unroll=DMA_LOOP_UNROLL)

    def drain(t, carry):
        pltpu.make_async_copy(ys_ref.at[pl.ds(0, 1)], buf_ref.at[0, pl.ds(0, 1)], sem).wait()
        return carry

    lax.fori_loop(0, 2 * tm, drain, 0, unroll=2 * DMA_LOOP_UNROLL)

    gates = gate_ref[...]
    y = gates[:, 0:1] * buf_ref[0] + gates[:, 1:2] * buf_ref[1]
    o_ref[...] = _layer_norm(ALPHA * x_ref[...] + y, g_ref[...], b_ref[...])


def _combine(pos_flat, gates, x, g, b, ys, tm):
    n, d = x.shape
    return pl.pallas_call(
        _combine_kernel,
        grid=(n // tm,),
        in_specs=[pl.BlockSpec((1, 1, 2 * tm), lambda i: (i, 0, 0), memory_space=pltpu.SMEM),
                  pl.BlockSpec((tm, LANES), lambda i: (i, 0)),
                  pl.BlockSpec((tm, d), lambda i: (i, 0)),
                  _const_spec((1, d), lambda i: (0, 0)),
                  _const_spec((1, d), lambda i: (0, 0)),
                  pl.BlockSpec(memory_space=pl.ANY)],
        out_specs=pl.BlockSpec((tm, d), lambda i: (i, 0)),
        out_shape=jax.ShapeDtypeStruct((n, d), F32),
        scratch_shapes=[pltpu.VMEM((2, tm, d), F32), pltpu.SemaphoreType.DMA(())],
        compiler_params=_params(("arbitrary",), 32),
        name="moe_combine",
    )(pos_flat, gates, x, g, b, ys)


def kernel(x, mem, ln_g, ln_b, w_mix_out, w_mem_kv, a_w_in, a_vnorm_g, a_vnorm_b, a_w_s, a_b_s,
           shared_w_kv, b_w_in, b_lambda_q1, b_lambda_k1, b_lambda_q2, b_lambda_k2, b_subln_g,
           ffn_w_gate, ffn_w_up, ffn_w_down, moe_w_router, moe_w_gate, moe_w_up, moe_w_down):
    bsz, seq, d = x.shape
    n = bsz * seq
    t = _tiles(n, seq)
    xf = x.reshape(n, d)
    row = lambda v: v.reshape(1, -1)

    memkv = _memkv(mem, w_mem_kv.astype(BF16))

    mixed = _mixer_a(xf, a_w_in[0].astype(BF16), row(a_vnorm_g[0]), row(a_vnorm_b[0]),
                     a_w_s[0], jnp.transpose(a_b_s[0]), memkv, seq, t["mixer"])
    x1, x1b = _proj_ln(mixed, 0, mixed, MAIN_WIDTH // MEM_WIDTH, w_mix_out[0].astype(BF16), xf,
                       row(ln_g[0, 0]), row(ln_b[0, 0]), t["proj"])
    x2, x2b = _ffn_ln(x1, x1b, ffn_w_gate[0].astype(BF16), ffn_w_up[0].astype(BF16),
                      ffn_w_down[0].astype(BF16), row(ln_g[0, 1]), row(ln_b[0, 1]), t["ffn"], t["ffn_f"])

    w_cat = jnp.concatenate([b_w_in[0], shared_w_kv[:, :MAIN_WIDTH]], axis=1).astype(BF16)
    col_scale = jnp.concatenate([jnp.full((MAIN_WIDTH,), QK_DIM ** -0.5, F32),
                                 jnp.ones((w_cat.shape[1] - MAIN_WIDTH,), F32)]).reshape(1, -1)
    hk = _matmul_colscale(x2b, w_cat, col_scale, t["mm"], t["mm_n"])
    wv_t = jnp.transpose(shared_w_kv[:, MAIN_WIDTH:]).reshape(DIFF_HEADS, V_DIM, d)
    wv_t = jnp.pad(wv_t, ((0, 0), (0, ONES_ROWS), (0, 0))).reshape(DIFF_HEADS * V_AUG, d).astype(BF16)
    ones_bias = jnp.pad(jnp.zeros((DIFF_HEADS, V_DIM, 1), F32), ((0, 0), (0, ONES_ROWS), (0, 0)),
                        constant_values=1.0).reshape(DIFF_HEADS * V_AUG, 1)
    vt = _proj_transposed(x2b, wv_t, ones_bias, t["mm_t"])
    lambda_init = 0.8 - 0.6 * math.exp(-0.3 * 1)
    n_exp, _, d_ff = moe_w_gate[0].shape
    expert_w = [moe_w_gate[0].reshape(n_exp * d, d_ff), moe_w_up[0].reshape(n_exp * d, d_ff),
                moe_w_down[0].reshape(n_exp * d_ff, d)]
    main, wg_bf, wu_bf, wd_bf = _diff_attn(
        hk, vt, row(b_lambda_q1[0]), row(b_lambda_k1[0]), row(b_lambda_q2[0]), row(b_lambda_k2[0]),
        row(b_subln_g[0]), expert_w, bsz, seq, lambda_init, t["attn_q"], t["attn_k"])
    mem_out = _mem_attn_b(hk, memkv, seq, t["mem"])
    x3, _ = _proj_ln(main, 0, mem_out, 0, w_mix_out[1].astype(BF16), x2,
                     row(ln_g[1, 0]), row(ln_b[1, 0]), t["proj"])

    moe_tile = t["moe"]
    n_row_tiles = (2 * n) // moe_tile + N_EXPERTS
    w_router_pad = jnp.pad(moe_w_router[0], ((0, 0), (0, LANES - N_EXPERTS)))
    pos, gates, tile_info = _route(x3, w_router_pad, t["route"], moe_tile)
    tile_e = tile_info[0:8].reshape(-1)[:n_row_tiles]
    n_valid = tile_info[8, 0:1]
    tc = t["comb"]
    pos_flat = pos[:, :2].reshape(n // tc, 1, 2 * tc)
    fill_rows = jnp.concatenate([tile_info[16, :N_EXPERTS], n_valid * moe_tile])
    xs = _dispatch(fill_rows, pos_flat, x3, n_row_tiles * moe_tile, moe_tile, tc)
    ys = _experts(tile_e, n_valid, xs, wg_bf.reshape(n_exp, d, d_ff), wu_bf.reshape(n_exp, d, d_ff),
                  wd_bf.reshape(n_exp, d_ff, d), moe_tile, t["moe_f"])
    x4 = _combine(pos_flat, gates, x3, row(ln_g[1, 1]), row(ln_b[1, 1]), ys, tc)
    return x4.reshape(bsz, seq, d)
```

```python
import functools
import math

import numpy as np
import jax
import jax.numpy as jnp
from jax import lax
from jax.experimental import pallas as pl
from jax.experimental.pallas import tpu as pltpu

BF16 = jnp.bfloat16
F32 = jnp.float32

D_MODEL = 2048
MEM_WIDTH = 512
MAIN_WIDTH = 1536
MEM_HEADS = 4
MEM_HEAD_DIM = 128
CHUNK = 128
SG_GROUPS = 12
QK_DIM = 128
V_DIM = 256
DIFF_HEADS = 6
N_EXPERTS = 8
DEPTH = 2
ALPHA = (2.0 * DEPTH) ** 0.25
LN_EPS = 1e-5
LANES = 128
SUBLANES = 8
ONES_ROWS = 16
V_AUG = V_DIM + ONES_ROWS
NEG_BIG = -1e30
DMA_LOOP_UNROLL = 8
MIB = 1024 * 1024


def _tiles(n_tokens, seq):
    def fit(pref, total):
        t = min(pref, total)
        while total % t:
            t //= 2
        return t
    return dict(
        mixer=fit(512, seq),
        proj=fit(512, n_tokens),
        ffn=fit(512, n_tokens),
        ffn_f=512,
        mm=fit(1024, n_tokens),
        mm_n=1792,
        mm_t=fit(512, n_tokens),
        attn_q=fit(1024, seq),
        attn_k=fit(1024, seq),
        mem=fit(512, seq),
        route=fit(512, n_tokens),
        moe=fit(512, n_tokens),
        moe_f=1024,
        comb=fit(512, n_tokens),
    )


def _params(sem, vmem_mib):
    return pltpu.CompilerParams(dimension_semantics=sem, vmem_limit_bytes=vmem_mib * MIB)


def _const_spec(shape, index_map):
    return pl.BlockSpec(shape, index_map, pipeline_mode=pl.Buffered(1))


def _dot(a, b):
    return jnp.dot(a, b, preferred_element_type=F32)


def _dot_nt(a, b):
    return lax.dot_general(a, b, (((1,), (1,)), ((), ())), preferred_element_type=F32)


def _layer_norm(v, g, b):
    mu = jnp.mean(v, axis=-1, keepdims=True)
    c = v - mu
    var = jnp.mean(c * c, axis=-1, keepdims=True)
    return c * lax.rsqrt(var + LN_EPS) * g + b


def _gelu_tanh(v):
    return 0.5 * v * (1.0 + jnp.tanh(math.sqrt(2.0 / math.pi) * (v + 0.044715 * (v * v * v))))


def _silu(v):
    return v / (1.0 + jnp.exp(-v))


def _mem_attention(q, kv_ref, o_ref, col0):
    scale = MEM_HEAD_DIM ** -0.5
    for h in range(MEM_HEADS):
        lo = h * MEM_HEAD_DIM
        qh = q[:, lo:lo + MEM_HEAD_DIM].astype(BF16)
        kh = kv_ref[:, lo:lo + MEM_HEAD_DIM]
        vh = kv_ref[:, MEM_WIDTH + lo:MEM_WIDTH + lo + MEM_HEAD_DIM]
        s = _dot_nt(qh, kh) * scale
        e = jnp.exp(s - jnp.max(s, axis=-1, keepdims=True))
        o = _dot(e.astype(BF16), vh) / jnp.sum(e, axis=-1, keepdims=True)
        o_ref[:, col0 + lo:col0 + lo + MEM_HEAD_DIM] = o.astype(o_ref.dtype)


def _memkv_kernel(mem_ref, w_ref, o_ref):
    o_ref[...] = _dot(mem_ref[...].astype(BF16), w_ref[...]).astype(o_ref.dtype)


def _memkv(mem, w_mem_kv_bf):
    bsz, n_mem, d = mem.shape
    depth, _, cols = w_mem_kv_bf.shape
    return pl.pallas_call(
        _memkv_kernel,
        grid=(depth, bsz),
        in_specs=[pl.BlockSpec((None, n_mem, d), lambda l, b: (b, 0, 0)),
                  pl.BlockSpec((None, d, cols), lambda l, b: (l, 0, 0))],
        out_specs=pl.BlockSpec((None, None, n_mem, cols), lambda l, b: (l, b, 0, 0)),
        out_shape=jax.ShapeDtypeStruct((depth, bsz, n_mem, cols), BF16),
        compiler_params=_params(("arbitrary", "arbitrary"), 32),
        name="memkv",
    )(mem, w_mem_kv_bf)


def _mixer_a_kernel(x_ref, w_ref, vg_ref, vb_ref, ws_ref, bs_ref, kv_ref, o_ref):
    tm = x_ref.shape[0]
    xb = x_ref[...].astype(BF16)
    v = _gelu_tanh(_dot(xb, w_ref[:, MAIN_WIDTH:2 * MAIN_WIDTH]))
    vn = _layer_norm(v, vg_ref[...], vb_ref[...]).astype(BF16)
    u = _gelu_tanh(_dot(xb, w_ref[:, :MAIN_WIDTH]))
    row = lax.broadcasted_iota(jnp.int32, (CHUNK, CHUNK), 0)
    col = lax.broadcasted_iota(jnp.int32, (CHUNK, CHUNK), 1)
    causal = col <= row
    for g in range(SG_GROUPS):
        wg = jnp.where(causal, ws_ref[g], 0.0).astype(BF16)
        bias = bs_ref[:, g:g + 1]
        for c in range(tm // CHUNK):
            rows = slice(c * CHUNK, (c + 1) * CHUNK)
            cols = slice(g * CHUNK, (g + 1) * CHUNK)
            s = _dot(wg, vn[rows, cols]) + bias
            o_ref[rows, cols] = (u[rows, cols] * s).astype(o_ref.dtype)
    q_mem = _dot(xb, w_ref[:, 2 * MAIN_WIDTH:])
    _mem_attention(q_mem, kv_ref, o_ref, MAIN_WIDTH)


def _mixer_a(x, w_in_bf, vnorm_g, vnorm_b, w_s, b_s_t, memkv, seq, tm):
    n, d = x.shape
    in_cols = w_in_bf.shape[1]
    return pl.pallas_call(
        _mixer_a_kernel,
        grid=(n // tm,),
        in_specs=[pl.BlockSpec((tm, d), lambda i: (i, 0)),
                  _const_spec((d, in_cols), lambda i: (0, 0)),
                  _const_spec((1, MAIN_WIDTH), lambda i: (0, 0)),
                  _const_spec((1, MAIN_WIDTH), lambda i: (0, 0)),
                  _const_spec((SG_GROUPS, CHUNK, CHUNK), lambda i: (0, 0, 0)),
                  _const_spec((CHUNK, SG_GROUPS), lambda i: (0, 0)),
                  pl.BlockSpec((None, None) + memkv.shape[2:], lambda i: (0, (i * tm) // seq, 0, 0))],
        out_specs=pl.BlockSpec((tm, d), lambda i: (i, 0)),
        out_shape=jax.ShapeDtypeStruct((n, d), BF16),
        compiler_params=_params(("arbitrary",), 56),
        name="mixer_a",
    )(x, w_in_bf, vnorm_g, vnorm_b, w_s, b_s_t, memkv)


def _proj_ln_kernel(a1_ref, a2_ref, w_ref, x_ref, g_ref, b_ref, o32_ref, o16_ref, *, sub):
    for r in range(0, x_ref.shape[0], sub):
        rows = slice(r, r + sub)
        mix = _dot(a1_ref[rows, :], w_ref[:MAIN_WIDTH, :]) + _dot(a2_ref[rows, :], w_ref[MAIN_WIDTH:, :])
        y = _layer_norm(ALPHA * x_ref[rows, :] + mix, g_ref[...], b_ref[...])
        o32_ref[rows, :] = y
        o16_ref[rows, :] = y.astype(BF16)


def _proj_ln(a_main, main_blk, a_mem, mem_blk, w_bf, x, g, b, tm):
    n, d = x.shape
    return pl.pallas_call(
        functools.partial(_proj_ln_kernel, sub=min(tm, 2 * LANES)),
        grid=(n // tm,),
        in_specs=[pl.BlockSpec((tm, MAIN_WIDTH), lambda i: (i, main_blk)),
                  pl.BlockSpec((tm, MEM_WIDTH), lambda i: (i, mem_blk)),
                  _const_spec((d, d), lambda i: (0, 0)),
                  pl.BlockSpec((tm, d), lambda i: (i, 0)),
                  _const_spec((1, d), lambda i: (0, 0)),
                  _const_spec((1, d), lambda i: (0, 0))],
        out_specs=[pl.BlockSpec((tm, d), lambda i: (i, 0)),
                   pl.BlockSpec((tm, d), lambda i: (i, 0))],
        out_shape=[jax.ShapeDtypeStruct((n, d), F32), jax.ShapeDtypeStruct((n, d), BF16)],
        compiler_params=_params(("arbitrary",), 48),
        name="proj_ln",
    )(a_main, a_mem, w_bf, x, g, b)


def _ffn_ln_kernel(x_ref, xb_ref, wg_ref, wu_ref, wd_ref, g_ref, b_ref, o32_ref, o16_ref, acc_ref):
    j = pl.program_id(1)

    @pl.when(j == 0)
    def _():
        acc_ref[...] = jnp.zeros_like(acc_ref)

    xb = xb_ref[...]
    h = (_silu(_dot(xb, wg_ref[...])) * _dot(xb, wu_ref[...])).astype(BF16)
    acc_ref[...] += _dot(h, wd_ref[...])

    @pl.when(j == pl.num_programs(1) - 1)
    def _():
        y = _layer_norm(ALPHA * x_ref[...] + acc_ref[...], g_ref[...], b_ref[...])
        o32_ref[...] = y
        o16_ref[...] = y.astype(BF16)


def _ffn_ln(x, xb, wg_bf, wu_bf, wd_bf, g, b, tm, tf):
    n, d = x.shape
    f = wg_bf.shape[1]
    return pl.pallas_call(
        _ffn_ln_kernel,
        grid=(n // tm, f // tf),
        in_specs=[pl.BlockSpec((tm, d), lambda i, j: (i, 0)),
                  pl.BlockSpec((tm, d), lambda i, j: (i, 0)),
                  pl.BlockSpec((d, tf), lambda i, j: (0, j)),
                  pl.BlockSpec((d, tf), lambda i, j: (0, j)),
                  pl.BlockSpec((tf, d), lambda i, j: (j, 0)),
                  _const_spec((1, d), lambda i, j: (0, 0)),
                  _const_spec((1, d), lambda i, j: (0, 0))],
        out_specs=[pl.BlockSpec((tm, d), lambda i, j: (i, 0)),
                   pl.BlockSpec((tm, d), lambda i, j: (i, 0))],
        out_shape=[jax.ShapeDtypeStruct((n, d), F32), jax.ShapeDtypeStruct((n, d), BF16)],
        scratch_shapes=[pltpu.VMEM((tm, d), F32)],
        compiler_params=_params(("arbitrary", "arbitrary"), 56),
        name="ffn_ln",
    )(x, xb, wg_bf, wu_bf, wd_bf, g, b)


def _matmul_kernel(x_ref, w_ref, s_ref, o_ref):
    o_ref[...] = (_dot(x_ref[...], w_ref[...]) * s_ref[...]).astype(o_ref.dtype)


def _matmul_colscale(xb, w_bf, col_scale, tm, tn):
    n, d = xb.shape
    cols = w_bf.shape[1]
    return pl.pallas_call(
        _matmul_kernel,
        grid=(n // tm, cols // tn),
        in_specs=[pl.BlockSpec((tm, d), lambda i, j: (i, 0)),
                  pl.BlockSpec((d, tn), lambda i, j: (0, j)),
                  pl.BlockSpec((1, tn), lambda i, j: (0, j))],
        out_specs=pl.BlockSpec((tm, tn), lambda i, j: (i, j)),
        out_shape=jax.ShapeDtypeStruct((n, cols), BF16),
        compiler_params=_params(("arbitrary", "arbitrary"), 48),
        name="in_proj_b",
    )(xb, w_bf, col_scale)


def _proj_t_kernel(wt_ref, bias_ref, x_ref, o_ref):
    o_ref[...] = (_dot_nt(wt_ref[...], x_ref[...]) + bias_ref[...]).astype(o_ref.dtype)


def _proj_transposed(xb, wt_bf, bias_col, tn):
    n, d = xb.shape
    cols = wt_bf.shape[0]
    return pl.pallas_call(
        _proj_t_kernel,
        grid=(n // tn,),
        in_specs=[_const_spec((cols, d), lambda i: (0, 0)),
                  _const_spec((cols, 1), lambda i: (0, 0)),
                  pl.BlockSpec((tn, d), lambda i: (i, 0))],
        out_specs=pl.BlockSpec((cols, tn), lambda i: (0, i)),
        out_shape=jax.ShapeDtypeStruct((cols, n), BF16),
        compiler_params=_params(("arbitrary",), 48),
        name="v_proj_t",
    )(wt_bf, bias_col, xb)


def _diff_attn_kernel(qi_tab, kj_tab, q1_ref, q2_ref, k1_ref, k2_ref, vt_ref,
                      lq1_ref, lk1_ref, lq2_ref, lk2_ref, sg_ref, *rest, lambda_init, qw, n_cast):
    cast_src = rest[:n_cast]
    o_ref = rest[n_cast]
    cast_dst = rest[n_cast + 1:2 * n_cast + 1]
    m1_ref, a1_ref, m2_ref, a2_ref = rest[2 * n_cast + 1:]
    tq, tk = q1_ref.shape[0], k1_ref.shape[0]
    p = pl.program_id(2)
    qi = qi_tab[p]
    kj = kj_tab[p]

    @pl.when(kj == 0)
    def _():
        for m_ref, a_ref in ((m1_ref, a1_ref), (m2_ref, a2_ref)):
            m_ref[...] = jnp.full_like(m_ref, NEG_BIG)
            a_ref[...] = jnp.zeros_like(a_ref)

    chains = [(q_ref, k_ref, m_ref, a_ref, j)
              for j in range(tq // qw)
              for q_ref, k_ref, m_ref, a_ref in ((q1_ref, k1_ref, m1_ref, a1_ref),
                                                 (q2_ref, k2_ref, m2_ref, a2_ref))]

    def step(key0):
        if key0 is None:
            live = chains
        else:
            live = [c for c in chains if key0 <= (c[4] + 1) * qw - 1]

        def scores(chain):
            q_ref, k_ref, j = chain[0], chain[1], chain[4]
            s = _dot_nt(k_ref[...], q_ref[j * qw:(j + 1) * qw, :])
            if key0 is not None and key0 + tk - 1 > j * qw:
                key = key0 + lax.broadcasted_iota(jnp.int32, (tk, qw), 0)
                qry = j * qw + lax.broadcasted_iota(jnp.int32, (tk, qw), 1)
                s = jnp.where(key <= qry, s, NEG_BIG)
            return s.astype(BF16)

        def softmax(chain, s):
            m_ref, j = chain[2], chain[4]
            cols = slice(j * qw, (j + 1) * qw)
            m_old = m_ref[:, cols]
            m_new = jnp.maximum(m_old, jnp.max(s, axis=0, keepdims=True).astype(F32))
            alpha = jnp.exp(m_old - m_new)
            e = jnp.exp(s - m_new.astype(BF16))
            m_ref[:, cols] = m_new
            return e, alpha

        def weighted_values(chain, e, alpha):
            a_ref, j = chain[3], chain[4]
            cols = slice(j * qw, (j + 1) * qw)
            a_ref[:, cols] = alpha * a_ref[:, cols] + _dot(vt_ref[...], e)

        n = len(live)
        s_live, e_live = {}, {}
        for t in range(n + 2):
            if t < len(cast_src):
                cast_dst[t][...] = cast_src[t][...].astype(BF16)
            if t < n:
                s_live[t] = scores(live[t])
            if 0 <= t - 1 < n:
                e_live[t - 1] = softmax(live[t - 1], s_live.pop(t - 1))
            if 0 <= t - 2 < n:
                weighted_values(live[t - 2], *e_live.pop(t - 2))

    key0 = kj * tk - qi * tq
    pl.when(key0 < 0)(lambda: step(None))
    for static_key0 in range(0, tq, tk):
        pl.when(key0 == static_key0)(functools.partial(step, static_key0))

    @pl.when(kj == ((qi + 1) * tq - 1) // tk)
    def _():
        lam = (jnp.exp(jnp.sum(lq1_ref[...] * lk1_ref[...], axis=-1, keepdims=True))
               - jnp.exp(jnp.sum(lq2_ref[...] * lk2_ref[...], axis=-1, keepdims=True))
               + lambda_init)
        o1 = a1_ref[0:V_DIM, :] / a1_ref[V_DIM:V_DIM + 1, :]
        o2 = a2_ref[0:V_DIM, :] / a2_ref[V_DIM:V_DIM + 1, :]
        o = o1 - lam * o2
        o = o * lax.rsqrt(jnp.mean(o * o, axis=0, keepdims=True) + LN_EPS) * sg_ref[...]
        o_ref[...] = jnp.transpose(o * (1.0 - lambda_init)).astype(o_ref.dtype)


def _diff_attn(hk, vt, lq1, lk1, lq2, lk2, subln_g, to_cast, bsz, seq, lambda_init, tq, tk):
    n = hk.shape[0]
    assert tq % tk == 0 and tk >= 2
    nq = seq // tq
    pairs =[(qi, kj) for qi in range(nq) for kj in range(((qi + 1) * tq - 1) // tk + 1)]
    qi_tab = jnp.asarray(np.array([p[0] for p in pairs], np.int32))
    kj_tab = jnp.asarray(np.array([p[1] for p in pairs], np.int32))
    nqb, nkb = seq // tq, seq // tk
    n_pairs = len(pairs)
    n_steps = bsz * DIFF_HEADS * n_pairs
    k_blk0 = (MAIN_WIDTH + MEM_WIDTH) // QK_DIM
    qspec = lambda off: pl.BlockSpec((tq, QK_DIM), lambda b, h, p, qt, kt: (b * nqb + qt[p], off + h))
    kspec = lambda off: pl.BlockSpec((tk, QK_DIM), lambda b, h, p, qt, kt: (b * nkb + kt[p], off + h))
    vec = lambda w: pl.BlockSpec((1, w), lambda b, h, p, qt, kt: (0, 0))

    def cast_spec(arr):
        rows, cols = arr.shape
        n_blk = 1
        while n_blk * 2 <= n_steps and rows % (n_blk * 2) == 0 and (rows // (n_blk * 2)) % 16 == 0:
            n_blk *= 2
        return pl.BlockSpec((rows // n_blk, cols), lambda b, h, p, qt, kt: (
            jnp.minimum((b * DIFF_HEADS + h) * n_pairs + p, n_blk - 1), 0))

    cast_specs = [cast_spec(a) for a in to_cast]
    grid_spec = pltpu.PrefetchScalarGridSpec(
        num_scalar_prefetch=2,
        grid=(bsz, DIFF_HEADS, n_pairs),
        in_specs=[qspec(0), qspec(DIFF_HEADS), kspec(k_blk0), kspec(k_blk0 + DIFF_HEADS),
                  pl.BlockSpec((V_AUG, tk), lambda b, h, p, qt, kt: (h, b * nkb + kt[p])),
                  vec(QK_DIM), vec(QK_DIM), vec(QK_DIM), vec(QK_DIM),
                  pl.BlockSpec((V_DIM, 1), lambda b, h, p, qt, kt: (0, 0))] + cast_specs,
        out_specs=[pl.BlockSpec((tq, V_DIM), lambda b, h, p, qt, kt: (b * nqb + qt[p], h))] + cast_specs,
        scratch_shapes=[pltpu.VMEM((1, tq), F32), pltpu.VMEM((V_AUG, tq), F32),
                        pltpu.VMEM((1, tq), F32), pltpu.VMEM((V_AUG, tq), F32)],
    )
    return pl.pallas_call(
        functools.partial(_diff_attn_kernel, lambda_init=lambda_init, qw=min(tq, 2 * LANES),
                          n_cast=len(to_cast)),
        grid_spec=grid_spec,
        out_shape=[jax.ShapeDtypeStruct((n, MAIN_WIDTH), BF16)]
                  + [jax.ShapeDtypeStruct(a.shape, BF16) for a in to_cast],
        compiler_params=_params(("arbitrary", "arbitrary", "arbitrary"), 48),
        name="diff_attn",
    )(qi_tab, kj_tab, hk, hk, hk, hk, vt, lq1, lk1, lq2, lk2, subln_g.reshape(V_DIM, 1), *to_cast)


def _mem_attn_kernel(q_ref, kv_ref, o_ref):
    _mem_attention(q_ref[...].astype(F32), kv_ref, o_ref, 0)


def _mem_attn_b(hk, memkv, seq, tm):
    n = hk.shape[0]
    return pl.pallas_call(
        _mem_attn_kernel,
        grid=(n // tm,),
        in_specs=[pl.BlockSpec((tm, MEM_WIDTH), lambda i: (i, MAIN_WIDTH // MEM_WIDTH)),
                  pl.BlockSpec((None, None) + memkv.shape[2:], lambda i: (1, (i * tm) // seq, 0, 0))],
        out_specs=pl.BlockSpec((tm, MEM_WIDTH), lambda i: (i, 0)),
        out_shape=jax.ShapeDtypeStruct((n, MEM_WIDTH), BF16),
        compiler_params=_params(("arbitrary",), 32),
        name="mem_attn_b",
    )(hk, memkv)


def _lane_cumsum(v, lane):
    for shift in (1, 2, 4):
        v = v + jnp.where(lane >= shift, pltpu.roll(v, shift, 1), 0.0)
    return v


def _route_kernel(x_ref, wr_ref, pos_ref, gate_ref, tile_e_ref, cnt_ref, base_ref, start_ref, top2_ref,
                  *, moe_tile):
    phase = pl.program_id(0)
    i = pl.program_id(1)
    tm = x_ref.shape[0]
    lane = lax.broadcasted_iota(jnp.int32, (tm, LANES), 1).astype(F32)

    @pl.when(jnp.logical_and(phase == 0, i == 0))
    def _():
        cnt_ref[...] = jnp.zeros_like(cnt_ref)

    @pl.when(phase == 0)
    def _():
        x = x_ref[...]
        xh = x.astype(BF16)
        xl = (x - xh.astype(F32)).astype(BF16)
        w = wr_ref[...]
        wh = w.astype(BF16)
        wl = (w - wh.astype(F32)).astype(BF16)
        logits = _dot(xh, wh) + (_dot(xh, wl) + _dot(xl, wh))
        logits = jnp.where(lane < N_EXPERTS, logits, NEG_BIG)
        v1 = jnp.max(logits, axis=-1, keepdims=True)
        i1 = jnp.min(jnp.where(logits == v1, lane, float(LANES)), axis=-1, keepdims=True)
        rest = jnp.where(lane == i1, NEG_BIG, logits)
        v2 = jnp.max(rest, axis=-1, keepdims=True)
        i2 = jnp.min(jnp.where(rest == v2, lane, float(LANES)), axis=-1, keepdims=True)
        top2_ref[i] = jnp.where(lane == 0, i1, jnp.where(lane == 1, i2, jnp.where(lane == 2, v1, v2)))
        sel = jnp.logical_or(lane == i1, lane == i2)
        cnt_ref[...] += jnp.sum(sel.astype(F32), axis=0, keepdims=True)

    @pl.when(jnp.logical_and(phase == 1, i == 0))
    def _():
        lane8 = lax.broadcasted_iota(jnp.int32, (8, LANES), 1).astype(F32)
        sub8 = lax.broadcasted_iota(jnp.int32, (8, LANES), 0).astype(F32)
        cnt = jnp.broadcast_to(cnt_ref[...], (8, LANES))
        padded = jnp.ceil(cnt * (1.0 / moe_tile)) * moe_tile
        ends = _lane_cumsum(padded, lane8)
        start_ref[...] = (ends - padded)[0:1, :]
        base_ref[...] = jnp.zeros_like(base_ref)
        tile_row0 = (sub8 * LANES + lane8) * moe_tile
        tile_e = jnp.zeros((8, LANES), F32)
        for e in range(N_EXPERTS):
            end_e = jnp.sum(jnp.where(lane8 == e, ends, 0.0), axis=-1, keepdims=True)
            tile_e = tile_e + (tile_row0 >= end_e).astype(F32)
        tile_e_ref[0:8, :] = jnp.minimum(tile_e, N_EXPERTS - 1.0).astype(jnp.int32)
        total = jnp.sum(jnp.where(lane8 == N_EXPERTS - 1, ends, 0.0), axis=-1, keepdims=True)
        tile_e_ref[8:16, :] = jnp.broadcast_to(total * (1.0 / moe_tile), (8, LANES)).astype(jnp.int32)
        tile_e_ref[16:24, :] = (ends - padded + cnt).astype(jnp.int32)

    @pl.when(phase == 1)
    def _():
        rec = top2_ref[i]
        i1, i2, v1, v2 = rec[:, 0:1], rec[:, 1:2], rec[:, 2:3], rec[:, 3:4]
        sel = jnp.logical_or(lane == i1, lane == i2)
        tile_cnt = jnp.sum(sel.astype(F32), axis=0, keepdims=True)
        r = lax.broadcasted_iota(jnp.int32, (tm, tm), 0)
        c = lax.broadcasted_iota(jnp.int32, (tm, tm), 1)
        before = (c < r).astype(BF16)
        rank = _dot(before, sel.astype(BF16))
        slot = start_ref[...] + base_ref[...] + rank
        p1 = jnp.sum(jnp.where(lane == i1, slot, 0.0), axis=-1, keepdims=True)
        p2 = jnp.sum(jnp.where(lane == i2, slot, 0.0), axis=-1, keepdims=True)
        pos_ref[...] = jnp.where(lane == 0, p1, jnp.where(lane == 1, p2, 0.0)).astype(jnp.int32)
        g1 = 1.0 / (1.0 + jnp.exp(v2 - v1))
        g2 = jnp.exp(v2 - v1) * g1
        gate_ref[...] = jnp.where(lane == 0, g1, jnp.where(lane == 1, g2, 0.0))
        base_ref[...] += tile_cnt


def _route(x, w_router_pad, tm, moe_tile):
    n, d = x.shape
    nt = n // tm
    return pl.pallas_call(
        functools.partial(_route_kernel, moe_tile=moe_tile),
        grid=(2, nt),
        in_specs=[pl.BlockSpec((tm, d), lambda ph, i: (i * (1 - ph) + (nt - 1) * ph, 0)),
                  _const_spec((d, LANES), lambda ph, i: (0, 0))],
        out_specs=[pl.BlockSpec((tm, LANES), lambda ph, i: (i * ph, 0)),
                   pl.BlockSpec((tm, LANES), lambda ph, i: (i * ph, 0)),
                   pl.BlockSpec((24, LANES), lambda ph, i: (0, 0))],
        out_shape=[jax.ShapeDtypeStruct((n, LANES), jnp.int32),
                   jax.ShapeDtypeStruct((n, LANES), F32),
                   jax.ShapeDtypeStruct((24, LANES), jnp.int32)],
        scratch_shapes=[pltpu.VMEM((1, LANES), F32), pltpu.VMEM((1, LANES), F32), pltpu.VMEM((1, LANES), F32),
                        pltpu.VMEM((nt, tm, LANES), F32)],
        compiler_params=_params(("arbitrary", "arbitrary"), 48),
        name="route",
    )(x, w_router_pad)


def _dispatch_kernel(pad_row0, pos_ref, x_ref, xs_ref, zero_ref, sem, *, moe_tile):
    tm = x_ref.shape[0] * SUBLANES

    @pl.when(pl.program_id(0) == 0)
    def _():
        zero_ref[...] = jnp.zeros_like(zero_ref)
        zr = zero_ref.shape[0]

        def drain_fill(r, carry):
            pltpu.make_async_copy(zero_ref.at[pl.ds(0, 1)], xs_ref.at[pl.ds(0, 1)], sem).wait()
            return carry

        for e in range(N_EXPERTS):
            def fill(r, carry, e=e):
                pltpu.make_async_copy(zero_ref.at[pl.ds(0, 1)],
                                      xs_ref.at[pl.ds(pad_row0[e] + r, 1)], sem).start()
                return carry

            lax.fori_loop(0, moe_tile, fill, 0, unroll=DMA_LOOP_UNROLL)
            lax.fori_loop(0, moe_tile, drain_fill, 0, unroll=2 * DMA_LOOP_UNROLL)

        n_alloc = xs_ref.shape[0]
        tail = [pltpu.make_async_copy(
                    zero_ref,
                    xs_ref.at[pl.ds(pl.multiple_of(
                        jnp.minimum(pad_row0[N_EXPERTS] + c * zr, n_alloc - zr), zr), zr)], sem)
                for c in range((N_EXPERTS + 1) * moe_tile // zr)]
        for f in tail:
            f.start()
            f.wait()

    def issue(g, carry):
        for u in range(SUBLANES):
            for k in range(2):
                dst = pos_ref[0, 0, 2 * SUBLANES * g + 2 * u + k]
                pltpu.make_async_copy(x_ref.at[g, pl.ds(u, 1)], xs_ref.at[pl.ds(dst, 1)], sem).start(priority=k)
        return carry

    lax.fori_loop(0, tm // SUBLANES, issue, 0)

    def drain(t, carry):
        pltpu.make_async_copy(x_ref.at[0, pl.ds(0, 1)], xs_ref.at[pl.ds(0, 1)], sem).wait()
        return carry

    lax.fori_loop(0, 2 * tm, drain, 0, unroll=2 * DMA_LOOP_UNROLL)


def _dispatch(pad_row0, pos_flat, x, n_rows, moe_tile, tm):
    n, d = x.shape
    zero_rows = min(moe_tile, LANES)
    grid_spec = pltpu.PrefetchScalarGridSpec(
        num_scalar_prefetch=1,
        grid=(n // tm,),
        in_specs=[pl.BlockSpec((1, 1, 2 * tm), lambda i, pr: (i, 0, 0), memory_space=pltpu.SMEM),
                  pl.BlockSpec((tm // SUBLANES, SUBLANES, d), lambda i, pr: (i, 0, 0))],
        out_specs=pl.BlockSpec(memory_space=pl.ANY),
        scratch_shapes=[pltpu.VMEM((zero_rows, d), F32), pltpu.SemaphoreType.DMA(())],
    )
    return pl.pallas_call(
        functools.partial(_dispatch_kernel, moe_tile=moe_tile),
        grid_spec=grid_spec,
        out_shape=jax.ShapeDtypeStruct((n_rows + moe_tile, d), F32),
        compiler_params=_params(("arbitrary",), 32),
        name="moe_dispatch",
    )(pad_row0, pos_flat, x.reshape(n // SUBLANES, SUBLANES, d))


def _experts_kernel(tile_e, n_valid, xs_ref, wg_ref, wu_ref, wd_ref, o_ref, xb_ref):
    del tile_e
    i = pl.program_id(0)
    j = pl.program_id(1)
    valid = i < n_valid[0]

    @pl.when(jnp.logical_and(valid, j == 0))
    def _():
        xb_ref[...] = xs_ref[...].astype(BF16)

    @pl.when(j == 0)
    def _():
        o_ref[...] = jnp.zeros_like(o_ref)

    @pl.when(valid)
    def _():
        xb = xb_ref[...]
        h = (_silu(_dot(xb, wg_ref[...])) * _dot(xb, wu_ref[...])).astype(BF16)
        o_ref[...] += _dot(h, wd_ref[...])


def _experts(tile_e, n_valid, xs, wg_bf, wu_bf, wd_bf, tm, tf):
    d = xs.shape[1]
    m = tile_e.shape[0] * tm
    f = wg_bf.shape[2]
    nj = f // tf

    def row_idx(i, j, te, nv):
        return (jnp.minimum(i, nv[0] - 1), 0)

    def col_of(i, j, nv):
        return jnp.where(i < nv[0], j, nj - 1)

    grid_spec = pltpu.PrefetchScalarGridSpec(
        num_scalar_prefetch=2,
        grid=(m // tm, nj),
        in_specs=[pl.BlockSpec((tm, d), row_idx),
                  pl.BlockSpec((None, d, tf), lambda i, j, te, nv: (te[i], 0, col_of(i, j, nv))),
                  pl.BlockSpec((None, d, tf), lambda i, j, te, nv: (te[i], 0, col_of(i, j, nv))),
                  pl.BlockSpec((None, tf, d), lambda i, j, te, nv: (te[i], col_of(i, j, nv), 0))],
        out_specs=pl.BlockSpec((tm, d), lambda i, j, te, nv: (i, 0)),
        scratch_shapes=[pltpu.VMEM((tm, d), BF16)],
    )
    return pl.pallas_call(
        _experts_kernel,
        grid_spec=grid_spec,
        out_shape=jax.ShapeDtypeStruct((m, d), F32),
        compiler_params=_params(("arbitrary", "arbitrary"), 56),
        name="moe_experts",
    )(tile_e, n_valid, xs, wg_bf, wu_bf, wd_bf)


def _combine_kernel(pos_ref, gate_ref, x_ref, g_ref, b_ref, ys_ref, o_ref, buf_ref, sem):
    tm = x_ref.shape[0]

    def issue(g, carry):
        for u in range(SUBLANES):
            for k in range(2):
                src = pos_ref[0, 0, 2 * SUBLANES * g + 2 * u + k]
                pltpu.make_async_copy(ys_ref.at[pl.ds(src, 1)], buf_ref.at[k, g, pl.ds(u, 1)],
                                      sem).start(priority=k)
        return carry

    lax.fori_loop(0, tm // SUBLANES, issue, 0)

    def drain(t, carry):
        pltpu.make_async_copy(ys_ref.at[pl.ds(0, 1)], buf_ref.at[0, 0, pl.ds(0, 1)], sem).wait()
        return carry

    lax.fori_loop(0, 2 * tm, drain, 0, unroll=2 * DMA_LOOP_UNROLL)

    gates = gate_ref[...]
    d = x_ref.shape[1]
    y = gates[:, 0:1] * buf_ref[0].reshape(tm, d) + gates[:, 1:2] * buf_ref[1].reshape(tm, d)
    o_ref[...] = _layer_norm(ALPHA * x_ref[...] + y, g_ref[...], b_ref[...])


def _combine(pos_flat, gates, x, g, b, ys, tm):
    n, d = x.shape
    return pl.pallas_call(
        _combine_kernel,
        grid=(n // tm,),
        in_specs=[pl.BlockSpec((1, 1, 2 * tm), lambda i: (i, 0, 0), memory_space=pltpu.SMEM),
                  pl.BlockSpec((tm, LANES), lambda i: (i, 0)),
                  pl.BlockSpec((tm, d), lambda i: (i, 0)),
                  _const_spec((1, d), lambda i: (0, 0)),
                  _const_spec((1, d), lambda i: (0, 0)),
                  pl.BlockSpec(memory_space=pl.ANY)],
        out_specs=pl.BlockSpec((tm, d), lambda i: (i, 0)),
        out_shape=jax.ShapeDtypeStruct((n, d), F32),
        scratch_shapes=[pltpu.VMEM((2, tm // SUBLANES, SUBLANES, d), F32), pltpu.SemaphoreType.DMA(())],
        compiler_params=_params(("arbitrary",), 32),
        name="moe_combine",
    )(pos_flat, gates, x, g, b, ys)


def kernel(x, mem, ln_g, ln_b, w_mix_out, w_mem_kv, a_w_in, a_vnorm_g, a_vnorm_b, a_w_s, a_b_s,
           shared_w_kv, b_w_in, b_lambda_q1, b_lambda_k1, b_lambda_q2, b_lambda_k2, b_subln_g,
           ffn_w_gate, ffn_w_up, ffn_w_down, moe_w_router, moe_w_gate, moe_w_up, moe_w_down):
    bsz, seq, d = x.shape
    n = bsz * seq
    t = _tiles(n, seq)
    xf = x.reshape(n, d)
    row = lambda v: v.reshape(1, -1)

    memkv = _memkv(mem, w_mem_kv.astype(BF16))

    mixed = _mixer_a(xf, a_w_in[0].astype(BF16), row(a_vnorm_g[0]), row(a_vnorm_b[0]),
                     a_w_s[0], jnp.transpose(a_b_s[0]), memkv, seq, t["mixer"])
    x1, x1b = _proj_ln(mixed, 0, mixed, MAIN_WIDTH // MEM_WIDTH, w_mix_out[0].astype(BF16), xf,
                       row(ln_g[0, 0]), row(ln_b[0, 0]), t["proj"])
    x2, x2b = _ffn_ln(x1, x1b, ffn_w_gate[0].astype(BF16), ffn_w_up[0].astype(BF16),
                      ffn_w_down[0].astype(BF16), row(ln_g[0, 1]), row(ln_b[0, 1]), t["ffn"], t["ffn_f"])

    w_cat = jnp.concatenate([b_w_in[0], shared_w_kv[:, :MAIN_WIDTH]], axis=1).astype(BF16)
    col_scale = jnp.concatenate([jnp.full((MAIN_WIDTH,), QK_DIM ** -0.5, F32),
                                 jnp.ones((w_cat.shape[1] - MAIN_WIDTH,), F32)]).reshape(1, -1)
    hk = _matmul_colscale(x2b, w_cat, col_scale, t["mm"], t["mm_n"])
    wv_t = jnp.transpose(shared_w_kv[:, MAIN_WIDTH:]).reshape(DIFF_HEADS, V_DIM, d)
    wv_t = jnp.pad(wv_t, ((0, 0), (0, ONES_ROWS), (0, 0))).reshape(DIFF_HEADS * V_AUG, d).astype(BF16)
    ones_bias = jnp.pad(jnp.zeros((DIFF_HEADS, V_DIM, 1), F32), ((0, 0), (0, ONES_ROWS), (0, 0)),
                        constant_values=1.0).reshape(DIFF_HEADS * V_AUG, 1)
    vt = _proj_transposed(x2b, wv_t, ones_bias, t["mm_t"])
    lambda_init = 0.8 - 0.6 * math.exp(-0.3 * 1)
    n_exp, _, d_ff = moe_w_gate[0].shape
    expert_w = [moe_w_gate[0].reshape(n_exp * d, d_ff), moe_w_up[0].reshape(n_exp * d, d_ff),
                moe_w_down[0].reshape(n_exp * d_ff, d)]
    main, wg_bf, wu_bf, wd_bf = _diff_attn(
        hk, vt, row(b_lambda_q1[0]), row(b_lambda_k1[0]), row(b_lambda_q2[0]), row(b_lambda_k2[0]),
        row(b_subln_g[0]), expert_w, bsz, seq, lambda_init, t["attn_q"], t["attn_k"])
    mem_out = _mem_attn_b(hk, memkv, seq, t["mem"])
    x3, _ = _proj_ln(main, 0, mem_out, 0, w_mix_out[1].astype(BF16), x2,
                     row(ln_g[1, 0]), row(ln_b[1, 0]), t["proj"])

    moe_tile = t["moe"]
    n_row_tiles = (2 * n) // moe_tile + N_EXPERTS
    w_router_pad = jnp.pad(moe_w_router[0], ((0, 0), (0, LANES - N_EXPERTS)))
    pos, gates, tile_info = _route(x3, w_router_pad, t["route"], moe_tile)
    tile_e = tile_info[0:8].reshape(-1)[:n_row_tiles]
    n_valid = tile_info[8, 0:1]
    tc = t["comb"]
    pos_flat = pos[:, :2].reshape(n // tc, 1, 2 * tc)
    fill_rows = jnp.concatenate([tile_info[16, :N_EXPERTS], n_valid * moe_tile])
    xs = _dispatch(fill_rows, pos_flat, x3, n_row_tiles * moe_tile, moe_tile, tc)
    ys = _experts(tile_e, n_valid, xs, wg_bf.reshape(n_exp, d, d_ff), wu_bf.reshape(n_exp, d, d_ff),
                  wd_bf.reshape(n_exp, d_ff, d), moe_tile, t["moe_f"])
    x4 = _combine(pos_flat, gates, x3, row(ln_g[1, 1]), row(ln_b[1, 1]), ys, tc)
    return x4.reshape(bsz, seq, d)
```

```python
import functools
import math

import numpy as np
import jax
import jax.numpy as jnp
from jax import lax
from jax.experimental import pallas as pl
from jax.experimental.pallas import tpu as pltpu

BF16 = jnp.bfloat16
F32 = jnp.float32

D_MODEL = 2048
MEM_WIDTH = 512
MAIN_WIDTH = 1536
MEM_HEADS = 4
MEM_HEAD_DIM = 128
CHUNK = 128
SG_GROUPS = 12
QK_DIM = 128
V_DIM = 256
DIFF_HEADS = 6
N_EXPERTS = 8
DEPTH = 2
ALPHA = (2.0 * DEPTH) ** 0.25
LN_EPS = 1e-5
LANES = 128
SUBLANES = 8
ONES_ROWS = 16
V_AUG = V_DIM + ONES_ROWS
NEG_BIG = -1e30
DMA_LOOP_UNROLL = 8
MIB = 1024 * 1024
V7X_VMEM_MIB = 64
VMEM_LARGE_MIB = V7X_VMEM_MIB - 8
VMEM_MID_MIB = V7X_VMEM_MIB - 16
VMEM_SMALL_MIB = V7X_VMEM_MIB // 2


def _tiles(n_tokens, seq):
    def fit(pref, total):
        t = min(pref, total)
        while total % t:
            t //= 2
        return t
    return dict(
        mixer=fit(512, seq),
        proj=fit(512, n_tokens),
        ffn=fit(512, n_tokens),
        ffn_f=512,
        mm=fit(1024, n_tokens),
        mm_n=1792,
        mm_t=fit(512, n_tokens),
        attn_q=fit(1024, seq),
        attn_k=fit(1024, seq),
        mem=fit(512, seq),
        route=fit(512, n_tokens),
        moe=fit(512, n_tokens),
        moe_f=1024,
        comb=fit(512, n_tokens),
    )


def _params(sem, vmem_mib):
    return pltpu.CompilerParams(dimension_semantics=sem, vmem_limit_bytes=vmem_mib * MIB)


def _const_spec(shape, index_map):
    return pl.BlockSpec(shape, index_map, pipeline_mode=pl.Buffered(1))


def _dot(a, b):
    return jnp.dot(a, b, preferred_element_type=F32)


def _dot_nt(a, b):
    return lax.dot_general(a, b, (((1,), (1,)), ((), ())), preferred_element_type=F32)


def _layer_norm(v, g, b):
    mu = jnp.mean(v, axis=-1, keepdims=True)
    c = v - mu
    var = jnp.mean(c * c, axis=-1, keepdims=True)
    return c * lax.rsqrt(var + LN_EPS) * g + b


def _gelu_tanh(v):
    return 0.5 * v * (1.0 + jnp.tanh(math.sqrt(2.0 / math.pi) * (v + 0.044715 * (v * v * v))))


def _silu(v):
    return v / (1.0 + jnp.exp(-v))


def _mem_attention(q, kv_ref, o_ref, col0):
    scale = MEM_HEAD_DIM ** -0.5
    for h in range(MEM_HEADS):
        lo = h * MEM_HEAD_DIM
        qh = q[:, lo:lo + MEM_HEAD_DIM].astype(BF16)
        kh = kv_ref[:, lo:lo + MEM_HEAD_DIM]
        vh = kv_ref[:, MEM_WIDTH + lo:MEM_WIDTH + lo + MEM_HEAD_DIM]
        s = _dot_nt(qh, kh) * scale
        e = jnp.exp(s - jnp.max(s, axis=-1, keepdims=True))
        o = _dot(e.astype(BF16), vh) / jnp.sum(e, axis=-1, keepdims=True)
        o_ref[:, col0 + lo:col0 + lo + MEM_HEAD_DIM] = o.astype(o_ref.dtype)


def _memkv_kernel(mem_ref, w_ref, o_ref):
    o_ref[...] = _dot(mem_ref[...].astype(BF16), w_ref[...]).astype(o_ref.dtype)


def _memkv(mem, w_mem_kv_bf):
    bsz, n_mem, d = mem.shape
    depth, _, cols = w_mem_kv_bf.shape
    return pl.pallas_call(
        _memkv_kernel,
        grid=(depth, bsz),
        in_specs=[pl.BlockSpec((None, n_mem, d), lambda l, b: (b, 0, 0)),
                  pl.BlockSpec((None, d, cols), lambda l, b: (l, 0, 0))],
        out_specs=pl.BlockSpec((None, None, n_mem, cols), lambda l, b: (l, b, 0, 0)),
        out_shape=jax.ShapeDtypeStruct((depth, bsz, n_mem, cols), BF16),
        compiler_params=_params(("arbitrary", "arbitrary"), VMEM_SMALL_MIB),
        name="memkv",
    )(mem, w_mem_kv_bf)


def _mixer_a_kernel(x_ref, w_ref, vg_ref, vb_ref, ws_ref, bs_ref, kv_ref, o_ref):
    tm = x_ref.shape[0]
    xb = x_ref[...].astype(BF16)
    v = _gelu_tanh(_dot(xb, w_ref[:, MAIN_WIDTH:2 * MAIN_WIDTH]))
    vn = _layer_norm(v, vg_ref[...], vb_ref[...]).astype(BF16)
    u = _gelu_tanh(_dot(xb, w_ref[:, :MAIN_WIDTH]))
    row = lax.broadcasted_iota(jnp.int32, (CHUNK, CHUNK), 0)
    col = lax.broadcasted_iota(jnp.int32, (CHUNK, CHUNK), 1)
    causal = col <= row
    for g in range(SG_GROUPS):
        wg = jnp.where(causal, ws_ref[g], 0.0).astype(BF16)
        bias = bs_ref[:, g:g + 1]
        for c in range(tm // CHUNK):
            rows = slice(c * CHUNK, (c + 1) * CHUNK)
            cols = slice(g * CHUNK, (g + 1) * CHUNK)
            s = _dot(wg, vn[rows, cols]) + bias
            o_ref[rows, cols] = (u[rows, cols] * s).astype(o_ref.dtype)
    q_mem = _dot(xb, w_ref[:, 2 * MAIN_WIDTH:])
    _mem_attention(q_mem, kv_ref, o_ref, MAIN_WIDTH)


def _mixer_a(x, w_in_bf, vnorm_g, vnorm_b, w_s, b_s_t, memkv, seq, tm):
    n, d = x.shape
    in_cols = w_in_bf.shape[1]
    return pl.pallas_call(
        _mixer_a_kernel,
        grid=(n // tm,),
        in_specs=[pl.BlockSpec((tm, d), lambda i: (i, 0)),
                  _const_spec((d, in_cols), lambda i: (0, 0)),
                  _const_spec((1, MAIN_WIDTH), lambda i: (0, 0)),
                  _const_spec((1, MAIN_WIDTH), lambda i: (0, 0)),
                  _const_spec((SG_GROUPS, CHUNK, CHUNK), lambda i: (0, 0, 0)),
                  _const_spec((CHUNK, SG_GROUPS), lambda i: (0, 0)),
                  pl.BlockSpec((None, None) + memkv.shape[2:], lambda i: (0, (i * tm) // seq, 0, 0))],
        out_specs=pl.BlockSpec((tm, d), lambda i: (i, 0)),
        out_shape=jax.ShapeDtypeStruct((n, d), BF16),
        compiler_params=_params(("arbitrary",), VMEM_LARGE_MIB),
        name="mixer_a",
    )(x, w_in_bf, vnorm_g, vnorm_b, w_s, b_s_t, memkv)


def _proj_ln_kernel(a1_ref, a2_ref, w_ref, x_ref, g_ref, b_ref, o32_ref, *maybe_o16_ref, sub):
    for r in range(0, x_ref.shape[0], sub):
        rows = slice(r, r + sub)
        mix = _dot(a1_ref[rows, :], w_ref[:MAIN_WIDTH, :]) + _dot(a2_ref[rows, :], w_ref[MAIN_WIDTH:, :])
        y = _layer_norm(ALPHA * x_ref[rows, :] + mix, g_ref[...], b_ref[...])
        o32_ref[rows, :] = y
        for o16_ref in maybe_o16_ref:
            o16_ref[rows, :] = y.astype(BF16)


def _proj_ln(a_main, main_blk, a_mem, mem_blk, w_bf, x, g, b, tm, with_bf16):
    n, d = x.shape
    n_out = 2 if with_bf16 else 1
    return pl.pallas_call(
        functools.partial(_proj_ln_kernel, sub=min(tm, 2 * LANES)),
        grid=(n // tm,),
        in_specs=[pl.BlockSpec((tm, MAIN_WIDTH), lambda i: (i, main_blk)),
                  pl.BlockSpec((tm, MEM_WIDTH), lambda i: (i, mem_blk)),
                  _const_spec((d, d), lambda i: (0, 0)),
                  pl.BlockSpec((tm, d), lambda i: (i, 0)),
                  _const_spec((1, d), lambda i: (0, 0)),
                  _const_spec((1, d), lambda i: (0, 0))],
        out_specs=[pl.BlockSpec((tm, d), lambda i: (i, 0))] * n_out,
        out_shape=[jax.ShapeDtypeStruct((n, d), F32), jax.ShapeDtypeStruct((n, d), BF16)][:n_out],
        compiler_params=_params(("arbitrary",), VMEM_MID_MIB),
        name="proj_ln",
    )(a_main, a_mem, w_bf, x, g, b)


def _ffn_ln_kernel(x_ref, xb_ref, wg_ref, wu_ref, wd_ref, g_ref, b_ref, o32_ref, o16_ref, acc_ref):
    j = pl.program_id(1)

    @pl.when(j == 0)
    def _():
        acc_ref[...] = jnp.zeros_like(acc_ref)

    xb = xb_ref[...]
    h = (_silu(_dot(xb, wg_ref[...])) * _dot(xb, wu_ref[...])).astype(BF16)
    acc_ref[...] += _dot(h, wd_ref[...])

    @pl.when(j == pl.num_programs(1) - 1)
    def _():
        y = _layer_norm(ALPHA * x_ref[...] + acc_ref[...], g_ref[...], b_ref[...])
        o32_ref[...] = y
        o16_ref[...] = y.astype(BF16)


def _ffn_ln(x, xb, wg_bf, wu_bf, wd_bf, g, b, tm, tf):
    n, d = x.shape
    f = wg_bf.shape[1]
    return pl.pallas_call(
        _ffn_ln_kernel,
        grid=(n // tm, f // tf),
        in_specs=[pl.BlockSpec((tm, d), lambda i, j: (i, 0)),
                  pl.BlockSpec((tm, d), lambda i, j: (i, 0)),
                  pl.BlockSpec((d, tf), lambda i, j: (0, j)),
                  pl.BlockSpec((d, tf), lambda i, j: (0, j)),
                  pl.BlockSpec((tf, d), lambda i, j: (j, 0)),
                  _const_spec((1, d), lambda i, j: (0, 0)),
                  _const_spec((1, d), lambda i, j: (0, 0))],
        out_specs=[pl.BlockSpec((tm, d), lambda i, j: (i, 0)),
                   pl.BlockSpec((tm, d), lambda i, j: (i, 0))],
        out_shape=[jax.ShapeDtypeStruct((n, d), F32), jax.ShapeDtypeStruct((n, d), BF16)],
        scratch_shapes=[pltpu.VMEM((tm, d), F32)],
        compiler_params=_params(("arbitrary", "arbitrary"), VMEM_LARGE_MIB),
        name="ffn_ln",
    )(x, xb, wg_bf, wu_bf, wd_bf, g, b)


def _matmul_kernel(x_ref, w_ref, s_ref, o_ref):
    o_ref[...] = (_dot(x_ref[...], w_ref[...]) * s_ref[...]).astype(o_ref.dtype)


def _matmul_colscale(xb, w_bf, col_scale, tm, tn):
    n, d = xb.shape
    cols = w_bf.shape[1]
    return pl.pallas_call(
        _matmul_kernel,
        grid=(n // tm, cols // tn),
        in_specs=[pl.BlockSpec((tm, d), lambda i, j: (i, 0)),
                  pl.BlockSpec((d, tn), lambda i, j: (0, j)),
                  pl.BlockSpec((1, tn), lambda i, j: (0, j))],
        out_specs=pl.BlockSpec((tm, tn), lambda i, j: (i, j)),
        out_shape=jax.ShapeDtypeStruct((n, cols), BF16),
        compiler_params=_params(("arbitrary", "arbitrary"), VMEM_MID_MIB),
        name="in_proj_b",
    )(xb, w_bf, col_scale)


def _proj_t_kernel(wt_ref, bias_ref, x_ref, o_ref):
    o_ref[...] = (_dot_nt(wt_ref[...], x_ref[...]) + bias_ref[...]).astype(o_ref.dtype)


def _proj_transposed(xb, wt_bf, bias_col, tn):
    n, d = xb.shape
    cols = wt_bf.shape[0]
    return pl.pallas_call(
        _proj_t_kernel,
        grid=(n // tn,),
        in_specs=[_const_spec((cols, d), lambda i: (0, 0)),
                  _const_spec((cols, 1), lambda i: (0, 0)),
                  pl.BlockSpec((tn, d), lambda i: (i, 0))],
        out_specs=pl.BlockSpec((cols, tn), lambda i: (0, i)),
        out_shape=jax.ShapeDtypeStruct((cols, n), BF16),
        compiler_params=_params(("arbitrary",), VMEM_MID_MIB),
        name="v_proj_t",
    )(wt_bf, bias_col, xb)


def _diff_attn_kernel(qi_tab, kj_tab, q1_ref, q2_ref, k1_ref, k2_ref, vt_ref,
                      lq1_ref, lk1_ref, lq2_ref, lk2_ref, sg_ref, *rest, lambda_init, qw, n_cast):
    cast_src = rest[:n_cast]
    o_ref = rest[n_cast]
    cast_dst = rest[n_cast + 1:2 * n_cast + 1]
    m1_ref, a1_ref, m2_ref, a2_ref = rest[2 * n_cast + 1:]
    tq, tk = q1_ref.shape[0], k1_ref.shape[0]
    p = pl.program_id(2)
    qi = qi_tab[p]
    kj = kj_tab[p]

    @pl.when(kj == 0)
    def _():
        for m_ref, a_ref in ((m1_ref, a1_ref), (m2_ref, a2_ref)):
            m_ref[...] = jnp.full_like(m_ref, NEG_BIG)
            a_ref[...] = jnp.zeros_like(a_ref)

    chains = [(q_ref, k_ref, m_ref, a_ref, j)
              for j in range(tq // qw)
              for q_ref, k_ref, m_ref, a_ref in ((q1_ref, k1_ref, m1_ref, a1_ref),
                                                 (q2_ref, k2_ref, m2_ref, a2_ref))]

    def step(key0):
        if key0 is None:
            live = chains
        else:
            live = [c for c in chains if key0 <= (c[4] + 1) * qw - 1]

        def scores(chain):
            q_ref, k_ref, j = chain[0], chain[1], chain[4]
            s = _dot_nt(k_ref[...], q_ref[j * qw:(j + 1) * qw, :])
            if key0 is not None and key0 + tk - 1 > j * qw:
                key = key0 + lax.broadcasted_iota(jnp.int32, (tk, qw), 0)
                qry = j * qw + lax.broadcasted_iota(jnp.int32, (tk, qw), 1)
                s = jnp.where(key <= qry, s, NEG_BIG)
            return s.astype(BF16)

        def softmax(chain, s):
            m_ref, j = chain[2], chain[4]
            cols = slice(j * qw, (j + 1) * qw)
            m_old = m_ref[:, cols]
            m_new = jnp.maximum(m_old, jnp.max(s, axis=0, keepdims=True).astype(F32))
            alpha = jnp.exp(m_old - m_new)
            e = jnp.exp(s - m_new.astype(BF16))
            m_ref[:, cols] = m_new
            return e, alpha

        def weighted_values(chain, e, alpha):
            a_ref, j = chain[3], chain[4]
            cols = slice(j * qw, (j + 1) * qw)
            a_ref[:, cols] = alpha * a_ref[:, cols] + _dot(vt_ref[...], e)

        n = len(live)
        s_live, e_live = {}, {}
        for t in range(n + 2):
            if t < len(cast_src):
                cast_dst[t][...] = cast_src[t][...].astype(BF16)
            if t < n:
                s_live[t] = scores(live[t])
            if 0 <= t - 1 < n:
                e_live[t - 1] = softmax(live[t - 1], s_live.pop(t - 1))
            if 0 <= t - 2 < n:
                weighted_values(live[t - 2], *e_live.pop(t - 2))

    key0 = kj * tk - qi * tq
    pl.when(key0 < 0)(lambda: step(None))
    for static_key0 in range(0, tq, tk):
        pl.when(key0 == static_key0)(functools.partial(step, static_key0))

    @pl.when(kj == ((qi + 1) * tq - 1) // tk)
    def _():
        lam = (jnp.exp(jnp.sum(lq1_ref[...] * lk1_ref[...], axis=-1, keepdims=True))
               - jnp.exp(jnp.sum(lq2_ref[...] * lk2_ref[...], axis=-1, keepdims=True))
               + lambda_init)
        o1 = a1_ref[0:V_DIM, :] / a1_ref[V_DIM:V_DIM + 1, :]
        o2 = a2_ref[0:V_DIM, :] / a2_ref[V_DIM:V_DIM + 1, :]
        o = o1 - lam * o2
        o = o * lax.rsqrt(jnp.mean(o * o, axis=0, keepdims=True) + LN_EPS) * sg_ref[...]
        o_ref[...] = jnp.transpose(o * (1.0 - lambda_init)).astype(o_ref.dtype)


def _diff_attn(hk, vt, lq1, lk1, lq2, lk2, subln_g, to_cast, bsz, seq, lambda_init, tq, tk):
    n = hk.shape[0]
    assert tq % tk == 0 and tk >= 2
    nq = seq // tq
    pairs =[(qi, kj) for qi in range(nq) for kj in range(((qi + 1) * tq - 1) // tk + 1)]
    qi_tab = jnp.asarray(np.array([p[0] for p in pairs], np.int32))
    kj_tab = jnp.asarray(np.array([p[1] for p in pairs], np.int32))
    nqb, nkb = seq // tq, seq // tk
    n_pairs = len(pairs)
    n_steps = bsz * DIFF_HEADS * n_pairs
    k_blk0 = (MAIN_WIDTH + MEM_WIDTH) // QK_DIM
    qspec = lambda off: pl.BlockSpec((tq, QK_DIM), lambda b, h, p, qt, kt: (b * nqb + qt[p], off + h))
    kspec = lambda off: pl.BlockSpec((tk, QK_DIM), lambda b, h, p, qt, kt: (b * nkb + kt[p], off + h))
    vec = lambda w: pl.BlockSpec((1, w), lambda b, h, p, qt, kt: (0, 0))

    def cast_spec(arr):
        rows, cols = arr.shape
        n_blk = 1
        while n_blk * 2 <= n_steps and rows % (n_blk * 2) == 0 and (rows // (n_blk * 2)) % 16 == 0:
            n_blk *= 2
        return pl.BlockSpec((rows // n_blk, cols), lambda b, h, p, qt, kt: (
            jnp.minimum((b * DIFF_HEADS + h) * n_pairs + p, n_blk - 1), 0))

    cast_specs = [cast_spec(a) for a in to_cast]
    grid_spec = pltpu.PrefetchScalarGridSpec(
        num_scalar_prefetch=2,
        grid=(bsz, DIFF_HEADS, n_pairs),
        in_specs=[qspec(0), qspec(DIFF_HEADS), kspec(k_blk0), kspec(k_blk0 + DIFF_HEADS),
                  pl.BlockSpec((V_AUG, tk), lambda b, h, p, qt, kt: (h, b * nkb + kt[p])),
                  vec(QK_DIM), vec(QK_DIM), vec(QK_DIM), vec(QK_DIM),
                  pl.BlockSpec((V_DIM, 1), lambda b, h, p, qt, kt: (0, 0))] + cast_specs,
        out_specs=[pl.BlockSpec((tq, V_DIM), lambda b, h, p, qt, kt: (b * nqb + qt[p], h))] + cast_specs,
        scratch_shapes=[pltpu.VMEM((1, tq), F32), pltpu.VMEM((V_AUG, tq), F32),
                        pltpu.VMEM((1, tq), F32), pltpu.VMEM((V_AUG, tq), F32)],
    )
    return pl.pallas_call(
        functools.partial(_diff_attn_kernel, lambda_init=lambda_init, qw=min(tq, 4 * LANES),
                          n_cast=len(to_cast)),
        grid_spec=grid_spec,
        out_shape=[jax.ShapeDtypeStruct((n, MAIN_WIDTH), BF16)]
                  + [jax.ShapeDtypeStruct(a.shape, BF16) for a in to_cast],
        compiler_params=_params(("arbitrary", "arbitrary", "arbitrary"), VMEM_MID_MIB),
        name="diff_attn",
    )(qi_tab, kj_tab, hk, hk, hk, hk, vt, lq1, lk1, lq2, lk2, subln_g.reshape(V_DIM, 1), *to_cast)


def _mem_attn_kernel(q_ref, kv_ref, o_ref):
    _mem_attention(q_ref[...].astype(F32), kv_ref, o_ref, 0)


def _mem_attn_b(hk, memkv, seq, tm):
    n = hk.shape[0]
    return pl.pallas_call(
        _mem_attn_kernel,
        grid=(n // tm,),
        in_specs=[pl.BlockSpec((tm, MEM_WIDTH), lambda i: (i, MAIN_WIDTH // MEM_WIDTH)),
                  pl.BlockSpec((None, None) + memkv.shape[2:], lambda i: (1, (i * tm) // seq, 0, 0))],
        out_specs=pl.BlockSpec((tm, MEM_WIDTH), lambda i: (i, 0)),
        out_shape=jax.ShapeDtypeStruct((n, MEM_WIDTH), BF16),
        compiler_params=_params(("arbitrary",), VMEM_SMALL_MIB),
        name="mem_attn_b",
    )(hk, memkv)


def _lane_cumsum(v, lane):
    for shift in (1, 2, 4):
        v = v + jnp.where(lane >= shift, pltpu.roll(v, shift, 1), 0.0)
    return v


def _route_kernel(x_ref, wr_ref, pos_ref, gate_ref, tile_e_ref, cnt_ref, base_ref, start_ref, top2_ref,
                  *, moe_tile):
    phase = pl.program_id(0)
    i = pl.program_id(1)
    tm = x_ref.shape[0]
    lane = lax.broadcasted_iota(jnp.int32, (tm, LANES), 1).astype(F32)

    @pl.when(jnp.logical_and(phase == 0, i == 0))
    def _():
        cnt_ref[...] = jnp.zeros_like(cnt_ref)

    @pl.when(phase == 0)
    def _():
        x = x_ref[...]
        xh = x.astype(BF16)
        xl = (x - xh.astype(F32)).astype(BF16)
        w = wr_ref[...]
        wh = w.astype(BF16)
        wl = (w - wh.astype(F32)).astype(BF16)
        logits = _dot(xh, wh) + (_dot(xh, wl) + _dot(xl, wh))
        logits = jnp.where(lane < N_EXPERTS, logits, NEG_BIG)
        v1 = jnp.max(logits, axis=-1, keepdims=True)
        i1 = jnp.min(jnp.where(logits == v1, lane, float(LANES)), axis=-1, keepdims=True)
        rest = jnp.where(lane == i1, NEG_BIG, logits)
        v2 = jnp.max(rest, axis=-1, keepdims=True)
        i2 = jnp.min(jnp.where(rest == v2, lane, float(LANES)), axis=-1, keepdims=True)
        top2_ref[i] = jnp.where(lane == 0, i1, jnp.where(lane == 1, i2, jnp.where(lane == 2, v1, v2)))
        sel = jnp.logical_or(lane == i1, lane == i2)
        cnt_ref[...] += jnp.sum(sel.astype(F32), axis=0, keepdims=True)

    @pl.when(jnp.logical_and(phase == 1, i == 0))
    def _():
        lane8 = lax.broadcasted_iota(jnp.int32, (8, LANES), 1).astype(F32)
        sub8 = lax.broadcasted_iota(jnp.int32, (8, LANES), 0).astype(F32)
        cnt = jnp.broadcast_to(cnt_ref[...], (8, LANES))
        padded = jnp.ceil(cnt * (1.0 / moe_tile)) * moe_tile
        ends = _lane_cumsum(padded, lane8)
        start_ref[...] = (ends - padded)[0:1, :]
        base_ref[...] = jnp.zeros_like(base_ref)
        tile_row0 = (sub8 * LANES + lane8) * moe_tile
        tile_e = jnp.zeros((8, LANES), F32)
        for e in range(N_EXPERTS):
            end_e = jnp.sum(jnp.where(lane8 == e, ends, 0.0), axis=-1, keepdims=True)
            tile_e = tile_e + (tile_row0 >= end_e).astype(F32)
        tile_e_ref[0:8, :] = jnp.minimum(tile_e, N_EXPERTS - 1.0).astype(jnp.int32)
        total = jnp.sum(jnp.where(lane8 == N_EXPERTS - 1, ends, 0.0), axis=-1, keepdims=True)
        tile_e_ref[8:16, :] = jnp.broadcast_to(total * (1.0 / moe_tile), (8, LANES)).astype(jnp.int32)
        tile_e_ref[16:24, :] = (ends - padded + cnt).astype(jnp.int32)

    @pl.when(phase == 1)
    def _():
        rec = top2_ref[i]
        i1, i2, v1, v2 = rec[:, 0:1], rec[:, 1:2], rec[:, 2:3], rec[:, 3:4]
        sel = jnp.logical_or(lane == i1, lane == i2)
        tile_cnt = jnp.sum(sel.astype(F32), axis=0, keepdims=True)
        r = lax.broadcasted_iota(jnp.int32, (tm, tm), 0)
        c = lax.broadcasted_iota(jnp.int32, (tm, tm), 1)
        before = (c < r).astype(BF16)
        rank = _dot(before, sel.astype(BF16))
        slot = start_ref[...] + base_ref[...] + rank
        p1 = jnp.sum(jnp.where(lane == i1, slot, 0.0), axis=-1, keepdims=True)
        p2 = jnp.sum(jnp.where(lane == i2, slot, 0.0), axis=-1, keepdims=True)
        pos_ref[...] = jnp.where(lane == 0, p1, jnp.where(lane == 1, p2, 0.0)).astype(jnp.int32)
        g1 = 1.0 / (1.0 + jnp.exp(v2 - v1))
        g2 = jnp.exp(v2 - v1) * g1
        gate_ref[...] = jnp.where(lane == 0, g1, jnp.where(lane == 1, g2, 0.0))
        base_ref[...] += tile_cnt


def _route(x, w_router_pad, tm, moe_tile):
    n, d = x.shape
    nt = n // tm
    return pl.pallas_call(
        functools.partial(_route_kernel, moe_tile=moe_tile),
        grid=(2, nt),
        in_specs=[pl.BlockSpec((tm, d), lambda ph, i: (i * (1 - ph) + (nt - 1) * ph, 0)),
                  _const_spec((d, LANES), lambda ph, i: (0, 0))],
        out_specs=[pl.BlockSpec((tm, LANES), lambda ph, i: (i * ph, 0)),
                   pl.BlockSpec((tm, LANES), lambda ph, i: (i * ph, 0)),
                   pl.BlockSpec((24, LANES), lambda ph, i: (0, 0))],
        out_shape=[jax.ShapeDtypeStruct((n, LANES), jnp.int32),
                   jax.ShapeDtypeStruct((n, LANES), F32),
                   jax.ShapeDtypeStruct((24, LANES), jnp.int32)],
        scratch_shapes=[pltpu.VMEM((1, LANES), F32), pltpu.VMEM((1, LANES), F32), pltpu.VMEM((1, LANES), F32),
                        pltpu.VMEM((nt, tm, LANES), F32)],
        compiler_params=_params(("arbitrary", "arbitrary"), VMEM_MID_MIB),
        name="route",
    )(x, w_router_pad)


def _dispatch_kernel(pad_row0, pos_ref, x_ref, xs_ref, zero_ref, sem, *, moe_tile):
    tm = x_ref.shape[0] * SUBLANES

    @pl.when(pl.program_id(0) == 0)
    def _():
        zero_ref[...] = jnp.zeros_like(zero_ref)
        zr = zero_ref.shape[0]

        def drain_fill(r, carry):
            pltpu.make_async_copy(zero_ref.at[pl.ds(0, 1)], xs_ref.at[pl.ds(0, 1)], sem).wait()
            return carry

        for e in range(N_EXPERTS):
            def fill(r, carry, e=e):
                pltpu.make_async_copy(zero_ref.at[pl.ds(0, 1)],
                                      xs_ref.at[pl.ds(pad_row0[e] + r, 1)], sem).start()
                return carry

            lax.fori_loop(0, moe_tile, fill, 0, unroll=DMA_LOOP_UNROLL)
            lax.fori_loop(0, moe_tile, drain_fill, 0, unroll=2 * DMA_LOOP_UNROLL)

        n_alloc = xs_ref.shape[0]
        tail = [pltpu.make_async_copy(
                    zero_ref,
                    xs_ref.at[pl.ds(pl.multiple_of(
                        jnp.minimum(pad_row0[N_EXPERTS] + c * zr, n_alloc - zr), zr), zr)], sem)
                for c in range((N_EXPERTS + 1) * moe_tile // zr)]
        for f in tail:
            f.start()
            f.wait()

    def issue(g, carry):
        for u in range(SUBLANES):
            for k in range(2):
                dst = pos_ref[0, 0, 2 * SUBLANES * g + 2 * u + k]
                pltpu.make_async_copy(x_ref.at[g, pl.ds(u, 1)], xs_ref.at[pl.ds(dst, 1)], sem).start(priority=k)
        return carry

    lax.fori_loop(0, tm // SUBLANES, issue, 0)

    def drain(t, carry):
        pltpu.make_async_copy(x_ref.at[0, pl.ds(0, 1)], xs_ref.at[pl.ds(0, 1)], sem).wait()
        return carry

    lax.fori_loop(0, 2 * tm, drain, 0, unroll=2 * DMA_LOOP_UNROLL)


def _dispatch(pad_row0, pos_flat, x, n_rows, moe_tile, tm):
    n, d = x.shape
    zero_rows = min(moe_tile, LANES)
    grid_spec = pltpu.PrefetchScalarGridSpec(
        num_scalar_prefetch=1,
        grid=(n // tm,),
        in_specs=[pl.BlockSpec((1, 1, 2 * tm), lambda i, pr: (i, 0, 0), memory_space=pltpu.SMEM),
                  pl.BlockSpec((tm // SUBLANES, SUBLANES, d), lambda i, pr: (i, 0, 0))],
        out_specs=pl.BlockSpec(memory_space=pl.ANY),
        scratch_shapes=[pltpu.VMEM((zero_rows, d), F32), pltpu.SemaphoreType.DMA(())],
    )
    return pl.pallas_call(
        functools.partial(_dispatch_kernel, moe_tile=moe_tile),
        grid_spec=grid_spec,
        out_shape=jax.ShapeDtypeStruct((n_rows + moe_tile, d), F32),
        compiler_params=_params(("arbitrary",), VMEM_SMALL_MIB),
        name="moe_dispatch",
    )(pad_row0, pos_flat, x.reshape(n // SUBLANES, SUBLANES, d))


def _experts_kernel(tile_e, n_valid, xs_ref, wg_ref, wu_ref, wd_ref, o_ref, xb_ref):
    del tile_e
    i = pl.program_id(0)
    j = pl.program_id(1)
    valid = i < n_valid[0]

    @pl.when(jnp.logical_and(valid, j == 0))
    def _():
        xb_ref[...] = xs_ref[...].astype(BF16)

    @pl.when(j == 0)
    def _():
        o_ref[...] = jnp.zeros_like(o_ref)

    @pl.when(valid)
    def _():
        xb = xb_ref[...]
        h = (_silu(_dot(xb, wg_ref[...])) * _dot(xb, wu_ref[...])).astype(BF16)
        o_ref[...] += _dot(h, wd_ref[...])


def _experts(tile_e, n_valid, xs, wg_bf, wu_bf, wd_bf, tm, tf):
    d = xs.shape[1]
    m = tile_e.shape[0] * tm
    f = wg_bf.shape[2]
    nj = f // tf

    def row_idx(i, j, te, nv):
        return (jnp.minimum(i, nv[0] - 1), 0)

    def col_of(i, j, nv):
        return jnp.where(i < nv[0], j, nj - 1)

    grid_spec = pltpu.PrefetchScalarGridSpec(
        num_scalar_prefetch=2,
        grid=(m // tm, nj),
        in_specs=[pl.BlockSpec((tm, d), row_idx),
                  pl.BlockSpec((None, d, tf), lambda i, j, te, nv: (te[i], 0, col_of(i, j, nv))),
                  pl.BlockSpec((None, d, tf), lambda i, j, te, nv: (te[i], 0, col_of(i, j, nv))),
                  pl.BlockSpec((None, tf, d), lambda i, j, te, nv: (te[i], col_of(i, j, nv), 0))],
        out_specs=pl.BlockSpec((tm, d), lambda i, j, te, nv: (i, 0)),
        scratch_shapes=[pltpu.VMEM((tm, d), BF16)],
    )
    return pl.pallas_call(
        _experts_kernel,
        grid_spec=grid_spec,
        out_shape=jax.ShapeDtypeStruct((m, d), F32),
        compiler_params=_params(("arbitrary", "arbitrary"), VMEM_LARGE_MIB),
        name="moe_experts",
    )(tile_e, n_valid, xs, wg_bf, wu_bf, wd_bf)


def _combine_kernel(pos_ref, gate_ref, x_ref, g_ref, b_ref, ys_ref, o_ref, buf_ref, sem):
    tm = x_ref.shape[0]

    def issue(g, carry):
        for u in range(SUBLANES):
            for k in range(2):
                src = pos_ref[0, 0, 2 * SUBLANES * g + 2 * u + k]
                pltpu.make_async_copy(ys_ref.at[pl.ds(src, 1)], buf_ref.at[k, g, pl.ds(u, 1)],
                                      sem).start(priority=k)
        return carry

    lax.fori_loop(0, tm // SUBLANES, issue, 0)

    def drain(t, carry):
        pltpu.make_async_copy(ys_ref.at[pl.ds(0, 1)], buf_ref.at[0, 0, pl.ds(0, 1)], sem).wait()
        return carry

    lax.fori_loop(0, 2 * tm, drain, 0, unroll=2 * DMA_LOOP_UNROLL)

    gates = gate_ref[...]
    d = x_ref.shape[1]
    y = gates[:, 0:1] * buf_ref[0].reshape(tm, d) + gates[:, 1:2] * buf_ref[1].reshape(tm, d)
    o_ref[...] = _layer_norm(ALPHA * x_ref[...] + y, g_ref[...], b_ref[...])


def _combine(pos_flat, gates, x, g, b, ys, tm):
    n, d = x.shape
    return pl.pallas_call(
        _combine_kernel,
        grid=(n // tm,),
        in_specs=[pl.BlockSpec((1, 1, 2 * tm), lambda i: (i, 0, 0), memory_space=pltpu.SMEM),
                  pl.BlockSpec((tm, LANES), lambda i: (i, 0)),
                  pl.BlockSpec((tm, d), lambda i: (i, 0)),
                  _const_spec((1, d), lambda i: (0, 0)),
                  _const_spec((1, d), lambda i: (0, 0)),
                  pl.BlockSpec(memory_space=pl.ANY)],
        out_specs=pl.BlockSpec((tm, d), lambda i: (i, 0)),
        out_shape=jax.ShapeDtypeStruct((n, d), F32),
        scratch_shapes=[pltpu.VMEM((2, tm // SUBLANES, SUBLANES, d), F32), pltpu.SemaphoreType.DMA(())],
        compiler_params=_params(("arbitrary",), VMEM_SMALL_MIB),
        name="moe_combine",
    )(pos_flat, gates, x, g, b, ys)


def kernel(x, mem, ln_g, ln_b, w_mix_out, w_mem_kv, a_w_in, a_vnorm_g, a_vnorm_b, a_w_s, a_b_s,
           shared_w_kv, b_w_in, b_lambda_q1, b_lambda_k1, b_lambda_q2, b_lambda_k2, b_subln_g,
           ffn_w_gate, ffn_w_up, ffn_w_down, moe_w_router, moe_w_gate, moe_w_up, moe_w_down):
    bsz, seq, d = x.shape
    n = bsz * seq
    t = _tiles(n, seq)
    xf = x.reshape(n, d)
    row = lambda v: v.reshape(1, -1)

    memkv = _memkv(mem, w_mem_kv.astype(BF16))

    mixed = _mixer_a(xf, a_w_in[0].astype(BF16), row(a_vnorm_g[0]), row(a_vnorm_b[0]),
                     a_w_s[0], jnp.transpose(a_b_s[0]), memkv, seq, t["mixer"])
    x1, x1b = _proj_ln(mixed, 0, mixed, MAIN_WIDTH // MEM_WIDTH, w_mix_out[0].astype(BF16), xf,
                       row(ln_g[0, 0]), row(ln_b[0, 0]), t["proj"], with_bf16=True)
    x2, x2b = _ffn_ln(x1, x1b, ffn_w_gate[0].astype(BF16), ffn_w_up[0].astype(BF16),
                      ffn_w_down[0].astype(BF16), row(ln_g[0, 1]), row(ln_b[0, 1]), t["ffn"], t["ffn_f"])

    w_cat = jnp.concatenate([b_w_in[0], shared_w_kv[:, :MAIN_WIDTH]], axis=1).astype(BF16)
    col_scale = jnp.concatenate([jnp.full((MAIN_WIDTH,), QK_DIM ** -0.5, F32),
                                 jnp.ones((w_cat.shape[1] - MAIN_WIDTH,), F32)]).reshape(1, -1)
    hk = _matmul_colscale(x2b, w_cat, col_scale, t["mm"], t["mm_n"])
    wv_t = jnp.transpose(shared_w_kv[:, MAIN_WIDTH:]).reshape(DIFF_HEADS, V_DIM, d)
    wv_t = jnp.pad(wv_t, ((0, 0), (0, ONES_ROWS), (0, 0))).reshape(DIFF_HEADS * V_AUG, d).astype(BF16)
    ones_bias = jnp.pad(jnp.zeros((DIFF_HEADS, V_DIM, 1), F32), ((0, 0), (0, ONES_ROWS), (0, 0)),
                        constant_values=1.0).reshape(DIFF_HEADS * V_AUG, 1)
    vt = _proj_transposed(x2b, wv_t, ones_bias, t["mm_t"])
    lambda_init = 0.8 - 0.6 * math.exp(-0.3 * 1)
    n_exp, _, d_ff = moe_w_gate[0].shape
    expert_w = [moe_w_gate[0].reshape(n_exp * d, d_ff), moe_w_up[0].reshape(n_exp * d, d_ff),
                moe_w_down[0].reshape(n_exp * d_ff, d)]
    main, wg_bf, wu_bf, wd_bf = _diff_attn(
        hk, vt, row(b_lambda_q1[0]), row(b_lambda_k1[0]), row(b_lambda_q2[0]), row(b_lambda_k2[0]),
        row(b_subln_g[0]), expert_w, bsz, seq, lambda_init, t["attn_q"], t["attn_k"])
    mem_out = _mem_attn_b(hk, memkv, seq, t["mem"])
    (x3,) = _proj_ln(main, 0, mem_out, 0, w_mix_out[1].astype(BF16), x2,
                     row(ln_g[1, 0]), row(ln_b[1, 0]), t["proj"], with_bf16=False)

    moe_tile = t["moe"]
    n_row_tiles = (2 * n) // moe_tile + N_EXPERTS
    w_router_pad = jnp.pad(moe_w_router[0], ((0, 0), (0, LANES - N_EXPERTS)))
    pos, gates, tile_info = _route(x3, w_router_pad, t["route"], moe_tile)
    tile_e = tile_info[0:8].reshape(-1)[:n_row_tiles]
    n_valid = tile_info[8, 0:1]
    tc = t["comb"]
    pos_flat = pos[:, :2].reshape(n // tc, 1, 2 * tc)
    fill_rows = jnp.concatenate([tile_info[16, :N_EXPERTS], n_valid * moe_tile])
    xs = _dispatch(fill_rows, pos_flat, x3, n_row_tiles * moe_tile, moe_tile, tc)
    ys = _experts(tile_e, n_valid, xs, wg_bf.reshape(n_exp, d, d_ff), wu_bf.reshape(n_exp, d, d_ff),
                  wd_bf.reshape(n_exp, d_ff, d), moe_tile, t["moe_f"])
    x4 = _combine(pos_flat, gates, x3, row(ln_g[1, 1]), row(ln_b[1, 1]), ys, tc)
    return x4.reshape(bsz, seq, d)
```

```python
import functools
import math

import numpy as np
import jax
import jax.numpy as jnp
from jax import lax
from jax.experimental import pallas as pl
from jax.experimental.pallas import tpu as pltpu

BF16 = jnp.bfloat16
F32 = jnp.float32

D_MODEL = 2048
MEM_WIDTH = 512
MAIN_WIDTH = 1536
MEM_HEADS = 4
MEM_HEAD_DIM = 128
CHUNK = 128
SG_GROUPS = 12
QK_DIM = 128
V_DIM = 256
DIFF_HEADS = 6
N_EXPERTS = 8
DEPTH = 2
ALPHA = (2.0 * DEPTH) ** 0.25
LN_EPS = 1e-5
LANES = 128
SUBLANES = 8
ONES_ROWS = 16
V_AUG = V_DIM + ONES_ROWS
NEG_BIG = -1e30
DMA_LOOP_UNROLL = 8
MIB = 1024 * 1024
V7X_VMEM_MIB = 64
VMEM_LARGE_MIB = V7X_VMEM_MIB - 8
VMEM_MID_MIB = V7X_VMEM_MIB - 16
VMEM_SMALL_MIB = V7X_VMEM_MIB // 2


def _tiles(n_tokens, seq):
    def fit(pref, total):
        t = min(pref, total)
        while total % t:
            t //= 2
        return t
    return dict(
        mixer=fit(512, seq),
        proj=fit(512, n_tokens),
        ffn=fit(512, n_tokens),
        ffn_f=512,
        mm=fit(1024, n_tokens),
        mm_n=1792,
        mm_t=fit(512, n_tokens),
        attn_q=fit(1024, seq),
        attn_k=fit(1024, seq),
        mem=fit(512, seq),
        route=fit(512, n_tokens),
        moe=fit(512, n_tokens),
        moe_f=1024,
        comb=fit(512, n_tokens),
    )


def _params(sem, vmem_mib):
    return pltpu.CompilerParams(dimension_semantics=sem, vmem_limit_bytes=vmem_mib * MIB)


def _const_spec(shape, index_map):
    return pl.BlockSpec(shape, index_map, pipeline_mode=pl.Buffered(1))


def _dot(a, b):
    return jnp.dot(a, b, preferred_element_type=F32)


def _dot_nt(a, b):
    return lax.dot_general(a, b, (((1,), (1,)), ((), ())), preferred_element_type=F32)


def _layer_norm(v, g, b):
    mu = jnp.mean(v, axis=-1, keepdims=True)
    c = v - mu
    var = jnp.mean(c * c, axis=-1, keepdims=True)
    return c * lax.rsqrt(var + LN_EPS) * g + b


def _gelu_tanh(v):
    return 0.5 * v * (1.0 + jnp.tanh(math.sqrt(2.0 / math.pi) * (v + 0.044715 * (v * v * v))))


def _silu(v):
    return v / (1.0 + jnp.exp(-v))


def _mem_attention(q, kv_ref, o_ref, col0):
    scale = MEM_HEAD_DIM ** -0.5
    for h in range(MEM_HEADS):
        lo = h * MEM_HEAD_DIM
        qh = q[:, lo:lo + MEM_HEAD_DIM].astype(BF16)
        kh = kv_ref[:, lo:lo + MEM_HEAD_DIM]
        vh = kv_ref[:, MEM_WIDTH + lo:MEM_WIDTH + lo + MEM_HEAD_DIM]
        s = _dot_nt(qh, kh) * scale
        e = jnp.exp(s - jnp.max(s, axis=-1, keepdims=True))
        o = _dot(e.astype(BF16), vh) / jnp.sum(e, axis=-1, keepdims=True)
        o_ref[:, col0 + lo:col0 + lo + MEM_HEAD_DIM] = o.astype(o_ref.dtype)


def _memkv_kernel(mem_ref, w_ref, o_ref):
    o_ref[...] = _dot(mem_ref[...].astype(BF16), w_ref[...]).astype(o_ref.dtype)


def _memkv(mem, w_mem_kv_bf):
    bsz, n_mem, d = mem.shape
    depth, _, cols = w_mem_kv_bf.shape
    return pl.pallas_call(
        _memkv_kernel,
        grid=(depth, bsz),
        in_specs=[pl.BlockSpec((None, n_mem, d), lambda l, b: (b, 0, 0)),
                  pl.BlockSpec((None, d, cols), lambda l, b: (l, 0, 0))],
        out_specs=pl.BlockSpec((None, None, n_mem, cols), lambda l, b: (l, b, 0, 0)),
        out_shape=jax.ShapeDtypeStruct((depth, bsz, n_mem, cols), BF16),
        compiler_params=_params(("arbitrary", "arbitrary"), VMEM_SMALL_MIB),
        name="memkv",
    )(mem, w_mem_kv_bf)


def _mixer_a_kernel(x_ref, w_ref, vg_ref, vb_ref, ws_ref, bs_ref, kv_ref, o_ref):
    tm = x_ref.shape[0]
    xb = x_ref[...].astype(BF16)
    v = _gelu_tanh(_dot(xb, w_ref[:, MAIN_WIDTH:2 * MAIN_WIDTH]))
    vn = _layer_norm(v, vg_ref[...], vb_ref[...]).astype(BF16)
    u = _gelu_tanh(_dot(xb, w_ref[:, :MAIN_WIDTH]))
    row = lax.broadcasted_iota(jnp.int32, (CHUNK, CHUNK), 0)
    col = lax.broadcasted_iota(jnp.int32, (CHUNK, CHUNK), 1)
    causal = col <= row
    for g in range(SG_GROUPS):
        wg = jnp.where(causal, ws_ref[g], 0.0).astype(BF16)
        bias = bs_ref[:, g:g + 1]
        for c in range(tm // CHUNK):
            rows = slice(c * CHUNK, (c + 1) * CHUNK)
            cols = slice(g * CHUNK, (g + 1) * CHUNK)
            s = _dot(wg, vn[rows, cols]) + bias
            o_ref[rows, cols] = (u[rows, cols] * s).astype(o_ref.dtype)
    q_mem = _dot(xb, w_ref[:, 2 * MAIN_WIDTH:])
    _mem_attention(q_mem, kv_ref, o_ref, MAIN_WIDTH)


def _mixer_a(x, w_in_bf, vnorm_g, vnorm_b, w_s, b_s_t, memkv, seq, tm):
    n, d = x.shape
    in_cols = w_in_bf.shape[1]
    return pl.pallas_call(
        _mixer_a_kernel,
        grid=(n // tm,),
        in_specs=[pl.BlockSpec((tm, d), lambda i: (i, 0)),
                  _const_spec((d, in_cols), lambda i: (0, 0)),
                  _const_spec((1, MAIN_WIDTH), lambda i: (0, 0)),
                  _const_spec((1, MAIN_WIDTH), lambda i: (0, 0)),
                  _const_spec((SG_GROUPS, CHUNK, CHUNK), lambda i: (0, 0, 0)),
                  _const_spec((CHUNK, SG_GROUPS), lambda i: (0, 0)),
                  pl.BlockSpec((None, None) + memkv.shape[2:], lambda i: (0, (i * tm) // seq, 0, 0))],
        out_specs=pl.BlockSpec((tm, d), lambda i: (i, 0)),
        out_shape=jax.ShapeDtypeStruct((n, d), BF16),
        compiler_params=_params(("arbitrary",), VMEM_LARGE_MIB),
        name="mixer_a",
    )(x, w_in_bf, vnorm_g, vnorm_b, w_s, b_s_t, memkv)


def _proj_ln_kernel(a1_ref, a2_ref, w_ref, x_ref, g_ref, b_ref, o32_ref, *maybe_o16_ref, sub):
    for r in range(0, x_ref.shape[0], sub):
        rows = slice(r, r + sub)
        mix = _dot(a1_ref[rows, :], w_ref[:MAIN_WIDTH, :]) + _dot(a2_ref[rows, :], w_ref[MAIN_WIDTH:, :])
        y = _layer_norm(ALPHA * x_ref[rows, :] + mix, g_ref[...], b_ref[...])
        o32_ref[rows, :] = y
        for o16_ref in maybe_o16_ref:
            o16_ref[rows, :] = y.astype(BF16)


def _proj_ln(a_main, main_blk, a_mem, mem_blk, w_bf, x, g, b, tm, with_bf16):
    n, d = x.shape
    n_out = 2 if with_bf16 else 1
    return pl.pallas_call(
        functools.partial(_proj_ln_kernel, sub=min(tm, 2 * LANES)),
        grid=(n // tm,),
        in_specs=[pl.BlockSpec((tm, MAIN_WIDTH), lambda i: (i, main_blk)),
                  pl.BlockSpec((tm, MEM_WIDTH), lambda i: (i, mem_blk)),
                  _const_spec((d, d), lambda i: (0, 0)),
                  pl.BlockSpec((tm, d), lambda i: (i, 0)),
                  _const_spec((1, d), lambda i: (0, 0)),
                  _const_spec((1, d), lambda i: (0, 0))],
        out_specs=[pl.BlockSpec((tm, d), lambda i: (i, 0))] * n_out,
        out_shape=[jax.ShapeDtypeStruct((n, d), F32), jax.ShapeDtypeStruct((n, d), BF16)][:n_out],
        compiler_params=_params(("arbitrary",), VMEM_MID_MIB),
        name="proj_ln",
    )(a_main, a_mem, w_bf, x, g, b)


def _ffn_ln_kernel(x_ref, xb_ref, wg_ref, wu_ref, wd_ref, g_ref, b_ref, o32_ref, o16_ref, acc_ref):
    j = pl.program_id(1)

    @pl.when(j == 0)
    def _():
        acc_ref[...] = jnp.zeros_like(acc_ref)

    xb = xb_ref[...]
    h = (_silu(_dot(xb, wg_ref[...])) * _dot(xb, wu_ref[...])).astype(BF16)
    acc_ref[...] += _dot(h, wd_ref[...])

    @pl.when(j == pl.num_programs(1) - 1)
    def _():
        y = _layer_norm(ALPHA * x_ref[...] + acc_ref[...], g_ref[...], b_ref[...])
        o32_ref[...] = y
        o16_ref[...] = y.astype(BF16)


def _ffn_ln(x, xb, wg_bf, wu_bf, wd_bf, g, b, tm, tf):
    n, d = x.shape
    f = wg_bf.shape[1]
    return pl.pallas_call(
        _ffn_ln_kernel,
        grid=(n // tm, f // tf),
        in_specs=[pl.BlockSpec((tm, d), lambda i, j: (i, 0)),
                  pl.BlockSpec((tm, d), lambda i, j: (i, 0)),
                  pl.BlockSpec((d, tf), lambda i, j: (0, j)),
                  pl.BlockSpec((d, tf), lambda i, j: (0, j)),
                  pl.BlockSpec((tf, d), lambda i, j: (j, 0)),
                  _const_spec((1, d), lambda i, j: (0, 0)),
                  _const_spec((1, d), lambda i, j: (0, 0))],
        out_specs=[pl.BlockSpec((tm, d), lambda i, j: (i, 0)),
                   pl.BlockSpec((tm, d), lambda i, j: (i, 0))],
        out_shape=[jax.ShapeDtypeStruct((n, d), F32), jax.ShapeDtypeStruct((n, d), BF16)],
        scratch_shapes=[pltpu.VMEM((tm, d), F32)],
        compiler_params=_params(("arbitrary", "arbitrary"), VMEM_LARGE_MIB),
        name="ffn_ln",
    )(x, xb, wg_bf, wu_bf, wd_bf, g, b)


def _matmul_kernel(x_ref, w_ref, s_ref, o_ref):
    o_ref[...] = (_dot(x_ref[...], w_ref[...]) * s_ref[...]).astype(o_ref.dtype)


def _matmul_colscale(xb, w_bf, col_scale, tm, tn):
    n, d = xb.shape
    cols = w_bf.shape[1]
    return pl.pallas_call(
        _matmul_kernel,
        grid=(n // tm, cols // tn),
        in_specs=[pl.BlockSpec((tm, d), lambda i, j: (i, 0)),
                  pl.BlockSpec((d, tn), lambda i, j: (0, j)),
                  pl.BlockSpec((1, tn), lambda i, j: (0, j))],
        out_specs=pl.BlockSpec((tm, tn), lambda i, j: (i, j)),
        out_shape=jax.ShapeDtypeStruct((n, cols), BF16),
        compiler_params=_params(("arbitrary", "arbitrary"), VMEM_MID_MIB),
        name="in_proj_b",
    )(xb, w_bf, col_scale)


def _proj_t_kernel(wt_ref, bias_ref, x_ref, o_ref):
    o_ref[...] = (_dot_nt(wt_ref[...], x_ref[...]) + bias_ref[...]).astype(o_ref.dtype)


def _proj_transposed(xb, wt_bf, bias_col, tn):
    n, d = xb.shape
    cols = wt_bf.shape[0]
    return pl.pallas_call(
        _proj_t_kernel,
        grid=(n // tn,),
        in_specs=[_const_spec((cols, d), lambda i: (0, 0)),
                  _const_spec((cols, 1), lambda i: (0, 0)),
                  pl.BlockSpec((tn, d), lambda i: (i, 0))],
        out_specs=pl.BlockSpec((cols, tn), lambda i: (0, i)),
        out_shape=jax.ShapeDtypeStruct((cols, n), BF16),
        compiler_params=_params(("arbitrary",), VMEM_MID_MIB),
        name="v_proj_t",
    )(wt_bf, bias_col, xb)


def _diff_attn_kernel(qi_tab, kj_tab, q1_ref, q2_ref, k1_ref, k2_ref, vt_ref,
                      lq1_ref, lk1_ref, lq2_ref, lk2_ref, sg_ref, *rest, lambda_init, qw, n_cast):
    cast_src = rest[:n_cast]
    o_ref = rest[n_cast]
    cast_dst = rest[n_cast + 1:2 * n_cast + 1]
    m1_ref, a1_ref, m2_ref, a2_ref = rest[2 * n_cast + 1:]
    tq, tk = q1_ref.shape[0], k1_ref.shape[0]
    p = pl.program_id(2)
    qi = qi_tab[p]
    kj = kj_tab[p]

    @pl.when(kj == 0)
    def _():
        for m_ref, a_ref in ((m1_ref, a1_ref), (m2_ref, a2_ref)):
            m_ref[...] = jnp.full_like(m_ref, NEG_BIG)
            a_ref[...] = jnp.zeros_like(a_ref)

    chains = [(q_ref, k_ref, m_ref, a_ref, j)
              for j in range(tq // qw)
              for q_ref, k_ref, m_ref, a_ref in ((q1_ref, k1_ref, m1_ref, a1_ref),
                                                 (q2_ref, k2_ref, m2_ref, a2_ref))]

    def step(key0):
        if key0 is None:
            live = chains
        else:
            live = [c for c in chains if key0 <= (c[4] + 1) * qw - 1]

        def n_keys(j):
            return tk if key0 is None else min(tk, (j + 1) * qw - key0)

        def scores(chain):
            q_ref, k_ref, j = chain[0], chain[1], chain[4]
            nk = n_keys(j)
            s = _dot_nt(k_ref[0:nk, :], q_ref[j * qw:(j + 1) * qw, :])
            if key0 is not None and key0 + nk - 1 > j * qw:
                key = key0 + lax.broadcasted_iota(jnp.int32, (nk, qw), 0)
                qry = j * qw + lax.broadcasted_iota(jnp.int32, (nk, qw), 1)
                s = jnp.where(key <= qry, s, NEG_BIG)
            return s.astype(BF16)

        def softmax(chain, s):
            m_ref, j = chain[2], chain[4]
            cols = slice(j * qw, (j + 1) * qw)
            m_old = m_ref[:, cols]
            m_new = jnp.maximum(m_old, jnp.max(s, axis=0, keepdims=True).astype(F32))
            alpha = jnp.exp(m_old - m_new)
            e = jnp.exp(s - m_new.astype(BF16))
            m_ref[:, cols] = m_new
            return e, alpha

        def weighted_values(chain, e, alpha):
            a_ref, j = chain[3], chain[4]
            cols = slice(j * qw, (j + 1) * qw)
            a_ref[:, cols] = alpha * a_ref[:, cols] + _dot(vt_ref[:, 0:n_keys(j)], e)

        n = len(live)
        s_live, e_live = {}, {}
        for t in range(n + 2):
            if t < len(cast_src):
                cast_dst[t][...] = cast_src[t][...].astype(BF16)
            if t < n:
                s_live[t] = scores(live[t])
            if 0 <= t - 1 < n:
                e_live[t - 1] = softmax(live[t - 1], s_live.pop(t - 1))
            if 0 <= t - 2 < n:
                weighted_values(live[t - 2], *e_live.pop(t - 2))

    key0 = kj * tk - qi * tq
    pl.when(key0 < 0)(lambda: step(None))
    for static_key0 in range(0, tq, tk):
        pl.when(key0 == static_key0)(functools.partial(step, static_key0))

    @pl.when(kj == ((qi + 1) * tq - 1) // tk)
    def _():
        lam = (jnp.exp(jnp.sum(lq1_ref[...] * lk1_ref[...], axis=-1, keepdims=True))
               - jnp.exp(jnp.sum(lq2_ref[...] * lk2_ref[...], axis=-1, keepdims=True))
               + lambda_init)
        o1 = a1_ref[0:V_DIM, :] / a1_ref[V_DIM:V_DIM + 1, :]
        o2 = a2_ref[0:V_DIM, :] / a2_ref[V_DIM:V_DIM + 1, :]
        o = o1 - lam * o2
        o = o * lax.rsqrt(jnp.mean(o * o, axis=0, keepdims=True) + LN_EPS) * sg_ref[...]
        o_ref[...] = jnp.transpose(o * (1.0 - lambda_init)).astype(o_ref.dtype)


def _diff_attn(hk, vt, lq1, lk1, lq2, lk2, subln_g, to_cast, bsz, seq, lambda_init, tq, tk):
    n = hk.shape[0]
    assert tq % tk == 0 and tk >= 2
    nq = seq // tq
    pairs =[(qi, kj) for qi in range(nq) for kj in range(((qi + 1) * tq - 1) // tk + 1)]
    qi_tab = jnp.asarray(np.array([p[0] for p in pairs], np.int32))
    kj_tab = jnp.asarray(np.array([p[1] for p in pairs], np.int32))
    nqb, nkb = seq // tq, seq // tk
    n_pairs = len(pairs)
    n_steps = bsz * DIFF_HEADS * n_pairs
    k_blk0 = (MAIN_WIDTH + MEM_WIDTH) // QK_DIM
    qspec = lambda off: pl.BlockSpec((tq, QK_DIM), lambda b, h, p, qt, kt: (b * nqb + qt[p], off + h))
    kspec = lambda off: pl.BlockSpec((tk, QK_DIM), lambda b, h, p, qt, kt: (b * nkb + kt[p], off + h))
    vec = lambda w: pl.BlockSpec((1, w), lambda b, h, p, qt, kt: (0, 0))

    def cast_spec(arr):
        rows, cols = arr.shape
        n_blk = 1
        while n_blk * 2 <= n_steps and rows % (n_blk * 2) == 0 and (rows // (n_blk * 2)) % 16 == 0:
            n_blk *= 2
        return pl.BlockSpec((rows // n_blk, cols), lambda b, h, p, qt, kt: (
            jnp.minimum((b * DIFF_HEADS + h) * n_pairs + p, n_blk - 1), 0))

    cast_specs = [cast_spec(a) for a in to_cast]
    grid_spec = pltpu.PrefetchScalarGridSpec(
        num_scalar_prefetch=2,
        grid=(bsz, DIFF_HEADS, n_pairs),
        in_specs=[qspec(0), qspec(DIFF_HEADS), kspec(k_blk0), kspec(k_blk0 + DIFF_HEADS),
                  pl.BlockSpec((V_AUG, tk), lambda b, h, p, qt, kt: (h, b * nkb + kt[p])),
                  vec(QK_DIM), vec(QK_DIM), vec(QK_DIM), vec(QK_DIM),
                  pl.BlockSpec((V_DIM, 1), lambda b, h, p, qt, kt: (0, 0))] + cast_specs,
        out_specs=[pl.BlockSpec((tq, V_DIM), lambda b, h, p, qt, kt: (b * nqb + qt[p], h))] + cast_specs,
        scratch_shapes=[pltpu.VMEM((1, tq), F32), pltpu.VMEM((V_AUG, tq), F32),
                        pltpu.VMEM((1, tq), F32), pltpu.VMEM((V_AUG, tq), F32)],
    )
    return pl.pallas_call(
        functools.partial(_diff_attn_kernel, lambda_init=lambda_init, qw=min(tq, 4 * LANES),
                          n_cast=len(to_cast)),
        grid_spec=grid_spec,
        out_shape=[jax.ShapeDtypeStruct((n, MAIN_WIDTH), BF16)]
                  + [jax.ShapeDtypeStruct(a.shape, BF16) for a in to_cast],
        compiler_params=_params(("arbitrary", "arbitrary", "arbitrary"), VMEM_MID_MIB),
        name="diff_attn",
    )(qi_tab, kj_tab, hk, hk, hk, hk, vt, lq1, lk1, lq2, lk2, subln_g.reshape(V_DIM, 1), *to_cast)


def _mem_attn_kernel(q_ref, kv_ref, o_ref):
    _mem_attention(q_ref[...].astype(F32), kv_ref, o_ref, 0)


def _mem_attn_b(hk, memkv, seq, tm):
    n = hk.shape[0]
    return pl.pallas_call(
        _mem_attn_kernel,
        grid=(n // tm,),
        in_specs=[pl.BlockSpec((tm, MEM_WIDTH), lambda i: (i, MAIN_WIDTH // MEM_WIDTH)),
                  pl.BlockSpec((None, None) + memkv.shape[2:], lambda i: (1, (i * tm) // seq, 0, 0))],
        out_specs=pl.BlockSpec((tm, MEM_WIDTH), lambda i: (i, 0)),
        out_shape=jax.ShapeDtypeStruct((n, MEM_WIDTH), BF16),
        compiler_params=_params(("arbitrary",), VMEM_SMALL_MIB),
        name="mem_attn_b",
    )(hk, memkv)


def _lane_cumsum(v, lane):
    for shift in (1, 2, 4):
        v = v + jnp.where(lane >= shift, pltpu.roll(v, shift, 1), 0.0)
    return v


def _route_kernel(x_ref, wr_ref, pos_ref, gate_ref, tile_e_ref, cnt_ref, base_ref, start_ref, top2_ref,
                  *, moe_tile):
    phase = pl.program_id(0)
    i = pl.program_id(1)
    tm = x_ref.shape[0]
    lane = lax.broadcasted_iota(jnp.int32, (tm, LANES), 1).astype(F32)

    @pl.when(jnp.logical_and(phase == 0, i == 0))
    def _():
        cnt_ref[...] = jnp.zeros_like(cnt_ref)

    @pl.when(phase == 0)
    def _():
        x = x_ref[...]
        xh = x.astype(BF16)
        xl = (x - xh.astype(F32)).astype(BF16)
        w = wr_ref[...]
        wh = w.astype(BF16)
        wl = (w - wh.astype(F32)).astype(BF16)
        logits = _dot(xh, wh) + (_dot(xh, wl) + _dot(xl, wh))
        logits = jnp.where(lane < N_EXPERTS, logits, NEG_BIG)
        v1 = jnp.max(logits, axis=-1, keepdims=True)
        i1 = jnp.min(jnp.where(logits == v1, lane, float(LANES)), axis=-1, keepdims=True)
        rest = jnp.where(lane == i1, NEG_BIG, logits)
        v2 = jnp.max(rest, axis=-1, keepdims=True)
        i2 = jnp.min(jnp.where(rest == v2, lane, float(LANES)), axis=-1, keepdims=True)
        top2_ref[i] = jnp.where(lane == 0, i1, jnp.where(lane == 1, i2, jnp.where(lane == 2, v1, v2)))
        sel = jnp.logical_or(lane == i1, lane == i2)
        cnt_ref[...] += jnp.sum(sel.astype(F32), axis=0, keepdims=True)

    @pl.when(jnp.logical_and(phase == 1, i == 0))
    def _():
        lane8 = lax.broadcasted_iota(jnp.int32, (8, LANES), 1).astype(F32)
        sub8 = lax.broadcasted_iota(jnp.int32, (8, LANES), 0).astype(F32)
        cnt = jnp.broadcast_to(cnt_ref[...], (8, LANES))
        padded = jnp.ceil(cnt * (1.0 / moe_tile)) * moe_tile
        ends = _lane_cumsum(padded, lane8)
        start_ref[...] = (ends - padded)[0:1, :]
        base_ref[...] = jnp.zeros_like(base_ref)
        tile_row0 = (sub8 * LANES + lane8) * moe_tile
        tile_e = jnp.zeros((8, LANES), F32)
        for e in range(N_EXPERTS):
            end_e = jnp.sum(jnp.where(lane8 == e, ends, 0.0), axis=-1, keepdims=True)
            tile_e = tile_e + (tile_row0 >= end_e).astype(F32)
        tile_e_ref[0:8, :] = jnp.minimum(tile_e, N_EXPERTS - 1.0).astype(jnp.int32)
        total = jnp.sum(jnp.where(lane8 == N_EXPERTS - 1, ends, 0.0), axis=-1, keepdims=True)
        tile_e_ref[8:16, :] = jnp.broadcast_to(total * (1.0 / moe_tile), (8, LANES)).astype(jnp.int32)
        tile_e_ref[16:24, :] = (ends - padded + cnt).astype(jnp.int32)

    @pl.when(phase == 1)
    def _():
        rec = top2_ref[i]
        i1, i2, v1, v2 = rec[:, 0:1], rec[:, 1:2], rec[:, 2:3], rec[:, 3:4]
        sel = jnp.logical_or(lane == i1, lane == i2)
        tile_cnt = jnp.sum(sel.astype(F32), axis=0, keepdims=True)
        r = lax.broadcasted_iota(jnp.int32, (tm, tm), 0)
        c = lax.broadcasted_iota(jnp.int32, (tm, tm), 1)
        before = (c < r).astype(BF16)
        rank = _dot(before, sel.astype(BF16))
        slot = start_ref[...] + base_ref[...] + rank
        p1 = jnp.sum(jnp.where(lane == i1, slot, 0.0), axis=-1, keepdims=True)
        p2 = jnp.sum(jnp.where(lane == i2, slot, 0.0), axis=-1, keepdims=True)
        pos_ref[...] = jnp.where(lane == 0, p1, jnp.where(lane == 1, p2, 0.0)).astype(jnp.int32)
        g1 = 1.0 / (1.0 + jnp.exp(v2 - v1))
        g2 = jnp.exp(v2 - v1) * g1
        gate_ref[...] = jnp.where(lane == 0, g1, jnp.where(lane == 1, g2, 0.0))
        base_ref[...] += tile_cnt


def _route(x, w_router_pad, tm, moe_tile):
    n, d = x.shape
    nt = n // tm
    return pl.pallas_call(
        functools.partial(_route_kernel, moe_tile=moe_tile),
        grid=(2, nt),
        in_specs=[pl.BlockSpec((tm, d), lambda ph, i: (i * (1 - ph) + (nt - 1) * ph, 0)),
                  _const_spec((d, LANES), lambda ph, i: (0, 0))],
        out_specs=[pl.BlockSpec((tm, LANES), lambda ph, i: (i * ph, 0)),
                   pl.BlockSpec((tm, LANES), lambda ph, i: (i * ph, 0)),
                   pl.BlockSpec((24, LANES), lambda ph, i: (0, 0))],
        out_shape=[jax.ShapeDtypeStruct((n, LANES), jnp.int32),
                   jax.ShapeDtypeStruct((n, LANES), F32),
                   jax.ShapeDtypeStruct((24, LANES), jnp.int32)],
        scratch_shapes=[pltpu.VMEM((1, LANES), F32), pltpu.VMEM((1, LANES), F32), pltpu.VMEM((1, LANES), F32),
                        pltpu.VMEM((nt, tm, LANES), F32)],
        compiler_params=_params(("arbitrary", "arbitrary"), VMEM_MID_MIB),
        name="route",
    )(x, w_router_pad)


def _dispatch_kernel(pad_row0, pos_ref, x_ref, xs_ref, zero_ref, sem, *, moe_tile):
    tm = x_ref.shape[0] * SUBLANES

    @pl.when(pl.program_id(0) == 0)
    def _():
        zero_ref[...] = jnp.zeros_like(zero_ref)
        zr = zero_ref.shape[0]

        def drain_fill(r, carry):
            pltpu.make_async_copy(zero_ref.at[pl.ds(0, 1)], xs_ref.at[pl.ds(0, 1)], sem).wait()
            return carry

        for e in range(N_EXPERTS):
            def fill(r, carry, e=e):
                pltpu.make_async_copy(zero_ref.at[pl.ds(0, 1)],
                                      xs_ref.at[pl.ds(pad_row0[e] + r, 1)], sem).start()
                return carry

            lax.fori_loop(0, moe_tile, fill, 0, unroll=DMA_LOOP_UNROLL)
            lax.fori_loop(0, moe_tile, drain_fill, 0, unroll=2 * DMA_LOOP_UNROLL)

        n_alloc = xs_ref.shape[0]
        tail = [pltpu.make_async_copy(
                    zero_ref,
                    xs_ref.at[pl.ds(pl.multiple_of(
                        jnp.minimum(pad_row0[N_EXPERTS] + c * zr, n_alloc - zr), zr), zr)], sem)
                for c in range((N_EXPERTS + 1) * moe_tile // zr)]
        for f in tail:
            f.start()
            f.wait()

    def issue(g, carry):
        for u in range(SUBLANES):
            for k in range(2):
                dst = pos_ref[0, 0, 2 * SUBLANES * g + 2 * u + k]
                pltpu.make_async_copy(x_ref.at[g, pl.ds(u, 1)], xs_ref.at[pl.ds(dst, 1)], sem).start(priority=k)
        return carry

    lax.fori_loop(0, tm // SUBLANES, issue, 0)

    def drain(t, carry):
        pltpu.make_async_copy(x_ref.at[0, pl.ds(0, 1)], xs_ref.at[pl.ds(0, 1)], sem).wait()
        return carry

    lax.fori_loop(0, 2 * tm, drain, 0, unroll=2 * DMA_LOOP_UNROLL)


def _dispatch(pad_row0, pos_flat, x, n_rows, moe_tile, tm):
    n, d = x.shape
    zero_rows = min(moe_tile, LANES)
    grid_spec = pltpu.PrefetchScalarGridSpec(
        num_scalar_prefetch=1,
        grid=(n // tm,),
        in_specs=[pl.BlockSpec((1, 1, 2 * tm), lambda i, pr: (i, 0, 0), memory_space=pltpu.SMEM),
                  pl.BlockSpec((tm // SUBLANES, SUBLANES, d), lambda i, pr: (i, 0, 0))],
        out_specs=pl.BlockSpec(memory_space=pl.ANY),
        scratch_shapes=[pltpu.VMEM((zero_rows, d), F32), pltpu.SemaphoreType.DMA(())],
    )
    return pl.pallas_call(
        functools.partial(_dispatch_kernel, moe_tile=moe_tile),
        grid_spec=grid_spec,
        out_shape=jax.ShapeDtypeStruct((n_rows + moe_tile, d), F32),
        compiler_params=_params(("arbitrary",), VMEM_SMALL_MIB),
        name="moe_dispatch",
    )(pad_row0, pos_flat, x.reshape(n // SUBLANES, SUBLANES, d))


def _experts_kernel(tile_e, n_valid, xs_ref, wg_ref, wu_ref, wd_ref, o_ref, xb_ref):
    del tile_e
    i = pl.program_id(0)
    j = pl.program_id(1)
    valid = i < n_valid[0]

    @pl.when(j == 0)
    def _():
        xb_ref[...] = xs_ref[...].astype(BF16)
        o_ref[...] = jnp.zeros_like(o_ref)

    @pl.when(valid)
    def _():
        xb = xb_ref[...]
        h = (_silu(_dot(xb, wg_ref[...])) * _dot(xb, wu_ref[...])).astype(BF16)
        o_ref[...] += _dot(h, wd_ref[...])


def _experts(tile_e, n_valid, xs, wg_bf, wu_bf, wd_bf, tm, tf):
    d = xs.shape[1]
    m = tile_e.shape[0] * tm
    f = wg_bf.shape[2]
    nj = f // tf

    def row_idx(i, j, te, nv):
        return (jnp.minimum(i, nv[0] - 1), 0)

    def col_of(i, j, nv):
        return jnp.where(i < nv[0], j, nj - 1)

    grid_spec = pltpu.PrefetchScalarGridSpec(
        num_scalar_prefetch=2,
        grid=(m // tm, nj),
        in_specs=[pl.BlockSpec((tm, d), row_idx),
                  pl.BlockSpec((None, d, tf), lambda i, j, te, nv: (te[i], 0, col_of(i, j, nv))),
                  pl.BlockSpec((None, d, tf), lambda i, j, te, nv: (te[i], 0, col_of(i, j, nv))),
                  pl.BlockSpec((None, tf, d), lambda i, j, te, nv: (te[i], col_of(i, j, nv), 0))],
        out_specs=pl.BlockSpec((tm, d), lambda i, j, te, nv: (i, 0)),
        scratch_shapes=[pltpu.VMEM((tm, d), BF16)],
    )
    return pl.pallas_call(
        _experts_kernel,
        grid_spec=grid_spec,
        out_shape=jax.ShapeDtypeStruct((m, d), F32),
        compiler_params=_params(("arbitrary", "arbitrary"), VMEM_LARGE_MIB),
        name="moe_experts",
    )(tile_e, n_valid, xs, wg_bf, wu_bf, wd_bf)


def _combine_kernel(pos_ref, gate_ref, x_ref, g_ref, b_ref, ys_ref, o_ref, buf_ref, sem):
    tm = x_ref.shape[0]

    def issue(g, carry):
        for u in range(SUBLANES):
            for k in range(2):
                src = pos_ref[0, 0, 2 * SUBLANES * g + 2 * u + k]
                pltpu.make_async_copy(ys_ref.at[pl.ds(src, 1)], buf_ref.at[k, g, pl.ds(u, 1)],
                                      sem).start(priority=k)
        return carry

    lax.fori_loop(0, tm // SUBLANES, issue, 0)

    def drain(t, carry):
        pltpu.make_async_copy(ys_ref.at[pl.ds(0, 1)], buf_ref.at[0, 0, pl.ds(0, 1)], sem).wait()
        return carry

    lax.fori_loop(0, 2 * tm, drain, 0, unroll=2 * DMA_LOOP_UNROLL)

    gates = gate_ref[...]
    d = x_ref.shape[1]
    y = gates[:, 0:1] * buf_ref[0].reshape(tm, d) + gates[:, 1:2] * buf_ref[1].reshape(tm, d)
    o_ref[...] = _layer_norm(ALPHA * x_ref[...] + y, g_ref[...], b_ref[...])


def _combine(pos_flat, gates, x, g, b, ys, tm):
    n, d = x.shape
    return pl.pallas_call(
        _combine_kernel,
        grid=(n // tm,),
        in_specs=[pl.BlockSpec((1, 1, 2 * tm), lambda i: (i, 0, 0), memory_space=pltpu.SMEM),
                  pl.BlockSpec((tm, LANES), lambda i: (i, 0)),
                  pl.BlockSpec((tm, d), lambda i: (i, 0)),
                  _const_spec((1, d), lambda i: (0, 0)),
                  _const_spec((1, d), lambda i: (0, 0)),
                  pl.BlockSpec(memory_space=pl.ANY)],
        out_specs=pl.BlockSpec((tm, d), lambda i: (i, 0)),
        out_shape=jax.ShapeDtypeStruct((n, d), F32),
        scratch_shapes=[pltpu.VMEM((2, tm // SUBLANES, SUBLANES, d), F32), pltpu.SemaphoreType.DMA(())],
        compiler_params=_params(("arbitrary",), VMEM_SMALL_MIB),
        name="moe_combine",
    )(pos_flat, gates, x, g, b, ys)


def kernel(x, mem, ln_g, ln_b, w_mix_out, w_mem_kv, a_w_in, a_vnorm_g, a_vnorm_b, a_w_s, a_b_s,
           shared_w_kv, b_w_in, b_lambda_q1, b_lambda_k1, b_lambda_q2, b_lambda_k2, b_subln_g,
           ffn_w_gate, ffn_w_up, ffn_w_down, moe_w_router, moe_w_gate, moe_w_up, moe_w_down):
    bsz, seq, d = x.shape
    n = bsz * seq
    t = _tiles(n, seq)
    xf = x.reshape(n, d)
    row = lambda v: v.reshape(1, -1)

    memkv = _memkv(mem, w_mem_kv.astype(BF16))

    mixed = _mixer_a(xf, a_w_in[0].astype(BF16), row(a_vnorm_g[0]), row(a_vnorm_b[0]),
                     a_w_s[0], jnp.transpose(a_b_s[0]), memkv, seq, t["mixer"])
    x1, x1b = _proj_ln(mixed, 0, mixed, MAIN_WIDTH // MEM_WIDTH, w_mix_out[0].astype(BF16), xf,
                       row(ln_g[0, 0]), row(ln_b[0, 0]), t["proj"], with_bf16=True)
    x2, x2b = _ffn_ln(x1, x1b, ffn_w_gate[0].astype(BF16), ffn_w_up[0].astype(BF16),
                      ffn_w_down[0].astype(BF16), row(ln_g[0, 1]), row(ln_b[0, 1]), t["ffn"], t["ffn_f"])

    w_cat = jnp.concatenate([b_w_in[0], shared_w_kv[:, :MAIN_WIDTH]], axis=1).astype(BF16)
    col_scale = jnp.concatenate([jnp.full((MAIN_WIDTH,), QK_DIM ** -0.5, F32),
                                 jnp.ones((w_cat.shape[1] - MAIN_WIDTH,), F32)]).reshape(1, -1)
    hk = _matmul_colscale(x2b, w_cat, col_scale, t["mm"], t["mm_n"])
    wv_t = jnp.transpose(shared_w_kv[:, MAIN_WIDTH:]).reshape(DIFF_HEADS, V_DIM, d)
    wv_t = jnp.pad(wv_t, ((0, 0), (0, ONES_ROWS), (0, 0))).reshape(DIFF_HEADS * V_AUG, d).astype(BF16)
    ones_bias = jnp.pad(jnp.zeros((DIFF_HEADS, V_DIM, 1), F32), ((0, 0), (0, ONES_ROWS), (0, 0)),
                        constant_values=1.0).reshape(DIFF_HEADS * V_AUG, 1)
    vt = _proj_transposed(x2b, wv_t, ones_bias, t["mm_t"])
    lambda_init = 0.8 - 0.6 * math.exp(-0.3 * 1)
    n_exp, _, d_ff = moe_w_gate[0].shape
    expert_w = [moe_w_gate[0].reshape(n_exp * d, d_ff), moe_w_up[0].reshape(n_exp * d, d_ff),
                moe_w_down[0].reshape(n_exp * d_ff, d)]
    main, wg_bf, wu_bf, wd_bf = _diff_attn(
        hk, vt, row(b_lambda_q1[0]), row(b_lambda_k1[0]), row(b_lambda_q2[0]), row(b_lambda_k2[0]),
        row(b_subln_g[0]), expert_w, bsz, seq, lambda_init, t["attn_q"], t["attn_k"])
    mem_out = _mem_attn_b(hk, memkv, seq, t["mem"])
    (x3,) = _proj_ln(main, 0, mem_out, 0, w_mix_out[1].astype(BF16), x2,
                     row(ln_g[1, 0]), row(ln_b[1, 0]), t["proj"], with_bf16=False)

    moe_tile = t["moe"]
    n_row_tiles = (2 * n) // moe_tile + N_EXPERTS
    w_router_pad = jnp.pad(moe_w_router[0], ((0, 0), (0, LANES - N_EXPERTS)))
    pos, gates, tile_info = _route(x3, w_router_pad, t["route"], moe_tile)
    tile_e = tile_info[0:8].reshape(-1)[:n_row_tiles]
    n_valid = tile_info[8, 0:1]
    tc = t["comb"]
    pos_flat = pos[:, :2].reshape(n // tc, 1, 2 * tc)
    fill_rows = jnp.concatenate([tile_info[16, :N_EXPERTS], n_valid * moe_tile])
    xs = _dispatch(fill_rows, pos_flat, x3, n_row_tiles * moe_tile, moe_tile, tc)
    ys = _experts(tile_e, n_valid, xs, wg_bf.reshape(n_exp, d, d_ff), wu_bf.reshape(n_exp, d, d_ff),
                  wd_bf.reshape(n_exp, d_ff, d), moe_tile, t["moe_f"])
    x4 = _combine(pos_flat, gates, x3, row(ln_g[1, 1]), row(ln_b[1, 1]), ys, tc)
    return x4.reshape(bsz, seq, d)
```

```python
import functools
import math

import numpy as np
import jax
import jax.numpy as jnp
from jax import lax
from jax.experimental import pallas as pl
from jax.experimental.pallas import tpu as pltpu

BF16 = jnp.bfloat16
F32 = jnp.float32

D_MODEL = 2048
MEM_WIDTH = 512
MAIN_WIDTH = 1536
MEM_HEADS = 4
MEM_HEAD_DIM = 128
CHUNK = 128
SG_GROUPS = 12
QK_DIM = 128
V_DIM = 256
DIFF_HEADS = 6
N_EXPERTS = 8
DEPTH = 2
ALPHA = (2.0 * DEPTH) ** 0.25
LN_EPS = 1e-5
LANES = 128
SUBLANES = 8
ONES_ROWS = 16
V_AUG = V_DIM + ONES_ROWS
NEG_BIG = -1e30
DMA_LOOP_UNROLL = 8
MIB = 1024 * 1024
V7X_VMEM_MIB = 64
VMEM_LARGE_MIB = V7X_VMEM_MIB - 8
VMEM_MID_MIB = V7X_VMEM_MIB - 16
VMEM_SMALL_MIB = V7X_VMEM_MIB // 2


def _tiles(n_tokens, seq):
    def fit(pref, total):
        t = min(pref, total)
        while total % t:
            t //= 2
        return t
    return dict(
        mixer=fit(512, seq),
        proj=fit(512, n_tokens),
        ffn=fit(512, n_tokens),
        ffn_f=512,
        mm=fit(1024, n_tokens),
        mm_n=1792,
        mm_t=fit(512, n_tokens),
        attn_q=fit(2048, seq),
        attn_k=fit(1024, seq),
        mem=fit(512, seq),
        route=fit(512, n_tokens),
        moe=fit(512, n_tokens),
        moe_f=1024,
        comb=fit(512, n_tokens),
    )


def _params(sem, vmem_mib):
    return pltpu.CompilerParams(dimension_semantics=sem, vmem_limit_bytes=vmem_mib * MIB)


def _const_spec(shape, index_map):
    return pl.BlockSpec(shape, index_map, pipeline_mode=pl.Buffered(1))


def _dot(a, b):
    return jnp.dot(a, b, preferred_element_type=F32)


def _dot_nt(a, b):
    return lax.dot_general(a, b, (((1,), (1,)), ((), ())), preferred_element_type=F32)


def _layer_norm(v, g, b):
    mu = jnp.mean(v, axis=-1, keepdims=True)
    c = v - mu
    var = jnp.mean(c * c, axis=-1, keepdims=True)
    return c * lax.rsqrt(var + LN_EPS) * g + b


def _gelu_tanh(v):
    return 0.5 * v * (1.0 + jnp.tanh(math.sqrt(2.0 / math.pi) * (v + 0.044715 * (v * v * v))))


def _silu(v):
    return v / (1.0 + jnp.exp(-v))


def _mem_attention(q, kv_ref, o_ref, col0):
    scale = MEM_HEAD_DIM ** -0.5
    for h in range(MEM_HEADS):
        lo = h * MEM_HEAD_DIM
        qh = q[:, lo:lo + MEM_HEAD_DIM].astype(BF16)
        kh = kv_ref[:, lo:lo + MEM_HEAD_DIM]
        vh = kv_ref[:, MEM_WIDTH + lo:MEM_WIDTH + lo + MEM_HEAD_DIM]
        s = _dot_nt(qh, kh) * scale
        e = jnp.exp(s - jnp.max(s, axis=-1, keepdims=True))
        o = _dot(e.astype(BF16), vh) / jnp.sum(e, axis=-1, keepdims=True)
        o_ref[:, col0 + lo:col0 + lo + MEM_HEAD_DIM] = o.astype(o_ref.dtype)


def _memkv_kernel(mem_ref, w_ref, o_ref):
    o_ref[...] = _dot(mem_ref[...].astype(BF16), w_ref[...]).astype(o_ref.dtype)


def _memkv(mem, w_mem_kv_bf):
    bsz, n_mem, d = mem.shape
    depth, _, cols = w_mem_kv_bf.shape
    return pl.pallas_call(
        _memkv_kernel,
        grid=(depth, bsz),
        in_specs=[pl.BlockSpec((None, n_mem, d), lambda l, b: (b, 0, 0)),
                  pl.BlockSpec((None, d, cols), lambda l, b: (l, 0, 0))],
        out_specs=pl.BlockSpec((None, None, n_mem, cols), lambda l, b: (l, b, 0, 0)),
        out_shape=jax.ShapeDtypeStruct((depth, bsz, n_mem, cols), BF16),
        compiler_params=_params(("arbitrary", "arbitrary"), VMEM_SMALL_MIB),
        name="memkv",
    )(mem, w_mem_kv_bf)


def _mixer_a_kernel(x_ref, w_ref, vg_ref, vb_ref, ws_ref, bs_ref, kv_ref, o_ref):
    tm = x_ref.shape[0]
    xb = x_ref[...].astype(BF16)
    v = _gelu_tanh(_dot(xb, w_ref[:, MAIN_WIDTH:2 * MAIN_WIDTH]))
    vn = _layer_norm(v, vg_ref[...], vb_ref[...]).astype(BF16)
    u = _gelu_tanh(_dot(xb, w_ref[:, :MAIN_WIDTH]))
    row = lax.broadcasted_iota(jnp.int32, (CHUNK, CHUNK), 0)
    col = lax.broadcasted_iota(jnp.int32, (CHUNK, CHUNK), 1)
    causal = col <= row
    for g in range(SG_GROUPS):
        wg = jnp.where(causal, ws_ref[g], 0.0).astype(BF16)
        bias = bs_ref[:, g:g + 1]
        for c in range(tm // CHUNK):
            rows = slice(c * CHUNK, (c + 1) * CHUNK)
            cols = slice(g * CHUNK, (g + 1) * CHUNK)
            s = _dot(wg, vn[rows, cols]) + bias
            o_ref[rows, cols] = (u[rows, cols] * s).astype(o_ref.dtype)
    q_mem = _dot(xb, w_ref[:, 2 * MAIN_WIDTH:])
    _mem_attention(q_mem, kv_ref, o_ref, MAIN_WIDTH)


def _mixer_a(x, w_in_bf, vnorm_g, vnorm_b, w_s, b_s_t, memkv, seq, tm):
    n, d = x.shape
    in_cols = w_in_bf.shape[1]
    return pl.pallas_call(
        _mixer_a_kernel,
        grid=(n // tm,),
        in_specs=[pl.BlockSpec((tm, d), lambda i: (i, 0)),
                  _const_spec((d, in_cols), lambda i: (0, 0)),
                  _const_spec((1, MAIN_WIDTH), lambda i: (0, 0)),
                  _const_spec((1, MAIN_WIDTH), lambda i: (0, 0)),
                  _const_spec((SG_GROUPS, CHUNK, CHUNK), lambda i: (0, 0, 0)),
                  _const_spec((CHUNK, SG_GROUPS), lambda i: (0, 0)),
                  pl.BlockSpec((None, None) + memkv.shape[2:], lambda i: (0, (i * tm) // seq, 0, 0))],
        out_specs=pl.BlockSpec((tm, d), lambda i: (i, 0)),
        out_shape=jax.ShapeDtypeStruct((n, d), BF16),
        compiler_params=_params(("arbitrary",), VMEM_LARGE_MIB),
        name="mixer_a",
    )(x, w_in_bf, vnorm_g, vnorm_b, w_s, b_s_t, memkv)


def _proj_ln_kernel(a1_ref, a2_ref, w_ref, x_ref, g_ref, b_ref, o32_ref, *maybe_o16_ref, sub):
    for r in range(0, x_ref.shape[0], sub):
        rows = slice(r, r + sub)
        mix = _dot(a1_ref[rows, :], w_ref[:MAIN_WIDTH, :]) + _dot(a2_ref[rows, :], w_ref[MAIN_WIDTH:, :])
        y = _layer_norm(ALPHA * x_ref[rows, :] + mix, g_ref[...], b_ref[...])
        o32_ref[rows, :] = y
        for o16_ref in maybe_o16_ref:
            o16_ref[rows, :] = y.astype(BF16)


def _proj_ln(a_main, main_blk, a_mem, mem_blk, w_bf, x, g, b, tm, with_bf16):
    n, d = x.shape
    n_out = 2 if with_bf16 else 1
    return pl.pallas_call(
        functools.partial(_proj_ln_kernel, sub=min(tm, 2 * LANES)),
        grid=(n // tm,),
        in_specs=[pl.BlockSpec((tm, MAIN_WIDTH), lambda i: (i, main_blk)),
                  pl.BlockSpec((tm, MEM_WIDTH), lambda i: (i, mem_blk)),
                  _const_spec((d, d), lambda i: (0, 0)),
                  pl.BlockSpec((tm, d), lambda i: (i, 0)),
                  _const_spec((1, d), lambda i: (0, 0)),
                  _const_spec((1, d), lambda i: (0, 0))],
        out_specs=[pl.BlockSpec((tm, d), lambda i: (i, 0))] * n_out,
        out_shape=[jax.ShapeDtypeStruct((n, d), F32), jax.ShapeDtypeStruct((n, d), BF16)][:n_out],
        compiler_params=_params(("arbitrary",), VMEM_MID_MIB),
        name="proj_ln",
    )(a_main, a_mem, w_bf, x, g, b)


def _ffn_ln_kernel(x_ref, xb_ref, wg_ref, wu_ref, wd_ref, g_ref, b_ref, o32_ref, o16_ref, acc_ref):
    j = pl.program_id(1)

    @pl.when(j == 0)
    def _():
        acc_ref[...] = jnp.zeros_like(acc_ref)

    xb = xb_ref[...]
    h = (_silu(_dot(xb, wg_ref[...])) * _dot(xb, wu_ref[...])).astype(BF16)
    acc_ref[...] += _dot(h, wd_ref[...])

    @pl.when(j == pl.num_programs(1) - 1)
    def _():
        y = _layer_norm(ALPHA * x_ref[...] + acc_ref[...], g_ref[...], b_ref[...])
        o32_ref[...] = y
        o16_ref[...] = y.astype(BF16)


def _ffn_ln(x, xb, wg_bf, wu_bf, wd_bf, g, b, tm, tf):
    n, d = x.shape
    f = wg_bf.shape[1]
    return pl.pallas_call(
        _ffn_ln_kernel,
        grid=(n // tm, f // tf),
        in_specs=[pl.BlockSpec((tm, d), lambda i, j: (i, 0)),
                  pl.BlockSpec((tm, d), lambda i, j: (i, 0)),
                  pl.BlockSpec((d, tf), lambda i, j: (0, j)),
                  pl.BlockSpec((d, tf), lambda i, j: (0, j)),
                  pl.BlockSpec((tf, d), lambda i, j: (j, 0)),
                  _const_spec((1, d), lambda i, j: (0, 0)),
                  _const_spec((1, d), lambda i, j: (0, 0))],
        out_specs=[pl.BlockSpec((tm, d), lambda i, j: (i, 0)),
                   pl.BlockSpec((tm, d), lambda i, j: (i, 0))],
        out_shape=[jax.ShapeDtypeStruct((n, d), F32), jax.ShapeDtypeStruct((n, d), BF16)],
        scratch_shapes=[pltpu.VMEM((tm, d), F32)],
        compiler_params=_params(("arbitrary", "arbitrary"), VMEM_LARGE_MIB),
        name="ffn_ln",
    )(x, xb, wg_bf, wu_bf, wd_bf, g, b)


def _matmul_kernel(x_ref, w_ref, s_ref, o_ref):
    o_ref[...] = (_dot(x_ref[...], w_ref[...]) * s_ref[...]).astype(o_ref.dtype)


def _matmul_colscale(xb, w_bf, col_scale, tm, tn):
    n, d = xb.shape
    cols = w_bf.shape[1]
    return pl.pallas_call(
        _matmul_kernel,
        grid=(n // tm, cols // tn),
        in_specs=[pl.BlockSpec((tm, d), lambda i, j: (i, 0)),
                  pl.BlockSpec((d, tn), lambda i, j: (0, j)),
                  pl.BlockSpec((1, tn), lambda i, j: (0, j))],
        out_specs=pl.BlockSpec((tm, tn), lambda i, j: (i, j)),
        out_shape=jax.ShapeDtypeStruct((n, cols), BF16),
        compiler_params=_params(("arbitrary", "arbitrary"), VMEM_MID_MIB),
        name="in_proj_b",
    )(xb, w_bf, col_scale)


def _proj_t_kernel(wt_ref, bias_ref, x_ref, o_ref):
    o_ref[...] = (_dot_nt(wt_ref[...], x_ref[...]) + bias_ref[...]).astype(o_ref.dtype)


def _proj_transposed(xb, wt_bf, bias_col, tn):
    n, d = xb.shape
    cols = wt_bf.shape[0]
    return pl.pallas_call(
        _proj_t_kernel,
        grid=(n // tn,),
        in_specs=[_const_spec((cols, d), lambda i: (0, 0)),
                  _const_spec((cols, 1), lambda i: (0, 0)),
                  pl.BlockSpec((tn, d), lambda i: (i, 0))],
        out_specs=pl.BlockSpec((cols, tn), lambda i: (0, i)),
        out_shape=jax.ShapeDtypeStruct((cols, n), BF16),
        compiler_params=_params(("arbitrary",), VMEM_MID_MIB),
        name="v_proj_t",
    )(wt_bf, bias_col, xb)


def _diff_attn_kernel(qi_tab, kj_tab, q1_ref, q2_ref, k1_ref, k2_ref, vt_ref,
                      lq1_ref, lk1_ref, lq2_ref, lk2_ref, sg_ref, *rest, lambda_init, qw, n_cast):
    cast_src = rest[:n_cast]
    o_ref = rest[n_cast]
    cast_dst = rest[n_cast + 1:2 * n_cast + 1]
    m1_ref, a1_ref, m2_ref, a2_ref = rest[2 * n_cast + 1:]
    tq, tk = q1_ref.shape[0], k1_ref.shape[0]
    p = pl.program_id(2)
    qi = qi_tab[p]
    kj = kj_tab[p]

    @pl.when(kj == 0)
    def _():
        for m_ref, a_ref in ((m1_ref, a1_ref), (m2_ref, a2_ref)):
            m_ref[...] = jnp.full_like(m_ref, NEG_BIG)
            a_ref[...] = jnp.zeros_like(a_ref)

    chains = [(q_ref, k_ref, m_ref, a_ref, j)
              for j in range(tq // qw)
              for q_ref, k_ref, m_ref, a_ref in ((q1_ref, k1_ref, m1_ref, a1_ref),
                                                 (q2_ref, k2_ref, m2_ref, a2_ref))]

    def step(key0):
        if key0 is None:
            live = chains
        else:
            live = [c for c in chains if key0 <= (c[4] + 1) * qw - 1]

        def n_keys(j):
            return tk if key0 is None else min(tk, (j + 1) * qw - key0)

        def scores(chain):
            q_ref, k_ref, j = chain[0], chain[1], chain[4]
            nk = n_keys(j)
            s = _dot_nt(k_ref[0:nk, :], q_ref[j * qw:(j + 1) * qw, :])
            if key0 is not None and key0 + nk - 1 > j * qw:
                key = key0 + lax.broadcasted_iota(jnp.int32, (nk, qw), 0)
                qry = j * qw + lax.broadcasted_iota(jnp.int32, (nk, qw), 1)
                s = jnp.where(key <= qry, s, NEG_BIG)
            return s.astype(BF16)

        def softmax(chain, s):
            m_ref, j = chain[2], chain[4]
            cols = slice(j * qw, (j + 1) * qw)
            m_old = m_ref[:, cols]
            m_new = jnp.maximum(m_old, jnp.max(s, axis=0, keepdims=True).astype(F32))
            alpha = jnp.exp(m_old - m_new)
            e = jnp.exp(s - m_new.astype(BF16))
            m_ref[:, cols] = m_new
            return e, alpha

        def weighted_values(chain, e, alpha):
            a_ref, j = chain[3], chain[4]
            cols = slice(j * qw, (j + 1) * qw)
            a_ref[:, cols] = alpha * a_ref[:, cols] + _dot(vt_ref[:, 0:n_keys(j)], e)

        n = len(live)
        s_live, e_live = {}, {}
        for t in range(n + 2):
            if t < len(cast_src):
                cast_dst[t][...] = cast_src[t][...].astype(BF16)
            if t < n:
                s_live[t] = scores(live[t])
            if 0 <= t - 1 < n:
                e_live[t - 1] = softmax(live[t - 1], s_live.pop(t - 1))
            if 0 <= t - 2 < n:
                weighted_values(live[t - 2], *e_live.pop(t - 2))

    key0 = kj * tk - qi * tq
    pl.when(key0 < 0)(lambda: step(None))
    for static_key0 in range(0, tq, tk):
        pl.when(key0 == static_key0)(functools.partial(step, static_key0))

    @pl.when(kj == ((qi + 1) * tq - 1) // tk)
    def _():
        lam = (jnp.exp(jnp.sum(lq1_ref[...] * lk1_ref[...], axis=-1, keepdims=True))
               - jnp.exp(jnp.sum(lq2_ref[...] * lk2_ref[...], axis=-1, keepdims=True))
               + lambda_init)
        o1 = a1_ref[0:V_DIM, :] / a1_ref[V_DIM:V_DIM + 1, :]
        o2 = a2_ref[0:V_DIM, :] / a2_ref[V_DIM:V_DIM + 1, :]
        o = o1 - lam * o2
        o = o * lax.rsqrt(jnp.mean(o * o, axis=0, keepdims=True) + LN_EPS) * sg_ref[...]
        o_ref[...] = jnp.transpose(o * (1.0 - lambda_init)).astype(o_ref.dtype)


def _diff_attn(hk, vt, lq1, lk1, lq2, lk2, subln_g, to_cast, bsz, seq, lambda_init, tq, tk):
    n = hk.shape[0]
    assert tq % tk == 0 and tk >= 2
    nq = seq // tq
    pairs =[(qi, kj) for qi in range(nq) for kj in range(((qi + 1) * tq - 1) // tk + 1)]
    qi_tab = jnp.asarray(np.array([p[0] for p in pairs], np.int32))
    kj_tab = jnp.asarray(np.array([p[1] for p in pairs], np.int32))
    nqb, nkb = seq // tq, seq // tk
    n_pairs = len(pairs)
    n_steps = bsz * DIFF_HEADS * n_pairs
    k_blk0 = (MAIN_WIDTH + MEM_WIDTH) // QK_DIM
    qspec = lambda off: pl.BlockSpec((tq, QK_DIM), lambda b, h, p, qt, kt: (b * nqb + qt[p], off + h))
    kspec = lambda off: pl.BlockSpec((tk, QK_DIM), lambda b, h, p, qt, kt: (b * nkb + kt[p], off + h))
    vec = lambda w: pl.BlockSpec((1, w), lambda b, h, p, qt, kt: (0, 0))

    def cast_spec(arr):
        rows, cols = arr.shape
        n_blk = 1
        while n_blk * 2 <= n_steps and rows % (n_blk * 2) == 0 and (rows // (n_blk * 2)) % 16 == 0:
            n_blk *= 2
        return pl.BlockSpec((rows // n_blk, cols), lambda b, h, p, qt, kt: (
            jnp.minimum((b * DIFF_HEADS + h) * n_pairs + p, n_blk - 1), 0))

    cast_specs = [cast_spec(a) for a in to_cast]
    grid_spec = pltpu.PrefetchScalarGridSpec(
        num_scalar_prefetch=2,
        grid=(bsz, DIFF_HEADS, n_pairs),
        in_specs=[qspec(0), qspec(DIFF_HEADS), kspec(k_blk0), kspec(k_blk0 + DIFF_HEADS),
                  pl.BlockSpec((V_AUG, tk), lambda b, h, p, qt, kt: (h, b * nkb + kt[p])),
                  vec(QK_DIM), vec(QK_DIM), vec(QK_DIM), vec(QK_DIM),
                  pl.BlockSpec((V_DIM, 1), lambda b, h, p, qt, kt: (0, 0))] + cast_specs,
        out_specs=[pl.BlockSpec((tq, V_DIM), lambda b, h, p, qt, kt: (b * nqb + qt[p], h))] + cast_specs,
        scratch_shapes=[pltpu.VMEM((1, tq), F32), pltpu.VMEM((V_AUG, tq), F32),
                        pltpu.VMEM((1, tq), F32), pltpu.VMEM((V_AUG, tq), F32)],
    )
    return pl.pallas_call(
        functools.partial(_diff_attn_kernel, lambda_init=lambda_init, qw=min(tq, 4 * LANES),
                          n_cast=len(to_cast)),
        grid_spec=grid_spec,
        out_shape=[jax.ShapeDtypeStruct((n, MAIN_WIDTH), BF16)]
                  + [jax.ShapeDtypeStruct(a.shape, BF16) for a in to_cast],
        compiler_params=_params(("arbitrary", "arbitrary", "arbitrary"), VMEM_MID_MIB),
        name="diff_attn",
    )(qi_tab, kj_tab, hk, hk, hk, hk, vt, lq1, lk1, lq2, lk2, subln_g.reshape(V_DIM, 1), *to_cast)


def _mem_attn_kernel(q_ref, kv_ref, o_ref):
    _mem_attention(q_ref[...].astype(F32), kv_ref, o_ref, 0)


def _mem_attn_b(hk, memkv, seq, tm):
    n = hk.shape[0]
    return pl.pallas_call(
        _mem_attn_kernel,
        grid=(n // tm,),
        in_specs=[pl.BlockSpec((tm, MEM_WIDTH), lambda i: (i, MAIN_WIDTH // MEM_WIDTH)),
                  pl.BlockSpec((None, None) + memkv.shape[2:], lambda i: (1, (i * tm) // seq, 0, 0))],
        out_specs=pl.BlockSpec((tm, MEM_WIDTH), lambda i: (i, 0)),
        out_shape=jax.ShapeDtypeStruct((n, MEM_WIDTH), BF16),
        compiler_params=_params(("arbitrary",), VMEM_SMALL_MIB),
        name="mem_attn_b",
    )(hk, memkv)


def _lane_cumsum(v, lane):
    for shift in (1, 2, 4):
        v = v + jnp.where(lane >= shift, pltpu.roll(v, shift, 1), 0.0)
    return v


def _route_kernel(x_ref, wr_ref, pos_ref, gate_ref, tile_e_ref, cnt_ref, base_ref, start_ref, top2_ref,
                  *, moe_tile):
    phase = pl.program_id(0)
    i = pl.program_id(1)
    tm = x_ref.shape[0]
    lane = lax.broadcasted_iota(jnp.int32, (tm, LANES), 1).astype(F32)

    @pl.when(jnp.logical_and(phase == 0, i == 0))
    def _():
        cnt_ref[...] = jnp.zeros_like(cnt_ref)

    @pl.when(phase == 0)
    def _():
        x = x_ref[...]
        xh = x.astype(BF16)
        xl = (x - xh.astype(F32)).astype(BF16)
        w = wr_ref[...]
        wh = w.astype(BF16)
        wl = (w - wh.astype(F32)).astype(BF16)
        logits = _dot(xh, wh) + (_dot(xh, wl) + _dot(xl, wh))
        logits = jnp.where(lane < N_EXPERTS, logits, NEG_BIG)
        v1 = jnp.max(logits, axis=-1, keepdims=True)
        i1 = jnp.min(jnp.where(logits == v1, lane, float(LANES)), axis=-1, keepdims=True)
        rest = jnp.where(lane == i1, NEG_BIG, logits)
        v2 = jnp.max(rest, axis=-1, keepdims=True)
        i2 = jnp.min(jnp.where(rest == v2, lane, float(LANES)), axis=-1, keepdims=True)
        top2_ref[i] = jnp.where(lane == 0, i1, jnp.where(lane == 1, i2, jnp.where(lane == 2, v1, v2)))
        sel = jnp.logical_or(lane == i1, lane == i2)
        cnt_ref[...] += jnp.sum(sel.astype(F32), axis=0, keepdims=True)

    @pl.when(jnp.logical_and(phase == 1, i == 0))
    def _():
        lane8 = lax.broadcasted_iota(jnp.int32, (8, LANES), 1).astype(F32)
        sub8 = lax.broadcasted_iota(jnp.int32, (8, LANES), 0).astype(F32)
        cnt = jnp.broadcast_to(cnt_ref[...], (8, LANES))
        padded = jnp.ceil(cnt * (1.0 / moe_tile)) * moe_tile
        ends = _lane_cumsum(padded, lane8)
        start_ref[...] = (ends - padded)[0:1, :]
        base_ref[...] = jnp.zeros_like(base_ref)
        tile_row0 = (sub8 * LANES + lane8) * moe_tile
        tile_e = jnp.zeros((8, LANES), F32)
        for e in range(N_EXPERTS):
            end_e = jnp.sum(jnp.where(lane8 == e, ends, 0.0), axis=-1, keepdims=True)
            tile_e = tile_e + (tile_row0 >= end_e).astype(F32)
        tile_e_ref[0:8, :] = jnp.minimum(tile_e, N_EXPERTS - 1.0).astype(jnp.int32)
        total = jnp.sum(jnp.where(lane8 == N_EXPERTS - 1, ends, 0.0), axis=-1, keepdims=True)
        tile_e_ref[8:16, :] = jnp.broadcast_to(total * (1.0 / moe_tile), (8, LANES)).astype(jnp.int32)
        tile_e_ref[16:24, :] = (ends - padded + cnt).astype(jnp.int32)

    @pl.when(phase == 1)
    def _():
        rec = top2_ref[i]
        i1, i2, v1, v2 = rec[:, 0:1], rec[:, 1:2], rec[:, 2:3], rec[:, 3:4]
        sel = jnp.logical_or(lane == i1, lane == i2)
        tile_cnt = jnp.sum(sel.astype(F32), axis=0, keepdims=True)
        r = lax.broadcasted_iota(jnp.int32, (tm, tm), 0)
        c = lax.broadcasted_iota(jnp.int32, (tm, tm), 1)
        before = (c < r).astype(BF16)
        rank = _dot(before, sel.astype(BF16))
        slot = start_ref[...] + base_ref[...] + rank
        p1 = jnp.sum(jnp.where(lane == i1, slot, 0.0), axis=-1, keepdims=True)
        p2 = jnp.sum(jnp.where(lane == i2, slot, 0.0), axis=-1, keepdims=True)
        pos_ref[...] = jnp.where(lane == 0, p1, jnp.where(lane == 1, p2, 0.0)).astype(jnp.int32)
        g1 = 1.0 / (1.0 + jnp.exp(v2 - v1))
        g2 = jnp.exp(v2 - v1) * g1
        gate_ref[...] = jnp.where(lane == 0, g1, jnp.where(lane == 1, g2, 0.0))
        base_ref[...] += tile_cnt


def _route(x, w_router_pad, tm, moe_tile):
    n, d = x.shape
    nt = n // tm
    return pl.pallas_call(
        functools.partial(_route_kernel, moe_tile=moe_tile),
        grid=(2, nt),
        in_specs=[pl.BlockSpec((tm, d), lambda ph, i: (i * (1 - ph) + (nt - 1) * ph, 0)),
                  _const_spec((d, LANES), lambda ph, i: (0, 0))],
        out_specs=[pl.BlockSpec((tm, LANES), lambda ph, i: (i * ph, 0)),
                   pl.BlockSpec((tm, LANES), lambda ph, i: (i * ph, 0)),
                   pl.BlockSpec((24, LANES), lambda ph, i: (0, 0))],
        out_shape=[jax.ShapeDtypeStruct((n, LANES), jnp.int32),
                   jax.ShapeDtypeStruct((n, LANES), F32),
                   jax.ShapeDtypeStruct((24, LANES), jnp.int32)],
        scratch_shapes=[pltpu.VMEM((1, LANES), F32), pltpu.VMEM((1, LANES), F32), pltpu.VMEM((1, LANES), F32),
                        pltpu.VMEM((nt, tm, LANES), F32)],
        compiler_params=_params(("arbitrary", "arbitrary"), VMEM_MID_MIB),
        name="route",
    )(x, w_router_pad)


def _dispatch_kernel(pad_row0, pos_ref, x_ref, xs_ref, zero_ref, sem, *, moe_tile):
    tm = x_ref.shape[0] * SUBLANES

    @pl.when(pl.program_id(0) == 0)
    def _():
        zero_ref[...] = jnp.zeros_like(zero_ref)
        zr = zero_ref.shape[0]

        def drain_fill(r, carry):
            pltpu.make_async_copy(zero_ref.at[pl.ds(0, 1)], xs_ref.at[pl.ds(0, 1)], sem).wait()
            return carry

        for e in range(N_EXPERTS):
            def fill(r, carry, e=e):
                pltpu.make_async_copy(zero_ref.at[pl.ds(0, 1)],
                                      xs_ref.at[pl.ds(pad_row0[e] + r, 1)], sem).start()
                return carry

            lax.fori_loop(0, moe_tile, fill, 0, unroll=DMA_LOOP_UNROLL)
            lax.fori_loop(0, moe_tile, drain_fill, 0, unroll=2 * DMA_LOOP_UNROLL)

        n_alloc = xs_ref.shape[0]
        tail = [pltpu.make_async_copy(
                    zero_ref,
                    xs_ref.at[pl.ds(pl.multiple_of(
                        jnp.minimum(pad_row0[N_EXPERTS] + c * zr, n_alloc - zr), zr), zr)], sem)
                for c in range((N_EXPERTS + 1) * moe_tile // zr)]
        for f in tail:
            f.start()
            f.wait()

    def issue(g, carry):
        for u in range(SUBLANES):
            for k in range(2):
                dst = pos_ref[0, 0, 2 * SUBLANES * g + 2 * u + k]
                pltpu.make_async_copy(x_ref.at[g, pl.ds(u, 1)], xs_ref.at[pl.ds(dst, 1)], sem).start(priority=k)
        return carry

    lax.fori_loop(0, tm // SUBLANES, issue, 0)

    def drain(t, carry):
        pltpu.make_async_copy(x_ref.at[0, pl.ds(0, 1)], xs_ref.at[pl.ds(0, 1)], sem).wait()
        return carry

    lax.fori_loop(0, 2 * tm, drain, 0, unroll=2 * DMA_LOOP_UNROLL)


def _dispatch(pad_row0, pos_flat, x, n_rows, moe_tile, tm):
    n, d = x.shape
    zero_rows = min(moe_tile, LANES)
    grid_spec = pltpu.PrefetchScalarGridSpec(
        num_scalar_prefetch=1,
        grid=(n // tm,),
        in_specs=[pl.BlockSpec((1, 1, 2 * tm), lambda i, pr: (i, 0, 0), memory_space=pltpu.SMEM),
                  pl.BlockSpec((tm // SUBLANES, SUBLANES, d), lambda i, pr: (i, 0, 0))],
        out_specs=pl.BlockSpec(memory_space=pl.ANY),
        scratch_shapes=[pltpu.VMEM((zero_rows, d), F32), pltpu.SemaphoreType.DMA(())],
    )
    return pl.pallas_call(
        functools.partial(_dispatch_kernel, moe_tile=moe_tile),
        grid_spec=grid_spec,
        out_shape=jax.ShapeDtypeStruct((n_rows + moe_tile, d), F32),
        compiler_params=_params(("arbitrary",), VMEM_SMALL_MIB),
        name="moe_dispatch",
    )(pad_row0, pos_flat, x.reshape(n // SUBLANES, SUBLANES, d))


def _experts_kernel(tile_e, n_valid, xs_ref, wg_ref, wu_ref, wd_ref, o_ref, xb_ref):
    del tile_e
    i = pl.program_id(0)
    j = pl.program_id(1)
    valid = i < n_valid[0]

    @pl.when(j == 0)
    def _():
        xb_ref[...] = xs_ref[...].astype(BF16)
        o_ref[...] = jnp.zeros_like(o_ref)

    @pl.when(valid)
    def _():
        xb = xb_ref[...]
        h = (_silu(_dot(xb, wg_ref[...])) * _dot(xb, wu_ref[...])).astype(BF16)
        o_ref[...] += _dot(h, wd_ref[...])


def _experts(tile_e, n_valid, xs, wg_bf, wu_bf, wd_bf, tm, tf):
    d = xs.shape[1]
    m = tile_e.shape[0] * tm
    f = wg_bf.shape[2]
    nj = f // tf

    def row_idx(i, j, te, nv):
        return (jnp.minimum(i, nv[0] - 1), 0)

    def col_of(i, j, nv):
        return jnp.where(i < nv[0], j, nj - 1)

    grid_spec = pltpu.PrefetchScalarGridSpec(
        num_scalar_prefetch=2,
        grid=(m // tm, nj),
        in_specs=[pl.BlockSpec((tm, d), row_idx),
                  pl.BlockSpec((None, d, tf), lambda i, j, te, nv: (te[i], 0, col_of(i, j, nv))),
                  pl.BlockSpec((None, d, tf), lambda i, j, te, nv: (te[i], 0, col_of(i, j, nv))),
                  pl.BlockSpec((None, tf, d), lambda i, j, te, nv: (te[i], col_of(i, j, nv), 0))],
        out_specs=pl.BlockSpec((tm, d), lambda i, j, te, nv: (i, 0)),
        scratch_shapes=[pltpu.VMEM((tm, d), BF16)],
    )
    return pl.pallas_call(
        _experts_kernel,
        grid_spec=grid_spec,
        out_shape=jax.ShapeDtypeStruct((m, d), F32),
        compiler_params=_params(("arbitrary", "arbitrary"), VMEM_LARGE_MIB),
        name="moe_experts",
    )(tile_e, n_valid, xs, wg_bf, wu_bf, wd_bf)


def _combine_kernel(pos_ref, gate_ref, x_ref, g_ref, b_ref, ys_ref, o_ref, buf_ref, sem):
    tm = x_ref.shape[0]

    def issue(g, carry):
        for u in range(SUBLANES):
            for k in range(2):
                src = pos_ref[0, 0, 2 * SUBLANES * g + 2 * u + k]
                pltpu.make_async_copy(ys_ref.at[pl.ds(src, 1)], buf_ref.at[k, g, pl.ds(u, 1)],
                                      sem).start(priority=k)
        return carry

    lax.fori_loop(0, tm // SUBLANES, issue, 0)

    def drain(t, carry):
        pltpu.make_async_copy(ys_ref.at[pl.ds(0, 1)], buf_ref.at[0, 0, pl.ds(0, 1)], sem).wait()
        return carry

    lax.fori_loop(0, 2 * tm, drain, 0, unroll=2 * DMA_LOOP_UNROLL)

    gates = gate_ref[...]
    d = x_ref.shape[1]
    y = gates[:, 0:1] * buf_ref[0].reshape(tm, d) + gates[:, 1:2] * buf_ref[1].reshape(tm, d)
    o_ref[...] = _layer_norm(ALPHA * x_ref[...] + y, g_ref[...], b_ref[...])


def _combine(pos_flat, gates, x, g, b, ys, tm):
    n, d = x.shape
    return pl.pallas_call(
        _combine_kernel,
        grid=(n // tm,),
        in_specs=[pl.BlockSpec((1, 1, 2 * tm), lambda i: (i, 0, 0), memory_space=pltpu.SMEM),
                  pl.BlockSpec((tm, LANES), lambda i: (i, 0)),
                  pl.BlockSpec((tm, d), lambda i: (i, 0)),
                  _const_spec((1, d), lambda i: (0, 0)),
                  _const_spec((1, d), lambda i: (0, 0)),
                  pl.BlockSpec(memory_space=pl.ANY)],
        out_specs=pl.BlockSpec((tm, d), lambda i: (i, 0)),
        out_shape=jax.ShapeDtypeStruct((n, d), F32),
        scratch_shapes=[pltpu.VMEM((2, tm // SUBLANES, SUBLANES, d), F32), pltpu.SemaphoreType.DMA(())],
        compiler_params=_params(("arbitrary",), VMEM_SMALL_MIB),
        name="moe_combine",
    )(pos_flat, gates, x, g, b, ys)


def kernel(x, mem, ln_g, ln_b, w_mix_out, w_mem_kv, a_w_in, a_vnorm_g, a_vnorm_b, a_w_s, a_b_s,
           shared_w_kv, b_w_in, b_lambda_q1, b_lambda_k1, b_lambda_q2, b_lambda_k2, b_subln_g,
           ffn_w_gate, ffn_w_up, ffn_w_down, moe_w_router, moe_w_gate, moe_w_up, moe_w_down):
    bsz, seq, d = x.shape
    n = bsz * seq
    t = _tiles(n, seq)
    xf = x.reshape(n, d)
    row = lambda v: v.reshape(1, -1)

    memkv = _memkv(mem, w_mem_kv.astype(BF16))

    mixed = _mixer_a(xf, a_w_in[0].astype(BF16), row(a_vnorm_g[0]), row(a_vnorm_b[0]),
                     a_w_s[0], jnp.transpose(a_b_s[0]), memkv, seq, t["mixer"])
    x1, x1b = _proj_ln(mixed, 0, mixed, MAIN_WIDTH // MEM_WIDTH, w_mix_out[0].astype(BF16), xf,
                       row(ln_g[0, 0]), row(ln_b[0, 0]), t["proj"], with_bf16=True)
    x2, x2b = _ffn_ln(x1, x1b, ffn_w_gate[0].astype(BF16), ffn_w_up[0].astype(BF16),
                      ffn_w_down[0].astype(BF16), row(ln_g[0, 1]), row(ln_b[0, 1]), t["ffn"], t["ffn_f"])

    w_cat = jnp.concatenate([b_w_in[0], shared_w_kv[:, :MAIN_WIDTH]], axis=1).astype(BF16)
    col_scale = jnp.concatenate([jnp.full((MAIN_WIDTH,), QK_DIM ** -0.5, F32),
                                 jnp.ones((w_cat.shape[1] - MAIN_WIDTH,), F32)]).reshape(1, -1)
    hk = _matmul_colscale(x2b, w_cat, col_scale, t["mm"], t["mm_n"])
    wv_t = jnp.transpose(shared_w_kv[:, MAIN_WIDTH:]).reshape(DIFF_HEADS, V_DIM, d)
    wv_t = jnp.pad(wv_t, ((0, 0), (0, ONES_ROWS), (0, 0))).reshape(DIFF_HEADS * V_AUG, d).astype(BF16)
    ones_bias = jnp.pad(jnp.zeros((DIFF_HEADS, V_DIM, 1), F32), ((0, 0), (0, ONES_ROWS), (0, 0)),
                        constant_values=1.0).reshape(DIFF_HEADS * V_AUG, 1)
    vt = _proj_transposed(x2b, wv_t, ones_bias, t["mm_t"])
    lambda_init = 0.8 - 0.6 * math.exp(-0.3 * 1)
    n_exp, _, d_ff = moe_w_gate[0].shape
    expert_w = [moe_w_gate[0].reshape(n_exp * d, d_ff), moe_w_up[0].reshape(n_exp * d, d_ff),
                moe_w_down[0].reshape(n_exp * d_ff, d)]
    main, wg_bf, wu_bf, wd_bf = _diff_attn(
        hk, vt, row(b_lambda_q1[0]), row(b_lambda_k1[0]), row(b_lambda_q2[0]), row(b_lambda_k2[0]),
        row(b_subln_g[0]), expert_w, bsz, seq, lambda_init, t["attn_q"], t["attn_k"])
    mem_out = _mem_attn_b(hk, memkv, seq, t["mem"])
    (x3,) = _proj_ln(main, 0, mem_out, 0, w_mix_out[1].astype(BF16), x2,
                     row(ln_g[1, 0]), row(ln_b[1, 0]), t["proj"], with_bf16=False)

    moe_tile = t["moe"]
    n_row_tiles = (2 * n) // moe_tile + N_EXPERTS
    w_router_pad = jnp.pad(moe_w_router[0], ((0, 0), (0, LANES - N_EXPERTS)))
    pos, gates, tile_info = _route(x3, w_router_pad, t["route"], moe_tile)
    tile_e = tile_info[0:8].reshape(-1)[:n_row_tiles]
    n_valid = tile_info[8, 0:1]
    tc = t["comb"]
    pos_flat = pos[:, :2].reshape(n // tc, 1, 2 * tc)
    fill_rows = jnp.concatenate([tile_info[16, :N_EXPERTS], n_valid * moe_tile])
    xs = _dispatch(fill_rows, pos_flat, x3, n_row_tiles * moe_tile, moe_tile, tc)
    ys = _experts(tile_e, n_valid, xs, wg_bf.reshape(n_exp, d, d_ff), wu_bf.reshape(n_exp, d, d_ff),
                  wd_bf.reshape(n_exp, d_ff, d), moe_tile, t["moe_f"])
    x4 = _combine(pos_flat, gates, x3, row(ln_g[1, 1]), row(ln_b[1, 1]), ys, tc)
    return x4.reshape(bsz, seq, d)
```

```python
import functools
import math

import numpy as np
import jax
import jax.numpy as jnp
from jax import lax
from jax.experimental import pallas as pl
from jax.experimental.pallas import tpu as pltpu

BF16 = jnp.bfloat16
F32 = jnp.float32

D_MODEL = 2048
MEM_WIDTH = 512
MAIN_WIDTH = 1536
MEM_HEADS = 4
MEM_HEAD_DIM = 128
CHUNK = 128
SG_GROUPS = 12
QK_DIM = 128
V_DIM = 256
DIFF_HEADS = 6
N_EXPERTS = 8
DEPTH = 2
ALPHA = (2.0 * DEPTH) ** 0.25
LN_EPS = 1e-5
LANES = 128
SUBLANES = 8
ONES_ROWS = 16
V_AUG = V_DIM + ONES_ROWS
NEG_BIG = -1e30
DMA_LOOP_UNROLL = 8
MIB = 1024 * 1024
V7X_VMEM_MIB = 64
VMEM_LARGE_MIB = V7X_VMEM_MIB - 8
VMEM_MID_MIB = V7X_VMEM_MIB - 16
VMEM_SMALL_MIB = V7X_VMEM_MIB // 2


def _tiles(n_tokens, seq):
    def fit(pref, total):
        t = min(pref, total)
        while total % t:
            t //= 2
        return t
    return dict(
        mixer=fit(512, seq),
        proj=fit(512, n_tokens),
        ffn=fit(512, n_tokens),
        ffn_f=512,
        mm=fit(1024, n_tokens),
        mm_n=1792,
        mm_t=fit(512, n_tokens),
        attn_q=fit(2048, seq),
        attn_k=fit(1024, seq),
        mem=fit(512, seq),
        route=fit(512, n_tokens),
        moe=fit(512, n_tokens),
        moe_f=1024,
        comb=fit(512, n_tokens),
    )


def _params(sem, vmem_mib):
    return pltpu.CompilerParams(dimension_semantics=sem, vmem_limit_bytes=vmem_mib * MIB)


def _const_spec(shape, index_map):
    return pl.BlockSpec(shape, index_map, pipeline_mode=pl.Buffered(1))


def _dot(a, b):
    return jnp.dot(a, b, preferred_element_type=F32)


def _dot_nt(a, b):
    return lax.dot_general(a, b, (((1,), (1,)), ((), ())), preferred_element_type=F32)


def _layer_norm(v, g, b):
    mu = jnp.mean(v, axis=-1, keepdims=True)
    c = v - mu
    var = jnp.mean(c * c, axis=-1, keepdims=True)
    return c * lax.rsqrt(var + LN_EPS) * g + b


def _gelu_tanh(v):
    return 0.5 * v * (1.0 + jnp.tanh(math.sqrt(2.0 / math.pi) * (v + 0.044715 * (v * v * v))))


def _silu(v):
    return v / (1.0 + jnp.exp(-v))


def _mem_attention(q, kv_ref, o_ref, col0):
    scale = MEM_HEAD_DIM ** -0.5
    for h in range(MEM_HEADS):
        lo = h * MEM_HEAD_DIM
        qh = q[:, lo:lo + MEM_HEAD_DIM].astype(BF16)
        kh = kv_ref[:, lo:lo + MEM_HEAD_DIM]
        vh = kv_ref[:, MEM_WIDTH + lo:MEM_WIDTH + lo + MEM_HEAD_DIM]
        s = _dot_nt(qh, kh) * scale
        e = jnp.exp(s - jnp.max(s, axis=-1, keepdims=True))
        o = _dot(e.astype(BF16), vh) / jnp.sum(e, axis=-1, keepdims=True)
        o_ref[:, col0 + lo:col0 + lo + MEM_HEAD_DIM] = o.astype(o_ref.dtype)


def _memkv_kernel(mem_ref, w_ref, o_ref):
    o_ref[...] = _dot(mem_ref[...].astype(BF16), w_ref[...]).astype(o_ref.dtype)


def _memkv(mem, w_mem_kv_bf):
    bsz, n_mem, d = mem.shape
    depth, _, cols = w_mem_kv_bf.shape
    return pl.pallas_call(
        _memkv_kernel,
        grid=(depth, bsz),
        in_specs=[pl.BlockSpec((None, n_mem, d), lambda l, b: (b, 0, 0)),
                  pl.BlockSpec((None, d, cols), lambda l, b: (l, 0, 0))],
        out_specs=pl.BlockSpec((None, None, n_mem, cols), lambda l, b: (l, b, 0, 0)),
        out_shape=jax.ShapeDtypeStruct((depth, bsz, n_mem, cols), BF16),
        compiler_params=_params(("arbitrary", "arbitrary"), VMEM_SMALL_MIB),
        name="memkv",
    )(mem, w_mem_kv_bf)


def _mixer_a_kernel(x_ref, w_ref, vg_ref, vb_ref, ws_ref, bs_ref, kv_ref, o_ref):
    tm = x_ref.shape[0]
    xb = x_ref[...].astype(BF16)
    v = _gelu_tanh(_dot(xb, w_ref[:, MAIN_WIDTH:2 * MAIN_WIDTH]))
    vn = _layer_norm(v, vg_ref[...], vb_ref[...]).astype(BF16)
    u = _gelu_tanh(_dot(xb, w_ref[:, :MAIN_WIDTH]))
    row = lax.broadcasted_iota(jnp.int32, (CHUNK, CHUNK), 0)
    col = lax.broadcasted_iota(jnp.int32, (CHUNK, CHUNK), 1)
    causal = col <= row
    for g in range(SG_GROUPS):
        wg = jnp.where(causal, ws_ref[g], 0.0).astype(BF16)
        bias = bs_ref[:, g:g + 1]
        for c in range(tm // CHUNK):
            rows = slice(c * CHUNK, (c + 1) * CHUNK)
            cols = slice(g * CHUNK, (g + 1) * CHUNK)
            s = _dot(wg, vn[rows, cols]) + bias
            o_ref[rows, cols] = (u[rows, cols] * s).astype(o_ref.dtype)
    q_mem = _dot(xb, w_ref[:, 2 * MAIN_WIDTH:])
    _mem_attention(q_mem, kv_ref, o_ref, MAIN_WIDTH)


def _mixer_a(x, w_in_bf, vnorm_g, vnorm_b, w_s, b_s_t, memkv, seq, tm):
    n, d = x.shape
    in_cols = w_in_bf.shape[1]
    return pl.pallas_call(
        _mixer_a_kernel,
        grid=(n // tm,),
        in_specs=[pl.BlockSpec((tm, d), lambda i: (i, 0)),
                  _const_spec((d, in_cols), lambda i: (0, 0)),
                  _const_spec((1, MAIN_WIDTH), lambda i: (0, 0)),
                  _const_spec((1, MAIN_WIDTH), lambda i: (0, 0)),
                  _const_spec((SG_GROUPS, CHUNK, CHUNK), lambda i: (0, 0, 0)),
                  _const_spec((CHUNK, SG_GROUPS), lambda i: (0, 0)),
                  pl.BlockSpec((None, None) + memkv.shape[2:], lambda i: (0, (i * tm) // seq, 0, 0))],
        out_specs=pl.BlockSpec((tm, d), lambda i: (i, 0)),
        out_shape=jax.ShapeDtypeStruct((n, d), BF16),
        compiler_params=_params(("arbitrary",), VMEM_LARGE_MIB),
        name="mixer_a",
    )(x, w_in_bf, vnorm_g, vnorm_b, w_s, b_s_t, memkv)


def _proj_ln_kernel(a1_ref, a2_ref, w_ref, x_ref, g_ref, b_ref, o32_ref, *maybe_o16_ref, sub):
    for r in range(0, x_ref.shape[0], sub):
        rows = slice(r, r + sub)
        mix = _dot(a1_ref[rows, :], w_ref[:MAIN_WIDTH, :]) + _dot(a2_ref[rows, :], w_ref[MAIN_WIDTH:, :])
        y = _layer_norm(ALPHA * x_ref[rows, :] + mix, g_ref[...], b_ref[...])
        o32_ref[rows, :] = y
        for o16_ref in maybe_o16_ref:
            o16_ref[rows, :] = y.astype(BF16)


def _proj_ln(a_main, main_blk, a_mem, mem_blk, w_bf, x, g, b, tm, with_bf16):
    n, d = x.shape
    n_out = 2 if with_bf16 else 1
    return pl.pallas_call(
        functools.partial(_proj_ln_kernel, sub=min(tm, 2 * LANES)),
        grid=(n // tm,),
        in_specs=[pl.BlockSpec((tm, MAIN_WIDTH), lambda i: (i, main_blk)),
                  pl.BlockSpec((tm, MEM_WIDTH), lambda i: (i, mem_blk)),
                  _const_spec((d, d), lambda i: (0, 0)),
                  pl.BlockSpec((tm, d), lambda i: (i, 0)),
                  _const_spec((1, d), lambda i: (0, 0)),
                  _const_spec((1, d), lambda i: (0, 0))],
        out_specs=[pl.BlockSpec((tm, d), lambda i: (i, 0))] * n_out,
        out_shape=[jax.ShapeDtypeStruct((n, d), F32), jax.ShapeDtypeStruct((n, d), BF16)][:n_out],
        compiler_params=_params(("arbitrary",), VMEM_MID_MIB),
        name="proj_ln",
    )(a_main, a_mem, w_bf, x, g, b)


def _ffn_ln_kernel(x_ref, xb_ref, wg_ref, wu_ref, wd_ref, g_ref, b_ref, o32_ref, o16_ref, acc_ref):
    j = pl.program_id(1)

    @pl.when(j == 0)
    def _():
        acc_ref[...] = jnp.zeros_like(acc_ref)

    xb = xb_ref[...]
    h = (_silu(_dot(xb, wg_ref[...])) * _dot(xb, wu_ref[...])).astype(BF16)
    acc_ref[...] += _dot(h, wd_ref[...])

    @pl.when(j == pl.num_programs(1) - 1)
    def _():
        y = _layer_norm(ALPHA * x_ref[...] + acc_ref[...], g_ref[...], b_ref[...])
        o32_ref[...] = y
        o16_ref[...] = y.astype(BF16)


def _ffn_ln(x, xb, wg_bf, wu_bf, wd_bf, g, b, tm, tf):
    n, d = x.shape
    f = wg_bf.shape[1]
    return pl.pallas_call(
        _ffn_ln_kernel,
        grid=(n // tm, f // tf),
        in_specs=[pl.BlockSpec((tm, d), lambda i, j: (i, 0)),
                  pl.BlockSpec((tm, d), lambda i, j: (i, 0)),
                  pl.BlockSpec((d, tf), lambda i, j: (0, j)),
                  pl.BlockSpec((d, tf), lambda i, j: (0, j)),
                  pl.BlockSpec((tf, d), lambda i, j: (j, 0)),
                  _const_spec((1, d), lambda i, j: (0, 0)),
                  _const_spec((1, d), lambda i, j: (0, 0))],
        out_specs=[pl.BlockSpec((tm, d), lambda i, j: (i, 0)),
                   pl.BlockSpec((tm, d), lambda i, j: (i, 0))],
        out_shape=[jax.ShapeDtypeStruct((n, d), F32), jax.ShapeDtypeStruct((n, d), BF16)],
        scratch_shapes=[pltpu.VMEM((tm, d), F32)],
        compiler_params=_params(("arbitrary", "arbitrary"), VMEM_LARGE_MIB),
        name="ffn_ln",
    )(x, xb, wg_bf, wu_bf, wd_bf, g, b)


def _matmul_kernel(x_ref, w_ref, s_ref, o_ref):
    o_ref[...] = (_dot(x_ref[...], w_ref[...]) * s_ref[...]).astype(o_ref.dtype)


def _matmul_colscale(xb, w_bf, col_scale, tm, tn):
    n, d = xb.shape
    cols = w_bf.shape[1]
    return pl.pallas_call(
        _matmul_kernel,
        grid=(n // tm, cols // tn),
        in_specs=[pl.BlockSpec((tm, d), lambda i, j: (i, 0)),
                  pl.BlockSpec((d, tn), lambda i, j: (0, j)),
                  pl.BlockSpec((1, tn), lambda i, j: (0, j))],
        out_specs=pl.BlockSpec((tm, tn), lambda i, j: (i, j)),
        out_shape=jax.ShapeDtypeStruct((n, cols), BF16),
        compiler_params=_params(("arbitrary", "arbitrary"), VMEM_MID_MIB),
        name="in_proj_b",
    )(xb, w_bf, col_scale)


def _proj_t_kernel(wt_ref, bias_ref, x_ref, o_ref):
    o_ref[...] = (_dot_nt(wt_ref[...], x_ref[...]) + bias_ref[...]).astype(o_ref.dtype)


def _proj_transposed(xb, wt_bf, bias_col, tn):
    n, d = xb.shape
    cols = wt_bf.shape[0]
    return pl.pallas_call(
        _proj_t_kernel,
        grid=(n // tn,),
        in_specs=[_const_spec((cols, d), lambda i: (0, 0)),
                  _const_spec((cols, 1), lambda i: (0, 0)),
                  pl.BlockSpec((tn, d), lambda i: (i, 0))],
        out_specs=pl.BlockSpec((cols, tn), lambda i: (0, i)),
        out_shape=jax.ShapeDtypeStruct((cols, n), BF16),
        compiler_params=_params(("arbitrary",), VMEM_MID_MIB),
        name="v_proj_t",
    )(wt_bf, bias_col, xb)


def _diff_attn_kernel(qi_tab, kj_tab, q1_ref, q2_ref, k1_ref, k2_ref, vt_ref,
                      lq1_ref, lk1_ref, lq2_ref, lk2_ref, sg_ref, *rest, lambda_init, qw, n_cast):
    cast_src = rest[:n_cast]
    o_ref = rest[n_cast]
    cast_dst = rest[n_cast + 1:2 * n_cast + 1]
    m1_ref, a1_ref, m2_ref, a2_ref = rest[2 * n_cast + 1:]
    tq, tk = q1_ref.shape[0], k1_ref.shape[0]
    p = pl.program_id(2)
    qi = qi_tab[p]
    kj = kj_tab[p]

    @pl.when(kj == 0)
    def _():
        for m_ref, a_ref in ((m1_ref, a1_ref), (m2_ref, a2_ref)):
            m_ref[...] = jnp.full_like(m_ref, NEG_BIG)
            a_ref[...] = jnp.zeros_like(a_ref)

    chains = [(q_ref, k_ref, m_ref, a_ref, j)
              for j in range(tq // qw)
              for q_ref, k_ref, m_ref, a_ref in ((q1_ref, k1_ref, m1_ref, a1_ref),
                                                 (q2_ref, k2_ref, m2_ref, a2_ref))]

    def step(key0):
        if key0 is None:
            live = chains
        else:
            live = [c for c in chains if key0 <= (c[4] + 1) * qw - 1]

        def n_keys(j):
            return tk if key0 is None else min(tk, (j + 1) * qw - key0)

        def scores(chain):
            q_ref, k_ref, j = chain[0], chain[1], chain[4]
            nk = n_keys(j)
            s = _dot_nt(k_ref[0:nk, :], q_ref[j * qw:(j + 1) * qw, :])
            if key0 is not None and key0 + nk - 1 > j * qw:
                key = key0 + lax.broadcasted_iota(jnp.int32, (nk, qw), 0)
                qry = j * qw + lax.broadcasted_iota(jnp.int32, (nk, qw), 1)
                s = jnp.where(key <= qry, s, NEG_BIG)
            return s.astype(BF16)

        def softmax(chain, s):
            m_ref, j = chain[2], chain[4]
            cols = slice(j * qw, (j + 1) * qw)
            m_old = m_ref[:, cols]
            m_new = jnp.maximum(m_old, jnp.max(s, axis=0, keepdims=True).astype(F32))
            alpha = jnp.exp(m_old - m_new)
            e = jnp.exp(s - m_new.astype(BF16))
            m_ref[:, cols] = m_new
            return e, alpha

        def weighted_values(chain, e, alpha):
            a_ref, j = chain[3], chain[4]
            cols = slice(j * qw, (j + 1) * qw)
            a_ref[:, cols] = alpha * a_ref[:, cols] + _dot(vt_ref[:, 0:n_keys(j)], e)

        n = len(live)
        s_live, e_live = {}, {}
        for t in range(n + 2):
            if t < len(cast_src):
                cast_dst[t][...] = cast_src[t][...].astype(BF16)
            if t < n:
                s_live[t] = scores(live[t])
            if 0 <= t - 1 < n:
                e_live[t - 1] = softmax(live[t - 1], s_live.pop(t - 1))
            if 0 <= t - 2 < n:
                weighted_values(live[t - 2], *e_live.pop(t - 2))

    key0 = kj * tk - qi * tq
    pl.when(key0 < 0)(lambda: step(None))
    for static_key0 in range(0, tq, tk):
        pl.when(key0 == static_key0)(functools.partial(step, static_key0))

    @pl.when(kj == ((qi + 1) * tq - 1) // tk)
    def _():
        lam = (jnp.exp(jnp.sum(lq1_ref[...] * lk1_ref[...], axis=-1, keepdims=True))
               - jnp.exp(jnp.sum(lq2_ref[...] * lk2_ref[...], axis=-1, keepdims=True))
               + lambda_init)
        o1 = a1_ref[0:V_DIM, :] / a1_ref[V_DIM:V_DIM + 1, :]
        o2 = a2_ref[0:V_DIM, :] / a2_ref[V_DIM:V_DIM + 1, :]
        o = o1 - lam * o2
        o = o * lax.rsqrt(jnp.mean(o * o, axis=0, keepdims=True) + LN_EPS) * sg_ref[...]
        o_ref[...] = jnp.transpose(o * (1.0 - lambda_init)).astype(o_ref.dtype)


def _diff_attn(hk, vt, lq1, lk1, lq2, lk2, subln_g, to_cast, bsz, seq, lambda_init, tq, tk):
    n = hk.shape[0]
    assert tq % tk == 0 and tk >= 2
    nq = seq // tq
    pairs =[(qi, kj) for qi in range(nq) for kj in range(((qi + 1) * tq - 1) // tk + 1)]
    qi_tab = jnp.asarray(np.array([p[0] for p in pairs], np.int32))
    kj_tab = jnp.asarray(np.array([p[1] for p in pairs], np.int32))
    nqb, nkb = seq // tq, seq // tk
    n_pairs = len(pairs)
    n_steps = bsz * DIFF_HEADS * n_pairs
    k_blk0 = (MAIN_WIDTH + MEM_WIDTH) // QK_DIM
    qspec = lambda off: pl.BlockSpec((tq, QK_DIM), lambda b, h, p, qt, kt: (b * nqb + qt[p], off + h))
    kspec = lambda off: pl.BlockSpec((tk, QK_DIM), lambda b, h, p, qt, kt: (b * nkb + kt[p], off + h))
    vec = lambda w: pl.BlockSpec((1, w), lambda b, h, p, qt, kt: (0, 0))

    def cast_spec(arr):
        rows, cols = arr.shape
        n_blk = 1
        while n_blk * 2 <= n_steps and rows % (n_blk * 2) == 0 and (rows // (n_blk * 2)) % 16 == 0:
            n_blk *= 2
        return pl.BlockSpec((rows // n_blk, cols), lambda b, h, p, qt, kt: (
            jnp.minimum((b * DIFF_HEADS + h) * n_pairs + p, n_blk - 1), 0))

    cast_specs = [cast_spec(a) for a in to_cast]
    grid_spec = pltpu.PrefetchScalarGridSpec(
        num_scalar_prefetch=2,
        grid=(bsz, DIFF_HEADS, n_pairs),
        in_specs=[qspec(0), qspec(DIFF_HEADS), kspec(k_blk0), kspec(k_blk0 + DIFF_HEADS),
                  pl.BlockSpec((V_AUG, tk), lambda b, h, p, qt, kt: (h, b * nkb + kt[p])),
                  vec(QK_DIM), vec(QK_DIM), vec(QK_DIM), vec(QK_DIM),
                  pl.BlockSpec((V_DIM, 1), lambda b, h, p, qt, kt: (0, 0))] + cast_specs,
        out_specs=[pl.BlockSpec((tq, V_DIM), lambda b, h, p, qt, kt: (b * nqb + qt[p], h))] + cast_specs,
        scratch_shapes=[pltpu.VMEM((1, tq), F32), pltpu.VMEM((V_AUG, tq), F32),
                        pltpu.VMEM((1, tq), F32), pltpu.VMEM((V_AUG, tq), F32)],
    )
    return pl.pallas_call(
        functools.partial(_diff_attn_kernel, lambda_init=lambda_init, qw=min(tq, 2 * LANES),
                          n_cast=len(to_cast)),
        grid_spec=grid_spec,
        out_shape=[jax.ShapeDtypeStruct((n, MAIN_WIDTH), BF16)]
                  + [jax.ShapeDtypeStruct(a.shape, BF16) for a in to_cast],
        compiler_params=_params(("arbitrary", "arbitrary", "arbitrary"), VMEM_MID_MIB),
        name="diff_attn",
    )(qi_tab, kj_tab, hk, hk, hk, hk, vt, lq1, lk1, lq2, lk2, subln_g.reshape(V_DIM, 1), *to_cast)


def _mem_attn_kernel(q_ref, kv_ref, o_ref):
    _mem_attention(q_ref[...].astype(F32), kv_ref, o_ref, 0)


def _mem_attn_b(hk, memkv, seq, tm):
    n = hk.shape[0]
    return pl.pallas_call(
        _mem_attn_kernel,
        grid=(n // tm,),
        in_specs=[pl.BlockSpec((tm, MEM_WIDTH), lambda i: (i, MAIN_WIDTH // MEM_WIDTH)),
                  pl.BlockSpec((None, None) + memkv.shape[2:], lambda i: (1, (i * tm) // seq, 0, 0))],
        out_specs=pl.BlockSpec((tm, MEM_WIDTH), lambda i: (i, 0)),
        out_shape=jax.ShapeDtypeStruct((n, MEM_WIDTH), BF16),
        compiler_params=_params(("arbitrary",), VMEM_SMALL_MIB),
        name="mem_attn_b",
    )(hk, memkv)


def _lane_cumsum(v, lane):
    for shift in (1, 2, 4):
        v = v + jnp.where(lane >= shift, pltpu.roll(v, shift, 1), 0.0)
    return v


def _route_kernel(x_ref, wr_ref, pos_ref, gate_ref, tile_e_ref, cnt_ref, base_ref, start_ref, top2_ref,
                  *, moe_tile):
    phase = pl.program_id(0)
    i = pl.program_id(1)
    tm = x_ref.shape[0]
    lane = lax.broadcasted_iota(jnp.int32, (tm, LANES), 1).astype(F32)

    @pl.when(jnp.logical_and(phase == 0, i == 0))
    def _():
        cnt_ref[...] = jnp.zeros_like(cnt_ref)

    @pl.when(phase == 0)
    def _():
        x = x_ref[...]
        xh = x.astype(BF16)
        xl = (x - xh.astype(F32)).astype(BF16)
        w = wr_ref[...]
        wh = w.astype(BF16)
        wl = (w - wh.astype(F32)).astype(BF16)
        logits = _dot(xh, wh) + (_dot(xh, wl) + _dot(xl, wh))
        logits = jnp.where(lane < N_EXPERTS, logits, NEG_BIG)
        v1 = jnp.max(logits, axis=-1, keepdims=True)
        i1 = jnp.min(jnp.where(logits == v1, lane, float(LANES)), axis=-1, keepdims=True)
        rest = jnp.where(lane == i1, NEG_BIG, logits)
        v2 = jnp.max(rest, axis=-1, keepdims=True)
        i2 = jnp.min(jnp.where(rest == v2, lane, float(LANES)), axis=-1, keepdims=True)
        top2_ref[i] = jnp.where(lane == 0, i1, jnp.where(lane == 1, i2, jnp.where(lane == 2, v1, v2)))
        sel = jnp.logical_or(lane == i1, lane == i2)
        cnt_ref[...] += jnp.sum(sel.astype(F32), axis=0, keepdims=True)

    @pl.when(jnp.logical_and(phase == 1, i == 0))
    def _():
        lane8 = lax.broadcasted_iota(jnp.int32, (8, LANES), 1).astype(F32)
        sub8 = lax.broadcasted_iota(jnp.int32, (8, LANES), 0).astype(F32)
        cnt = jnp.broadcast_to(cnt_ref[...], (8, LANES))
        padded = jnp.ceil(cnt * (1.0 / moe_tile)) * moe_tile
        ends = _lane_cumsum(padded, lane8)
        start_ref[...] = (ends - padded)[0:1, :]
        base_ref[...] = jnp.zeros_like(base_ref)
        tile_row0 = (sub8 * LANES + lane8) * moe_tile
        tile_e = jnp.zeros((8, LANES), F32)
        for e in range(N_EXPERTS):
            end_e = jnp.sum(jnp.where(lane8 == e, ends, 0.0), axis=-1, keepdims=True)
            tile_e = tile_e + (tile_row0 >= end_e).astype(F32)
        tile_e_ref[0:8, :] = jnp.minimum(tile_e, N_EXPERTS - 1.0).astype(jnp.int32)
        total = jnp.sum(jnp.where(lane8 == N_EXPERTS - 1, ends, 0.0), axis=-1, keepdims=True)
        tile_e_ref[8:16, :] = jnp.broadcast_to(total * (1.0 / moe_tile), (8, LANES)).astype(jnp.int32)
        tile_e_ref[16:24, :] = (ends - padded + cnt).astype(jnp.int32)

    @pl.when(phase == 1)
    def _():
        rec = top2_ref[i]
        i1, i2, v1, v2 = rec[:, 0:1], rec[:, 1:2], rec[:, 2:3], rec[:, 3:4]
        sel = jnp.logical_or(lane == i1, lane == i2)
        tile_cnt = jnp.sum(sel.astype(F32), axis=0, keepdims=True)
        r = lax.broadcasted_iota(jnp.int32, (tm, tm), 0)
        c = lax.broadcasted_iota(jnp.int32, (tm, tm), 1)
        before = (c < r).astype(BF16)
        rank = _dot(before, sel.astype(BF16))
        slot = start_ref[...] + base_ref[...] + rank
        p1 = jnp.sum(jnp.where(lane == i1, slot, 0.0), axis=-1, keepdims=True)
        p2 = jnp.sum(jnp.where(lane == i2, slot, 0.0), axis=-1, keepdims=True)
        pos_ref[...] = jnp.where(lane == 0, p1, jnp.where(lane == 1, p2, 0.0)).astype(jnp.int32)
        g1 = 1.0 / (1.0 + jnp.exp(v2 - v1))
        g2 = jnp.exp(v2 - v1) * g1
        gate_ref[...] = jnp.where(lane == 0, g1, jnp.where(lane == 1, g2, 0.0))
        base_ref[...] += tile_cnt


def _route(x, w_router_pad, tm, moe_tile):
    n, d = x.shape
    nt = n // tm
    return pl.pallas_call(
        functools.partial(_route_kernel, moe_tile=moe_tile),
        grid=(2, nt),
        in_specs=[pl.BlockSpec((tm, d), lambda ph, i: (i * (1 - ph) + (nt - 1) * ph, 0)),
                  _const_spec((d, LANES), lambda ph, i: (0, 0))],
        out_specs=[pl.BlockSpec((tm, LANES), lambda ph, i: (i * ph, 0)),
                   pl.BlockSpec((tm, LANES), lambda ph, i: (i * ph, 0)),
                   pl.BlockSpec((24, LANES), lambda ph, i: (0, 0))],
        out_shape=[jax.ShapeDtypeStruct((n, LANES), jnp.int32),
                   jax.ShapeDtypeStruct((n, LANES), F32),
                   jax.ShapeDtypeStruct((24, LANES), jnp.int32)],
        scratch_shapes=[pltpu.VMEM((1, LANES), F32), pltpu.VMEM((1, LANES), F32), pltpu.VMEM((1, LANES), F32),
                        pltpu.VMEM((nt, tm, LANES), F32)],
        compiler_params=_params(("arbitrary", "arbitrary"), VMEM_MID_MIB),
        name="route",
    )(x, w_router_pad)


def _dispatch_kernel(pad_row0, pos_ref, x_ref, xs_ref, zero_ref, sem, *, moe_tile):
    tm = x_ref.shape[0] * SUBLANES

    @pl.when(pl.program_id(0) == 0)
    def _():
        zero_ref[...] = jnp.zeros_like(zero_ref)
        zr = zero_ref.shape[0]

        def drain_fill(r, carry):
            pltpu.make_async_copy(zero_ref.at[pl.ds(0, 1)], xs_ref.at[pl.ds(0, 1)], sem).wait()
            return carry

        for e in range(N_EXPERTS):
            def fill(r, carry, e=e):
                pltpu.make_async_copy(zero_ref.at[pl.ds(0, 1)],
                                      xs_ref.at[pl.ds(pad_row0[e] + r, 1)], sem).start()
                return carry

            lax.fori_loop(0, moe_tile, fill, 0, unroll=DMA_LOOP_UNROLL)
            lax.fori_loop(0, moe_tile, drain_fill, 0, unroll=2 * DMA_LOOP_UNROLL)

        n_alloc = xs_ref.shape[0]
        tail = [pltpu.make_async_copy(
                    zero_ref,
                    xs_ref.at[pl.ds(pl.multiple_of(
                        jnp.minimum(pad_row0[N_EXPERTS] + c * zr, n_alloc - zr), zr), zr)], sem)
                for c in range((N_EXPERTS + 1) * moe_tile // zr)]
        for f in tail:
            f.start()
            f.wait()

    def issue(g, carry):
        for u in range(SUBLANES):
            for k in range(2):
                dst = pos_ref[0, 0, 2 * SUBLANES * g + 2 * u + k]
                pltpu.make_async_copy(x_ref.at[g, pl.ds(u, 1)], xs_ref.at[pl.ds(dst, 1)], sem).start(priority=k)
        return carry

    lax.fori_loop(0, tm // SUBLANES, issue, 0)

    def drain(t, carry):
        pltpu.make_async_copy(x_ref.at[0, pl.ds(0, 1)], xs_ref.at[pl.ds(0, 1)], sem).wait()
        return carry

    lax.fori_loop(0, 2 * tm, drain, 0, unroll=2 * DMA_LOOP_UNROLL)


def _dispatch(pad_row0, pos_flat, x, n_rows, moe_tile, tm):
    n, d = x.shape
    zero_rows = min(moe_tile, LANES)
    grid_spec = pltpu.PrefetchScalarGridSpec(
        num_scalar_prefetch=1,
        grid=(n // tm,),
        in_specs=[pl.BlockSpec((1, 1, 2 * tm), lambda i, pr: (i, 0, 0), memory_space=pltpu.SMEM),
                  pl.BlockSpec((tm // SUBLANES, SUBLANES, d), lambda i, pr: (i, 0, 0))],
        out_specs=pl.BlockSpec(memory_space=pl.ANY),
        scratch_shapes=[pltpu.VMEM((zero_rows, d), F32), pltpu.SemaphoreType.DMA(())],
    )
    return pl.pallas_call(
        functools.partial(_dispatch_kernel, moe_tile=moe_tile),
        grid_spec=grid_spec,
        out_shape=jax.ShapeDtypeStruct((n_rows + moe_tile, d), F32),
        compiler_params=_params(("arbitrary",), VMEM_SMALL_MIB),
        name="moe_dispatch",
    )(pad_row0, pos_flat, x.reshape(n // SUBLANES, SUBLANES, d))


def _experts_kernel(tile_e, n_valid, xs_ref, wg_ref, wu_ref, wd_ref, o_ref, xb_ref):
    del tile_e
    i = pl.program_id(0)
    j = pl.program_id(1)
    valid = i < n_valid[0]

    @pl.when(j == 0)
    def _():
        xb_ref[...] = xs_ref[...].astype(BF16)
        o_ref[...] = jnp.zeros_like(o_ref)

    @pl.when(valid)
    def _():
        xb = xb_ref[...]
        h = (_silu(_dot(xb, wg_ref[...])) * _dot(xb, wu_ref[...])).astype(BF16)
        o_ref[...] += _dot(h, wd_ref[...])


def _experts(tile_e, n_valid, xs, wg_bf, wu_bf, wd_bf, tm, tf):
    d = xs.shape[1]
    m = tile_e.shape[0] * tm
    f = wg_bf.shape[2]
    nj = f // tf

    def row_idx(i, j, te, nv):
        return (jnp.minimum(i, nv[0] - 1), 0)

    def col_of(i, j, nv):
        return jnp.where(i < nv[0], j, nj - 1)

    grid_spec = pltpu.PrefetchScalarGridSpec(
        num_scalar_prefetch=2,
        grid=(m // tm, nj),
        in_specs=[pl.BlockSpec((tm, d), row_idx),
                  pl.BlockSpec((None, d, tf), lambda i, j, te, nv: (te[i], 0, col_of(i, j, nv))),
                  pl.BlockSpec((None, d, tf), lambda i, j, te, nv: (te[i], 0, col_of(i, j, nv))),
                  pl.BlockSpec((None, tf, d), lambda i, j, te, nv: (te[i], col_of(i, j, nv), 0))],
        out_specs=pl.BlockSpec((tm, d), lambda i, j, te, nv: (i, 0)),
        scratch_shapes=[pltpu.VMEM((tm, d), BF16)],
    )
    return pl.pallas_call(
        _experts_kernel,
        grid_spec=grid_spec,
        out_shape=jax.ShapeDtypeStruct((m, d), F32),
        compiler_params=_params(("arbitrary", "arbitrary"), VMEM_LARGE_MIB),
        name="moe_experts",
    )(tile_e, n_valid, xs, wg_bf, wu_bf, wd_bf)


def _combine_kernel(pos_ref, gate_ref, x_ref, g_ref, b_ref, ys_ref, o_ref, buf_ref, sem):
    tm = x_ref.shape[0]

    def issue(g, carry):
        for u in range(SUBLANES):
            for k in range(2):
                src = pos_ref[0, 0, 2 * SUBLANES * g + 2 * u + k]
                pltpu.make_async_copy(ys_ref.at[pl.ds(src, 1)], buf_ref.at[k, g, pl.ds(u, 1)],
                                      sem).start(priority=k)
        return carry

    lax.fori_loop(0, tm // SUBLANES, issue, 0)

    def drain(t, carry):
        pltpu.make_async_copy(ys_ref.at[pl.ds(0, 1)], buf_ref.at[0, 0, pl.ds(0, 1)], sem).wait()
        return carry

    lax.fori_loop(0, 2 * tm, drain, 0, unroll=2 * DMA_LOOP_UNROLL)

    gates = gate_ref[...]
    d = x_ref.shape[1]
    y = gates[:, 0:1] * buf_ref[0].reshape(tm, d) + gates[:, 1:2] * buf_ref[1].reshape(tm, d)
    o_ref[...] = _layer_norm(ALPHA * x_ref[...] + y, g_ref[...], b_ref[...])


def _combine(pos_flat, gates, x, g, b, ys, tm):
    n, d = x.shape
    return pl.pallas_call(
        _combine_kernel,
        grid=(n // tm,),
        in_specs=[pl.BlockSpec((1, 1, 2 * tm), lambda i: (i, 0, 0), memory_space=pltpu.SMEM),
                  pl.BlockSpec((tm, LANES), lambda i: (i, 0)),
                  pl.BlockSpec((tm, d), lambda i: (i, 0)),
                  _const_spec((1, d), lambda i: (0, 0)),
                  _const_spec((1, d), lambda i: (0, 0)),
                  pl.BlockSpec(memory_space=pl.ANY)],
        out_specs=pl.BlockSpec((tm, d), lambda i: (i, 0)),
        out_shape=jax.ShapeDtypeStruct((n, d), F32),
        scratch_shapes=[pltpu.VMEM((2, tm // SUBLANES, SUBLANES, d), F32), pltpu.SemaphoreType.DMA(())],
        compiler_params=_params(("arbitrary",), VMEM_SMALL_MIB),
        name="moe_combine",
    )(pos_flat, gates, x, g, b, ys)


def kernel(x, mem, ln_g, ln_b, w_mix_out, w_mem_kv, a_w_in, a_vnorm_g, a_vnorm_b, a_w_s, a_b_s,
           shared_w_kv, b_w_in, b_lambda_q1, b_lambda_k1, b_lambda_q2, b_lambda_k2, b_subln_g,
           ffn_w_gate, ffn_w_up, ffn_w_down, moe_w_router, moe_w_gate, moe_w_up, moe_w_down):
    bsz, seq, d = x.shape
    n = bsz * seq
    t = _tiles(n, seq)
    xf = x.reshape(n, d)
    row = lambda v: v.reshape(1, -1)

    memkv = _memkv(mem, w_mem_kv.astype(BF16))

    mixed = _mixer_a(xf, a_w_in[0].astype(BF16), row(a_vnorm_g[0]), row(a_vnorm_b[0]),
                     a_w_s[0], jnp.transpose(a_b_s[0]), memkv, seq, t["mixer"])
    x1, x1b = _proj_ln(mixed, 0, mixed, MAIN_WIDTH // MEM_WIDTH, w_mix_out[0].astype(BF16), xf,
                       row(ln_g[0, 0]), row(ln_b[0, 0]), t["proj"], with_bf16=True)
    x2, x2b = _ffn_ln(x1, x1b, ffn_w_gate[0].astype(BF16), ffn_w_up[0].astype(BF16),
                      ffn_w_down[0].astype(BF16), row(ln_g[0, 1]), row(ln_b[0, 1]), t["ffn"], t["ffn_f"])

    w_cat = jnp.concatenate([b_w_in[0], shared_w_kv[:, :MAIN_WIDTH]], axis=1).astype(BF16)
    col_scale = jnp.concatenate([jnp.full((MAIN_WIDTH,), QK_DIM ** -0.5, F32),
                                 jnp.ones((w_cat.shape[1] - MAIN_WIDTH,), F32)]).reshape(1, -1)
    hk = _matmul_colscale(x2b, w_cat, col_scale, t["mm"], t["mm_n"])
    wv_t = jnp.transpose(shared_w_kv[:, MAIN_WIDTH:]).reshape(DIFF_HEADS, V_DIM, d)
    wv_t = jnp.pad(wv_t, ((0, 0), (0, ONES_ROWS), (0, 0))).reshape(DIFF_HEADS * V_AUG, d).astype(BF16)
    ones_bias = jnp.pad(jnp.zeros((DIFF_HEADS, V_DIM, 1), F32), ((0, 0), (0, ONES_ROWS), (0, 0)),
                        constant_values=1.0).reshape(DIFF_HEADS * V_AUG, 1)
    vt = _proj_transposed(x2b, wv_t, ones_bias, t["mm_t"])
    lambda_init = 0.8 - 0.6 * math.exp(-0.3 * 1)
    n_exp, _, d_ff = moe_w_gate[0].shape
    expert_w = [moe_w_gate[0].reshape(n_exp * d, d_ff), moe_w_up[0].reshape(n_exp * d, d_ff),
                moe_w_down[0].reshape(n_exp * d_ff, d)]
    main, wg_bf, wu_bf, wd_bf = _diff_attn(
        hk, vt, row(b_lambda_q1[0]), row(b_lambda_k1[0]), row(b_lambda_q2[0]), row(b_lambda_k2[0]),
        row(b_subln_g[0]), expert_w, bsz, seq, lambda_init, t["attn_q"], t["attn_k"])
    mem_out = _mem_attn_b(hk, memkv, seq, t["mem"])
    (x3,) = _proj_ln(main, 0, mem_out, 0, w_mix_out[1].astype(BF16), x2,
                     row(ln_g[1, 0]), row(ln_b[1, 0]), t["proj"], with_bf16=False)

    moe_tile = t["moe"]
    n_row_tiles = (2 * n) // moe_tile + N_EXPERTS
    w_router_pad = jnp.pad(moe_w_router[0], ((0, 0), (0, LANES - N_EXPERTS)))
    pos, gates, tile_info = _route(x3, w_router_pad, t["route"], moe_tile)
    tile_e = tile_info[0:8].reshape(-1)[:n_row_tiles]
    n_valid = tile_info[8, 0:1]
    tc = t["comb"]
    pos_flat = pos[:, :2].reshape(n // tc, 1, 2 * tc)
    fill_rows = jnp.concatenate([tile_info[16, :N_EXPERTS], n_valid * moe_tile])
    xs = _dispatch(fill_rows, pos_flat, x3, n_row_tiles * moe_tile, moe_tile, tc)
    ys = _experts(tile_e, n_valid, xs, wg_bf.reshape(n_exp, d, d_ff), wu_bf.reshape(n_exp, d, d_ff),
                  wd_bf.reshape(n_exp, d_ff, d), moe_tile, t["moe_f"])
    x4 = _combine(pos_flat, gates, x3, row(ln_g[1, 1]), row(ln_b[1, 1]), ys, tc)
    return x4.reshape(bsz, seq, d)
```

```python
import functools
import math

import numpy as np
import jax
import jax.numpy as jnp
from jax import lax
from jax.experimental import pallas as pl
from jax.experimental.pallas import tpu as pltpu

BF16 = jnp.bfloat16
F32 = jnp.float32

D_MODEL = 2048
MEM_WIDTH = 512
MAIN_WIDTH = 1536
MEM_HEADS = 4
MEM_HEAD_DIM = 128
CHUNK = 128
SG_GROUPS = 12
QK_DIM = 128
V_DIM = 256
DIFF_HEADS = 6
N_EXPERTS = 8
DEPTH = 2
ALPHA = (2.0 * DEPTH) ** 0.25
LN_EPS = 1e-5
LANES = 128
SUBLANES = 8
ONES_ROWS = 16
V_AUG = V_DIM + ONES_ROWS
NEG_BIG = -1e30
DMA_LOOP_UNROLL = 8
MIB = 1024 * 1024
V7X_VMEM_MIB = 64
VMEM_LARGE_MIB = V7X_VMEM_MIB - 8
VMEM_MID_MIB = V7X_VMEM_MIB - 16
VMEM_SMALL_MIB = V7X_VMEM_MIB // 2


def _tiles(n_tokens, seq):
    def fit(pref, total):
        t = min(pref, total)
        while total % t:
            t //= 2
        return t
    return dict(
        mixer=fit(512, seq),
        proj=fit(512, n_tokens),
        ffn=fit(512, n_tokens),
        ffn_f=512,
        mm=fit(1024, n_tokens),
        mm_n=1792,
        mm_t=fit(512, n_tokens),
        attn_q=fit(4096, seq),
        attn_k=fit(1024, seq),
        mem=fit(512, seq),
        route=fit(512, n_tokens),
        moe=fit(512, n_tokens),
        moe_f=1024,
        comb=fit(512, n_tokens),
    )


def _params(sem, vmem_mib):
    return pltpu.CompilerParams(dimension_semantics=sem, vmem_limit_bytes=vmem_mib * MIB)


def _const_spec(shape, index_map):
    return pl.BlockSpec(shape, index_map, pipeline_mode=pl.Buffered(1))


def _dot(a, b):
    return jnp.dot(a, b, preferred_element_type=F32)


def _dot_nt(a, b):
    return lax.dot_general(a, b, (((1,), (1,)), ((), ())), preferred_element_type=F32)


def _layer_norm(v, g, b):
    mu = jnp.mean(v, axis=-1, keepdims=True)
    c = v - mu
    var = jnp.mean(c * c, axis=-1, keepdims=True)
    return c * lax.rsqrt(var + LN_EPS) * g + b


def _gelu_tanh(v):
    return 0.5 * v * (1.0 + jnp.tanh(math.sqrt(2.0 / math.pi) * (v + 0.044715 * (v * v * v))))


def _silu(v):
    return v / (1.0 + jnp.exp(-v))


def _mem_attention(q, kv_ref, o_ref, col0):
    scale = MEM_HEAD_DIM ** -0.5
    for h in range(MEM_HEADS):
        lo = h * MEM_HEAD_DIM
        qh = q[:, lo:lo + MEM_HEAD_DIM].astype(BF16)
        kh = kv_ref[:, lo:lo + MEM_HEAD_DIM]
        vh = kv_ref[:, MEM_WIDTH + lo:MEM_WIDTH + lo + MEM_HEAD_DIM]
        s = _dot_nt(qh, kh) * scale
        e = jnp.exp(s - jnp.max(s, axis=-1, keepdims=True))
        o = _dot(e.astype(BF16), vh) / jnp.sum(e, axis=-1, keepdims=True)
        o_ref[:, col0 + lo:col0 + lo + MEM_HEAD_DIM] = o.astype(o_ref.dtype)


def _memkv_kernel(mem_ref, w_ref, o_ref):
    o_ref[...] = _dot(mem_ref[...].astype(BF16), w_ref[...]).astype(o_ref.dtype)


def _memkv(mem, w_mem_kv_bf):
    bsz, n_mem, d = mem.shape
    depth, _, cols = w_mem_kv_bf.shape
    return pl.pallas_call(
        _memkv_kernel,
        grid=(depth, bsz),
        in_specs=[pl.BlockSpec((None, n_mem, d), lambda l, b: (b, 0, 0)),
                  pl.BlockSpec((None, d, cols), lambda l, b: (l, 0, 0))],
        out_specs=pl.BlockSpec((None, None, n_mem, cols), lambda l, b: (l, b, 0, 0)),
        out_shape=jax.ShapeDtypeStruct((depth, bsz, n_mem, cols), BF16),
        compiler_params=_params(("arbitrary", "arbitrary"), VMEM_SMALL_MIB),
        name="memkv",
    )(mem, w_mem_kv_bf)


def _mixer_a_kernel(x_ref, w_ref, vg_ref, vb_ref, ws_ref, bs_ref, kv_ref, o_ref):
    tm = x_ref.shape[0]
    xb = x_ref[...].astype(BF16)
    v = _gelu_tanh(_dot(xb, w_ref[:, MAIN_WIDTH:2 * MAIN_WIDTH]))
    vn = _layer_norm(v, vg_ref[...], vb_ref[...]).astype(BF16)
    u = _gelu_tanh(_dot(xb, w_ref[:, :MAIN_WIDTH]))
    row = lax.broadcasted_iota(jnp.int32, (CHUNK, CHUNK), 0)
    col = lax.broadcasted_iota(jnp.int32, (CHUNK, CHUNK), 1)
    causal = col <= row
    for g in range(SG_GROUPS):
        wg = jnp.where(causal, ws_ref[g], 0.0).astype(BF16)
        bias = bs_ref[:, g:g + 1]
        for c in range(tm // CHUNK):
            rows = slice(c * CHUNK, (c + 1) * CHUNK)
            cols = slice(g * CHUNK, (g + 1) * CHUNK)
            s = _dot(wg, vn[rows, cols]) + bias
            o_ref[rows, cols] = (u[rows, cols] * s).astype(o_ref.dtype)
    q_mem = _dot(xb, w_ref[:, 2 * MAIN_WIDTH:])
    _mem_attention(q_mem, kv_ref, o_ref, MAIN_WIDTH)


def _mixer_a(x, w_in_bf, vnorm_g, vnorm_b, w_s, b_s_t, memkv, seq, tm):
    n, d = x.shape
    in_cols = w_in_bf.shape[1]
    return pl.pallas_call(
        _mixer_a_kernel,
        grid=(n // tm,),
        in_specs=[pl.BlockSpec((tm, d), lambda i: (i, 0)),
                  _const_spec((d, in_cols), lambda i: (0, 0)),
                  _const_spec((1, MAIN_WIDTH), lambda i: (0, 0)),
                  _const_spec((1, MAIN_WIDTH), lambda i: (0, 0)),
                  _const_spec((SG_GROUPS, CHUNK, CHUNK), lambda i: (0, 0, 0)),
                  _const_spec((CHUNK, SG_GROUPS), lambda i: (0, 0)),
                  pl.BlockSpec((None, None) + memkv.shape[2:], lambda i: (0, (i * tm) // seq, 0, 0))],
        out_specs=pl.BlockSpec((tm, d), lambda i: (i, 0)),
        out_shape=jax.ShapeDtypeStruct((n, d), BF16),
        compiler_params=_params(("arbitrary",), VMEM_LARGE_MIB),
        name="mixer_a",
    )(x, w_in_bf, vnorm_g, vnorm_b, w_s, b_s_t, memkv)


def _proj_ln_kernel(a1_ref, a2_ref, w_ref, x_ref, g_ref, b_ref, o32_ref, *maybe_o16_ref, sub):
    for r in range(0, x_ref.shape[0], sub):
        rows = slice(r, r + sub)
        mix = _dot(a1_ref[rows, :], w_ref[:MAIN_WIDTH, :]) + _dot(a2_ref[rows, :], w_ref[MAIN_WIDTH:, :])
        y = _layer_norm(ALPHA * x_ref[rows, :] + mix, g_ref[...], b_ref[...])
        o32_ref[rows, :] = y
        for o16_ref in maybe_o16_ref:
            o16_ref[rows, :] = y.astype(BF16)


def _proj_ln(a_main, main_blk, a_mem, mem_blk, w_bf, x, g, b, tm, with_bf16):
    n, d = x.shape
    n_out = 2 if with_bf16 else 1
    return pl.pallas_call(
        functools.partial(_proj_ln_kernel, sub=min(tm, 2 * LANES)),
        grid=(n // tm,),
        in_specs=[pl.BlockSpec((tm, MAIN_WIDTH), lambda i: (i, main_blk)),
                  pl.BlockSpec((tm, MEM_WIDTH), lambda i: (i, mem_blk)),
                  _const_spec((d, d), lambda i: (0, 0)),
                  pl.BlockSpec((tm, d), lambda i: (i, 0)),
                  _const_spec((1, d), lambda i: (0, 0)),
                  _const_spec((1, d), lambda i: (0, 0))],
        out_specs=[pl.BlockSpec((tm, d), lambda i: (i, 0))] * n_out,
        out_shape=[jax.ShapeDtypeStruct((n, d), F32), jax.ShapeDtypeStruct((n, d), BF16)][:n_out],
        compiler_params=_params(("arbitrary",), VMEM_MID_MIB),
        name="proj_ln",
    )(a_main, a_mem, w_bf, x, g, b)


def _ffn_ln_kernel(x_ref, xb_ref, wg_ref, wu_ref, wd_ref, g_ref, b_ref, o32_ref, o16_ref, acc_ref):
    j = pl.program_id(1)

    @pl.when(j == 0)
    def _():
        acc_ref[...] = jnp.zeros_like(acc_ref)

    xb = xb_ref[...]
    h = (_silu(_dot(xb, wg_ref[...])) * _dot(xb, wu_ref[...])).astype(BF16)
    acc_ref[...] += _dot(h, wd_ref[...])

    @pl.when(j == pl.num_programs(1) - 1)
    def _():
        y = _layer_norm(ALPHA * x_ref[...] + acc_ref[...], g_ref[...], b_ref[...])
        o32_ref[...] = y
        o16_ref[...] = y.astype(BF16)


def _ffn_ln(x, xb, wg_bf, wu_bf, wd_bf, g, b, tm, tf):
    n, d = x.shape
    f = wg_bf.shape[1]
    return pl.pallas_call(
        _ffn_ln_kernel,
        grid=(n // tm, f // tf),
        in_specs=[pl.BlockSpec((tm, d), lambda i, j: (i, 0)),
                  pl.BlockSpec((tm, d), lambda i, j: (i, 0)),
                  pl.BlockSpec((d, tf), lambda i, j: (0, j)),
                  pl.BlockSpec((d, tf), lambda i, j: (0, j)),
                  pl.BlockSpec((tf, d), lambda i, j: (j, 0)),
                  _const_spec((1, d), lambda i, j: (0, 0)),
                  _const_spec((1, d), lambda i, j: (0, 0))],
        out_specs=[pl.BlockSpec((tm, d), lambda i, j: (i, 0)),
                   pl.BlockSpec((tm, d), lambda i, j: (i, 0))],
        out_shape=[jax.ShapeDtypeStruct((n, d), F32), jax.ShapeDtypeStruct((n, d), BF16)],
        scratch_shapes=[pltpu.VMEM((tm, d), F32)],
        compiler_params=_params(("arbitrary", "arbitrary"), VMEM_LARGE_MIB),
        name="ffn_ln",
    )(x, xb, wg_bf, wu_bf, wd_bf, g, b)


def _matmul_kernel(x_ref, w_ref, s_ref, o_ref):
    o_ref[...] = (_dot(x_ref[...], w_ref[...]) * s_ref[...]).astype(o_ref.dtype)


def _matmul_colscale(xb, w_bf, col_scale, tm, tn):
    n, d = xb.shape
    cols = w_bf.shape[1]
    return pl.pallas_call(
        _matmul_kernel,
        grid=(n // tm, cols // tn),
        in_specs=[pl.BlockSpec((tm, d), lambda i, j: (i, 0)),
                  pl.BlockSpec((d, tn), lambda i, j: (0, j)),
                  pl.BlockSpec((1, tn), lambda i, j: (0, j))],
        out_specs=pl.BlockSpec((tm, tn), lambda i, j: (i, j)),
        out_shape=jax.ShapeDtypeStruct((n, cols), BF16),
        compiler_params=_params(("arbitrary", "arbitrary"), VMEM_MID_MIB),
        name="in_proj_b",
    )(xb, w_bf, col_scale)


def _proj_t_kernel(wt_ref, bias_ref, x_ref, o_ref):
    o_ref[...] = (_dot_nt(wt_ref[...], x_ref[...]) + bias_ref[...]).astype(o_ref.dtype)


def _proj_transposed(xb, wt_bf, bias_col, tn):
    n, d = xb.shape
    cols = wt_bf.shape[0]
    return pl.pallas_call(
        _proj_t_kernel,
        grid=(n // tn,),
        in_specs=[_const_spec((cols, d), lambda i: (0, 0)),
                  _const_spec((cols, 1), lambda i: (0, 0)),
                  pl.BlockSpec((tn, d), lambda i: (i, 0))],
        out_specs=pl.BlockSpec((cols, tn), lambda i: (0, i)),
        out_shape=jax.ShapeDtypeStruct((cols, n), BF16),
        compiler_params=_params(("arbitrary",), VMEM_MID_MIB),
        name="v_proj_t",
    )(wt_bf, bias_col, xb)


def _diff_attn_kernel(qi_tab, kj_tab, q1_ref, q2_ref, k1_ref, k2_ref, vt_ref,
                      lq1_ref, lk1_ref, lq2_ref, lk2_ref, sg_ref, *rest, lambda_init, qw, n_cast):
    cast_src = rest[:n_cast]
    o_ref = rest[n_cast]
    cast_dst = rest[n_cast + 1:2 * n_cast + 1]
    m1_ref, a1_ref, m2_ref, a2_ref = rest[2 * n_cast + 1:]
    tq, tk = q1_ref.shape[0], k1_ref.shape[0]
    p = pl.program_id(2)
    qi = qi_tab[p]
    kj = kj_tab[p]

    @pl.when(kj == 0)
    def _():
        for m_ref, a_ref in ((m1_ref, a1_ref), (m2_ref, a2_ref)):
            m_ref[...] = jnp.full_like(m_ref, NEG_BIG)
            a_ref[...] = jnp.zeros_like(a_ref)

    chains = [(q_ref, k_ref, m_ref, a_ref, j)
              for j in range(tq // qw)
              for q_ref, k_ref, m_ref, a_ref in ((q1_ref, k1_ref, m1_ref, a1_ref),
                                                 (q2_ref, k2_ref, m2_ref, a2_ref))]

    def step(key0):
        if key0 is None:
            live = chains
        else:
            live = [c for c in chains if key0 <= (c[4] + 1) * qw - 1]

        def n_keys(j):
            return tk if key0 is None else min(tk, (j + 1) * qw - key0)

        def scores(chain):
            q_ref, k_ref, j = chain[0], chain[1], chain[4]
            nk = n_keys(j)
            s = _dot_nt(k_ref[0:nk, :], q_ref[j * qw:(j + 1) * qw, :])
            if key0 is not None and key0 + nk - 1 > j * qw:
                key = key0 + lax.broadcasted_iota(jnp.int32, (nk, qw), 0)
                qry = j * qw + lax.broadcasted_iota(jnp.int32, (nk, qw), 1)
                s = jnp.where(key <= qry, s, NEG_BIG)
            return s.astype(BF16)

        def softmax(chain, s):
            m_ref, j = chain[2], chain[4]
            cols = slice(j * qw, (j + 1) * qw)
            m_old = m_ref[:, cols]
            m_new = jnp.maximum(m_old, jnp.max(s, axis=0, keepdims=True).astype(F32))
            alpha = jnp.exp(m_old - m_new)
            e = jnp.exp(s - m_new.astype(BF16))
            m_ref[:, cols] = m_new
            return e, alpha

        def weighted_values(chain, e, alpha):
            a_ref, j = chain[3], chain[4]
            cols = slice(j * qw, (j + 1) * qw)
            a_ref[:, cols] = alpha * a_ref[:, cols] + _dot(vt_ref[:, 0:n_keys(j)], e)

        n = len(live)
        s_live, e_live = {}, {}
        for t in range(n + 2):
            if t < len(cast_src):
                cast_dst[t][...] = cast_src[t][...].astype(BF16)
            if t < n:
                s_live[t] = scores(live[t])
            if 0 <= t - 1 < n:
                e_live[t - 1] = softmax(live[t - 1], s_live.pop(t - 1))
            if 0 <= t - 2 < n:
                weighted_values(live[t - 2], *e_live.pop(t - 2))

    key0 = kj * tk - qi * tq
    pl.when(key0 < 0)(lambda: step(None))
    for static_key0 in range(0, tq, tk):
        pl.when(key0 == static_key0)(functools.partial(step, static_key0))

    @pl.when(kj == ((qi + 1) * tq - 1) // tk)
    def _():
        lam = (jnp.exp(jnp.sum(lq1_ref[...] * lk1_ref[...], axis=-1, keepdims=True))
               - jnp.exp(jnp.sum(lq2_ref[...] * lk2_ref[...], axis=-1, keepdims=True))
               + lambda_init)
        o1 = a1_ref[0:V_DIM, :] / a1_ref[V_DIM:V_DIM + 1, :]
        o2 = a2_ref[0:V_DIM, :] / a2_ref[V_DIM:V_DIM + 1, :]
        o = o1 - lam * o2
        o = o * lax.rsqrt(jnp.mean(o * o, axis=0, keepdims=True) + LN_EPS) * sg_ref[...]
        o_ref[...] = jnp.transpose(o * (1.0 - lambda_init)).astype(o_ref.dtype)


def _diff_attn(hk, vt, lq1, lk1, lq2, lk2, subln_g, to_cast, bsz, seq, lambda_init, tq, tk):
    n = hk.shape[0]
    assert tq % tk == 0 and tk >= 2
    nq = seq // tq
    pairs =[(qi, kj) for qi in range(nq) for kj in range(((qi + 1) * tq - 1) // tk + 1)]
    qi_tab = jnp.asarray(np.array([p[0] for p in pairs], np.int32))
    kj_tab = jnp.asarray(np.array([p[1] for p in pairs], np.int32))
    nqb, nkb = seq // tq, seq // tk
    n_pairs = len(pairs)
    n_steps = bsz * DIFF_HEADS * n_pairs
    k_blk0 = (MAIN_WIDTH + MEM_WIDTH) // QK_DIM
    qspec = lambda off: pl.BlockSpec((tq, QK_DIM), lambda b, h, p, qt, kt: (b * nqb + qt[p], off + h))
    kspec = lambda off: pl.BlockSpec((tk, QK_DIM), lambda b, h, p, qt, kt: (b * nkb + kt[p], off + h))
    vec = lambda w: pl.BlockSpec((1, w), lambda b, h, p, qt, kt: (0, 0))

    def cast_spec(arr):
        rows, cols = arr.shape
        n_blk = 1
        while n_blk * 2 <= n_steps and rows % (n_blk * 2) == 0 and (rows // (n_blk * 2)) % 16 == 0:
            n_blk *= 2
        return pl.BlockSpec((rows // n_blk, cols), lambda b, h, p, qt, kt: (
            jnp.minimum((b * DIFF_HEADS + h) * n_pairs + p, n_blk - 1), 0))

    cast_specs = [cast_spec(a) for a in to_cast]
    grid_spec = pltpu.PrefetchScalarGridSpec(
        num_scalar_prefetch=2,
        grid=(bsz, DIFF_HEADS, n_pairs),
        in_specs=[qspec(0), qspec(DIFF_HEADS), kspec(k_blk0), kspec(k_blk0 + DIFF_HEADS),
                  pl.BlockSpec((V_AUG, tk), lambda b, h, p, qt, kt: (h, b * nkb + kt[p])),
                  vec(QK_DIM), vec(QK_DIM), vec(QK_DIM), vec(QK_DIM),
                  pl.BlockSpec((V_DIM, 1), lambda b, h, p, qt, kt: (0, 0))] + cast_specs,
        out_specs=[pl.BlockSpec((tq, V_DIM), lambda b, h, p, qt, kt: (b * nqb + qt[p], h))] + cast_specs,
        scratch_shapes=[pltpu.VMEM((1, tq), F32), pltpu.VMEM((V_AUG, tq), F32),
                        pltpu.VMEM((1, tq), F32), pltpu.VMEM((V_AUG, tq), F32)],
    )
    return pl.pallas_call(
        functools.partial(_diff_attn_kernel, lambda_init=lambda_init, qw=min(tq, 2 * LANES),
                          n_cast=len(to_cast)),
        grid_spec=grid_spec,
        out_shape=[jax.ShapeDtypeStruct((n, MAIN_WIDTH), BF16)]
                  + [jax.ShapeDtypeStruct(a.shape, BF16) for a in to_cast],
        compiler_params=_params(("arbitrary", "arbitrary", "arbitrary"), VMEM_MID_MIB),
        name="diff_attn",
    )(qi_tab, kj_tab, hk, hk, hk, hk, vt, lq1, lk1, lq2, lk2, subln_g.reshape(V_DIM, 1), *to_cast)


def _mem_attn_kernel(q_ref, kv_ref, o_ref):
    _mem_attention(q_ref[...].astype(F32), kv_ref, o_ref, 0)


def _mem_attn_b(hk, memkv, seq, tm):
    n = hk.shape[0]
    return pl.pallas_call(
        _mem_attn_kernel,
        grid=(n // tm,),
        in_specs=[pl.BlockSpec((tm, MEM_WIDTH), lambda i: (i, MAIN_WIDTH // MEM_WIDTH)),
                  pl.BlockSpec((None, None) + memkv.shape[2:], lambda i: (1, (i * tm) // seq, 0, 0))],
        out_specs=pl.BlockSpec((tm, MEM_WIDTH), lambda i: (i, 0)),
        out_shape=jax.ShapeDtypeStruct((n, MEM_WIDTH), BF16),
        compiler_params=_params(("arbitrary",), VMEM_SMALL_MIB),
        name="mem_attn_b",
    )(hk, memkv)


def _lane_cumsum(v, lane):
    for shift in (1, 2, 4):
        v = v + jnp.where(lane >= shift, pltpu.roll(v, shift, 1), 0.0)
    return v


def _route_kernel(x_ref, wr_ref, pos_ref, gate_ref, tile_e_ref, cnt_ref, base_ref, start_ref, top2_ref,
                  *, moe_tile):
    phase = pl.program_id(0)
    i = pl.program_id(1)
    tm = x_ref.shape[0]
    lane = lax.broadcasted_iota(jnp.int32, (tm, LANES), 1).astype(F32)

    @pl.when(jnp.logical_and(phase == 0, i == 0))
    def _():
        cnt_ref[...] = jnp.zeros_like(cnt_ref)

    @pl.when(phase == 0)
    def _():
        x = x_ref[...]
        xh = x.astype(BF16)
        xl = (x - xh.astype(F32)).astype(BF16)
        w = wr_ref[...]
        wh = w.astype(BF16)
        wl = (w - wh.astype(F32)).astype(BF16)
        logits = _dot(xh, wh) + (_dot(xh, wl) + _dot(xl, wh))
        logits = jnp.where(lane < N_EXPERTS, logits, NEG_BIG)
        v1 = jnp.max(logits, axis=-1, keepdims=True)
        i1 = jnp.min(jnp.where(logits == v1, lane, float(LANES)), axis=-1, keepdims=True)
        rest = jnp.where(lane == i1, NEG_BIG, logits)
        v2 = jnp.max(rest, axis=-1, keepdims=True)
        i2 = jnp.min(jnp.where(rest == v2, lane, float(LANES)), axis=-1, keepdims=True)
        top2_ref[i] = jnp.where(lane == 0, i1, jnp.where(lane == 1, i2, jnp.where(lane == 2, v1, v2)))
        sel = jnp.logical_or(lane == i1, lane == i2)
        cnt_ref[...] += jnp.sum(sel.astype(F32), axis=0, keepdims=True)

    @pl.when(jnp.logical_and(phase == 1, i == 0))
    def _():
        lane8 = lax.broadcasted_iota(jnp.int32, (8, LANES), 1).astype(F32)
        sub8 = lax.broadcasted_iota(jnp.int32, (8, LANES), 0).astype(F32)
        cnt = jnp.broadcast_to(cnt_ref[...], (8, LANES))
        padded = jnp.ceil(cnt * (1.0 / moe_tile)) * moe_tile
        ends = _lane_cumsum(padded, lane8)
        start_ref[...] = (ends - padded)[0:1, :]
        base_ref[...] = jnp.zeros_like(base_ref)
        tile_row0 = (sub8 * LANES + lane8) * moe_tile
        tile_e = jnp.zeros((8, LANES), F32)
        for e in range(N_EXPERTS):
            end_e = jnp.sum(jnp.where(lane8 == e, ends, 0.0), axis=-1, keepdims=True)
            tile_e = tile_e + (tile_row0 >= end_e).astype(F32)
        tile_e_ref[0:8, :] = jnp.minimum(tile_e, N_EXPERTS - 1.0).astype(jnp.int32)
        total = jnp.sum(jnp.where(lane8 == N_EXPERTS - 1, ends, 0.0), axis=-1, keepdims=True)
        tile_e_ref[8:16, :] = jnp.broadcast_to(total * (1.0 / moe_tile), (8, LANES)).astype(jnp.int32)
        tile_e_ref[16:24, :] = (ends - padded + cnt).astype(jnp.int32)

    @pl.when(phase == 1)
    def _():
        rec = top2_ref[i]
        i1, i2, v1, v2 = rec[:, 0:1], rec[:, 1:2], rec[:, 2:3], rec[:, 3:4]
        sel = jnp.logical_or(lane == i1, lane == i2)
        tile_cnt = jnp.sum(sel.astype(F32), axis=0, keepdims=True)
        r = lax.broadcasted_iota(jnp.int32, (tm, tm), 0)
        c = lax.broadcasted_iota(jnp.int32, (tm, tm), 1)
        before = (c < r).astype(BF16)
        rank = _dot(before, sel.astype(BF16))
        slot = start_ref[...] + base_ref[...] + rank
        p1 = jnp.sum(jnp.where(lane == i1, slot, 0.0), axis=-1, keepdims=True)
        p2 = jnp.sum(jnp.where(lane == i2, slot, 0.0), axis=-1, keepdims=True)
        pos_ref[...] = jnp.where(lane == 0, p1, jnp.where(lane == 1, p2, 0.0)).astype(jnp.int32)
        g1 = 1.0 / (1.0 + jnp.exp(v2 - v1))
        g2 = jnp.exp(v2 - v1) * g1
        gate_ref[...] = jnp.where(lane == 0, g1, jnp.where(lane == 1, g2, 0.0))
        base_ref[...] += tile_cnt


def _route(x, w_router_pad, tm, moe_tile):
    n, d = x.shape
    nt = n // tm
    return pl.pallas_call(
        functools.partial(_route_kernel, moe_tile=moe_tile),
        grid=(2, nt),
        in_specs=[pl.BlockSpec((tm, d), lambda ph, i: (i * (1 - ph) + (nt - 1) * ph, 0)),
                  _const_spec((d, LANES), lambda ph, i: (0, 0))],
        out_specs=[pl.BlockSpec((tm, LANES), lambda ph, i: (i * ph, 0)),
                   pl.BlockSpec((tm, LANES), lambda ph, i: (i * ph, 0)),
                   pl.BlockSpec((24, LANES), lambda ph, i: (0, 0))],
        out_shape=[jax.ShapeDtypeStruct((n, LANES), jnp.int32),
                   jax.ShapeDtypeStruct((n, LANES), F32),
                   jax.ShapeDtypeStruct((24, LANES), jnp.int32)],
        scratch_shapes=[pltpu.VMEM((1, LANES), F32), pltpu.VMEM((1, LANES), F32), pltpu.VMEM((1, LANES), F32),
                        pltpu.VMEM((nt, tm, LANES), F32)],
        compiler_params=_params(("arbitrary", "arbitrary"), VMEM_MID_MIB),
        name="route",
    )(x, w_router_pad)


def _dispatch_kernel(pad_row0, pos_ref, x_ref, xs_ref, zero_ref, sem, *, moe_tile):
    tm = x_ref.shape[0] * SUBLANES

    @pl.when(pl.program_id(0) == 0)
    def _():
        zero_ref[...] = jnp.zeros_like(zero_ref)
        zr = zero_ref.shape[0]

        def drain_fill(r, carry):
            pltpu.make_async_copy(zero_ref.at[pl.ds(0, 1)], xs_ref.at[pl.ds(0, 1)], sem).wait()
            return carry

        for e in range(N_EXPERTS):
            def fill(r, carry, e=e):
                pltpu.make_async_copy(zero_ref.at[pl.ds(0, 1)],
                                      xs_ref.at[pl.ds(pad_row0[e] + r, 1)], sem).start()
                return carry

            lax.fori_loop(0, moe_tile, fill, 0, unroll=DMA_LOOP_UNROLL)
            lax.fori_loop(0, moe_tile, drain_fill, 0, unroll=2 * DMA_LOOP_UNROLL)

        n_alloc = xs_ref.shape[0]
        tail = [pltpu.make_async_copy(
                    zero_ref,
                    xs_ref.at[pl.ds(pl.multiple_of(
                        jnp.minimum(pad_row0[N_EXPERTS] + c * zr, n_alloc - zr), zr), zr)], sem)
                for c in range((N_EXPERTS + 1) * moe_tile // zr)]
        for f in tail:
            f.start()
            f.wait()

    def issue(g, carry):
        for u in range(SUBLANES):
            for k in range(2):
                dst = pos_ref[0, 0, 2 * SUBLANES * g + 2 * u + k]
                pltpu.make_async_copy(x_ref.at[g, pl.ds(u, 1)], xs_ref.at[pl.ds(dst, 1)], sem).start(priority=k)
        return carry

    lax.fori_loop(0, tm // SUBLANES, issue, 0)

    def drain(t, carry):
        pltpu.make_async_copy(x_ref.at[0, pl.ds(0, 1)], xs_ref.at[pl.ds(0, 1)], sem).wait()
        return carry

    lax.fori_loop(0, 2 * tm, drain, 0, unroll=2 * DMA_LOOP_UNROLL)


def _dispatch(pad_row0, pos_flat, x, n_rows, moe_tile, tm):
    n, d = x.shape
    zero_rows = min(moe_tile, LANES)
    grid_spec = pltpu.PrefetchScalarGridSpec(
        num_scalar_prefetch=1,
        grid=(n // tm,),
        in_specs=[pl.BlockSpec((1, 1, 2 * tm), lambda i, pr: (i, 0, 0), memory_space=pltpu.SMEM),
                  pl.BlockSpec((tm // SUBLANES, SUBLANES, d), lambda i, pr: (i, 0, 0))],
        out_specs=pl.BlockSpec(memory_space=pl.ANY),
        scratch_shapes=[pltpu.VMEM((zero_rows, d), F32), pltpu.SemaphoreType.DMA(())],
    )
    return pl.pallas_call(
        functools.partial(_dispatch_kernel, moe_tile=moe_tile),
        grid_spec=grid_spec,
        out_shape=jax.ShapeDtypeStruct((n_rows + moe_tile, d), F32),
        compiler_params=_params(("arbitrary",), VMEM_SMALL_MIB),
        name="moe_dispatch",
    )(pad_row0, pos_flat, x.reshape(n // SUBLANES, SUBLANES, d))


def _experts_kernel(tile_e, n_valid, xs_ref, wg_ref, wu_ref, wd_ref, o_ref, xb_ref):
    del tile_e
    i = pl.program_id(0)
    j = pl.program_id(1)
    valid = i < n_valid[0]

    @pl.when(j == 0)
    def _():
        xb_ref[...] = xs_ref[...].astype(BF16)
        o_ref[...] = jnp.zeros_like(o_ref)

    @pl.when(valid)
    def _():
        xb = xb_ref[...]
        h = (_silu(_dot(xb, wg_ref[...])) * _dot(xb, wu_ref[...])).astype(BF16)
        o_ref[...] += _dot(h, wd_ref[...])


def _experts(tile_e, n_valid, xs, wg_bf, wu_bf, wd_bf, tm, tf):
    d = xs.shape[1]
    m = tile_e.shape[0] * tm
    f = wg_bf.shape[2]
    nj = f // tf

    def row_idx(i, j, te, nv):
        return (jnp.minimum(i, nv[0] - 1), 0)

    def col_of(i, j, nv):
        return jnp.where(i < nv[0], j, nj - 1)

    grid_spec = pltpu.PrefetchScalarGridSpec(
        num_scalar_prefetch=2,
        grid=(m // tm, nj),
        in_specs=[pl.BlockSpec((tm, d), row_idx),
                  pl.BlockSpec((None, d, tf), lambda i, j, te, nv: (te[i], 0, col_of(i, j, nv))),
                  pl.BlockSpec((None, d, tf), lambda i, j, te, nv: (te[i], 0, col_of(i, j, nv))),
                  pl.BlockSpec((None, tf, d), lambda i, j, te, nv: (te[i], col_of(i, j, nv), 0))],
        out_specs=pl.BlockSpec((tm, d), lambda i, j, te, nv: (i, 0)),
        scratch_shapes=[pltpu.VMEM((tm, d), BF16)],
    )
    return pl.pallas_call(
        _experts_kernel,
        grid_spec=grid_spec,
        out_shape=jax.ShapeDtypeStruct((m, d), F32),
        compiler_params=_params(("arbitrary", "arbitrary"), VMEM_LARGE_MIB),
        name="moe_experts",
    )(tile_e, n_valid, xs, wg_bf, wu_bf, wd_bf)


def _combine_kernel(pos_ref, gate_ref, x_ref, g_ref, b_ref, ys_ref, o_ref, buf_ref, sem):
    tm = x_ref.shape[0]

    def issue(g, carry):
        for u in range(SUBLANES):
            for k in range(2):
                src = pos_ref[0, 0, 2 * SUBLANES * g + 2 * u + k]
                pltpu.make_async_copy(ys_ref.at[pl.ds(src, 1)], buf_ref.at[k, g, pl.ds(u, 1)],
                                      sem).start(priority=k)
        return carry

    lax.fori_loop(0, tm // SUBLANES, issue, 0)

    def drain(t, carry):
        pltpu.make_async_copy(ys_ref.at[pl.ds(0, 1)], buf_ref.at[0, 0, pl.ds(0, 1)], sem).wait()
        return carry

    lax.fori_loop(0, 2 * tm, drain, 0, unroll=2 * DMA_LOOP_UNROLL)

    gates = gate_ref[...]
    d = x_ref.shape[1]
    y = gates[:, 0:1] * buf_ref[0].reshape(tm, d) + gates[:, 1:2] * buf_ref[1].reshape(tm, d)
    o_ref[...] = _layer_norm(ALPHA * x_ref[...] + y, g_ref[...], b_ref[...])


def _combine(pos_flat, gates, x, g, b, ys, tm):
    n, d = x.shape
    return pl.pallas_call(
        _combine_kernel,
        grid=(n // tm,),
        in_specs=[pl.BlockSpec((1, 1, 2 * tm), lambda i: (i, 0, 0), memory_space=pltpu.SMEM),
                  pl.BlockSpec((tm, LANES), lambda i: (i, 0)),
                  pl.BlockSpec((tm, d), lambda i: (i, 0)),
                  _const_spec((1, d), lambda i: (0, 0)),
                  _const_spec((1, d), lambda i: (0, 0)),
                  pl.BlockSpec(memory_space=pl.ANY)],
        out_specs=pl.BlockSpec((tm, d), lambda i: (i, 0)),
        out_shape=jax.ShapeDtypeStruct((n, d), F32),
        scratch_shapes=[pltpu.VMEM((2, tm // SUBLANES, SUBLANES, d), F32), pltpu.SemaphoreType.DMA(())],
        compiler_params=_params(("arbitrary",), VMEM_SMALL_MIB),
        name="moe_combine",
    )(pos_flat, gates, x, g, b, ys)


def kernel(x, mem, ln_g, ln_b, w_mix_out, w_mem_kv, a_w_in, a_vnorm_g, a_vnorm_b, a_w_s, a_b_s,
           shared_w_kv, b_w_in, b_lambda_q1, b_lambda_k1, b_lambda_q2, b_lambda_k2, b_subln_g,
           ffn_w_gate, ffn_w_up, ffn_w_down, moe_w_router, moe_w_gate, moe_w_up, moe_w_down):
    bsz, seq, d = x.shape
    n = bsz * seq
    t = _tiles(n, seq)
    xf = x.reshape(n, d)
    row = lambda v: v.reshape(1, -1)

    memkv = _memkv(mem, w_mem_kv.astype(BF16))

    mixed = _mixer_a(xf, a_w_in[0].astype(BF16), row(a_vnorm_g[0]), row(a_vnorm_b[0]),
                     a_w_s[0], jnp.transpose(a_b_s[0]), memkv, seq, t["mixer"])
    x1, x1b = _proj_ln(mixed, 0, mixed, MAIN_WIDTH // MEM_WIDTH, w_mix_out[0].astype(BF16), xf,
                       row(ln_g[0, 0]), row(ln_b[0, 0]), t["proj"], with_bf16=True)
    x2, x2b = _ffn_ln(x1, x1b, ffn_w_gate[0].astype(BF16), ffn_w_up[0].astype(BF16),
                      ffn_w_down[0].astype(BF16), row(ln_g[0, 1]), row(ln_b[0, 1]), t["ffn"], t["ffn_f"])

    w_cat = jnp.concatenate([b_w_in[0], shared_w_kv[:, :MAIN_WIDTH]], axis=1).astype(BF16)
    col_scale = jnp.concatenate([jnp.full((MAIN_WIDTH,), QK_DIM ** -0.5, F32),
                                 jnp.ones((w_cat.shape[1] - MAIN_WIDTH,), F32)]).reshape(1, -1)
    hk = _matmul_colscale(x2b, w_cat, col_scale, t["mm"], t["mm_n"])
    wv_t = jnp.transpose(shared_w_kv[:, MAIN_WIDTH:]).reshape(DIFF_HEADS, V_DIM, d)
    wv_t = jnp.pad(wv_t, ((0, 0), (0, ONES_ROWS), (0, 0))).reshape(DIFF_HEADS * V_AUG, d).astype(BF16)
    ones_bias = jnp.pad(jnp.zeros((DIFF_HEADS, V_DIM, 1), F32), ((0, 0), (0, ONES_ROWS), (0, 0)),
                        constant_values=1.0).reshape(DIFF_HEADS * V_AUG, 1)
    vt = _proj_transposed(x2b, wv_t, ones_bias, t["mm_t"])
    lambda_init = 0.8 - 0.6 * math.exp(-0.3 * 1)
    n_exp, _, d_ff = moe_w_gate[0].shape
    expert_w = [moe_w_gate[0].reshape(n_exp * d, d_ff), moe_w_up[0].reshape(n_exp * d, d_ff),
                moe_w_down[0].reshape(n_exp * d_ff, d)]
    main, wg_bf, wu_bf, wd_bf = _diff_attn(
        hk, vt, row(b_lambda_q1[0]), row(b_lambda_k1[0]), row(b_lambda_q2[0]), row(b_lambda_k2[0]),
        row(b_subln_g[0]), expert_w, bsz, seq, lambda_init, t["attn_q"], t["attn_k"])
    mem_out = _mem_attn_b(hk, memkv, seq, t["mem"])
    (x3,) = _proj_ln(main, 0, mem_out, 0, w_mix_out[1].astype(BF16), x2,
                     row(ln_g[1, 0]), row(ln_b[1, 0]), t["proj"], with_bf16=False)

    moe_tile = t["moe"]
    n_row_tiles = (2 * n) // moe_tile + N_EXPERTS
    w_router_pad = jnp.pad(moe_w_router[0], ((0, 0), (0, LANES - N_EXPERTS)))
    pos, gates, tile_info = _route(x3, w_router_pad, t["route"], moe_tile)
    tile_e = tile_info[0:8].reshape(-1)[:n_row_tiles]
    n_valid = tile_info[8, 0:1]
    tc = t["comb"]
    pos_flat = pos[:, :2].reshape(n // tc, 1, 2 * tc)
    fill_rows = jnp.concatenate([tile_info[16, :N_EXPERTS], n_valid * moe_tile])
    xs = _dispatch(fill_rows, pos_flat, x3, n_row_tiles * moe_tile, moe_tile, tc)
    ys = _experts(tile_e, n_valid, xs, wg_bf.reshape(n_exp, d, d_ff), wu_bf.reshape(n_exp, d, d_ff),
                  wd_bf.reshape(n_exp, d_ff, d), moe_tile, t["moe_f"])
    x4 = _combine(pos_flat, gates, x3, row(ln_g[1, 1]), row(ln_b[1, 1]), ys, tc)
    return x4.reshape(bsz, seq, d)
```
